```python
import math
import jax
import jax.numpy as jnp
from jax import lax
import numpy as np

D_MODEL = 1024
BATCH = 2
SEQ = 8192
DEPTH = 2

GRID_W = 64
CTX_LEN = 256
HEAD_DIM = 64
ROPE_THETA = 10000.0
NORM_EPS = 1e-6
Q_BLOCK = 128
N_DIRS = 2

SSD_HEADS = 4
SSD_HEAD_DIM = 64
SSD_GROUPS = 2
SSD_STATE = 128
SSD_CONV = 3
SSD_CHUNK = 128
D_SSD = SSD_HEADS * SSD_HEAD_DIM
SSD_XBC = D_SSD + 2 * SSD_GROUPS * SSD_STATE

GA_HEADS = 4
GA_KV_HEADS = 2
D_GA = GA_HEADS * HEAD_DIM

WA_HEADS = 4
WA_KV_HEADS = 2
WINDOW = 128
D_WA = WA_HEADS * HEAD_DIM

S5_GROUPS = 16
S5_GROUP_CH = 16
S5_STATE = 64
S5_MAX_RE = -1e-4
D_S5 = S5_GROUPS * S5_GROUP_CH

D_MIX = D_SSD + D_GA + D_WA + D_S5

D_FF = 2816
FFN_CONV = 3

IN_SIZES = (D_SSD, SSD_XBC, N_DIRS * SSD_HEADS,
            D_GA, GA_KV_HEADS * HEAD_DIM, GA_KV_HEADS * HEAD_DIM,
            D_WA, WA_KV_HEADS * HEAD_DIM, WA_KV_HEADS * HEAD_DIM,
            D_S5)
IN_OFFSETS = tuple(sum(IN_SIZES[:j + 1]) for j in range(len(IN_SIZES) - 1))
D_IN = sum(IN_SIZES)

kernel_name = 'hybrid_parallel_heads_prefix_dit'


def rms_norm(x, g):
    xf = x.astype(jnp.float32)
    y = xf * lax.rsqrt(jnp.mean(xf * xf, axis=-1, keepdims=True) + NORM_EPS)
    return (y * g.astype(jnp.float32)).astype(x.dtype)


def modulate(x, g, shift, scale):
    return rms_norm(x, g) * (1 + scale) + shift


def dwconv(u, w, b):
    k = w.shape[0]
    y = lax.conv_general_dilated(u, w[:, None, :].astype(u.dtype), (1,), [(k // 2, k // 2)],
                                 dimension_numbers=('NWC', 'WIO', 'NWC'),
                                 feature_group_count=u.shape[-1])
    return y + b


def heads(u, n):
    return u.reshape(u.shape[:-1] + (n, HEAD_DIM))


def axial_rope(seq):
    rows = seq // GRID_W
    row = jnp.broadcast_to(jnp.arange(rows, dtype=jnp.float32)[:, None], (rows, GRID_W)).reshape(-1)
    col = jnp.broadcast_to(jnp.arange(GRID_W, dtype=jnp.float32)[None, :], (rows, GRID_W)).reshape(-1)
    n_freq = HEAD_DIM // 4
    inv = ROPE_THETA ** (-jnp.arange(n_freq, dtype=jnp.float32) / n_freq)
    ang = jnp.concatenate([row[:, None] * inv, col[:, None] * inv], axis=-1)
    return jnp.cos(ang), jnp.sin(ang)


def apply_rope(x, cos, sin):
    half = HEAD_DIM // 2
    xf = x.astype(jnp.float32)
    x1, x2 = xf[..., :half], xf[..., half:]
    cos, sin = cos[None, :, None, :], sin[None, :, None, :]
    return jnp.concatenate([x1 * cos - x2 * sin, x2 * cos + x1 * sin], axis=-1).astype(x.dtype)


def global_attention(q, k, v, kc, vc):
    b, s, hq, dh = q.shape
    n_kv = k.shape[2]
    g = hq // n_kv
    nb = s // Q_BLOCK
    k_all = jnp.concatenate([kc, k], axis=1)
    v_all = jnp.concatenate([vc, v], axis=1)
    qb = jnp.moveaxis(q.reshape(b, nb, Q_BLOCK, n_kv, g, dh), 1, 0)

    def block(qi):
        sc = jnp.einsum('bqhgd,bkhd->bhgqk', qi, k_all, preferred_element_type=jnp.float32)
        p = jax.nn.softmax(sc, axis=-1).astype(v_all.dtype)
        return jnp.einsum('bhgqk,bkhd->bqhgd', p, v_all)

    o = lax.map(block, qb)
    return jnp.moveaxis(o, 0, 1).reshape(b, s, hq * dh)


def window_attention(q, k, v, kc, vc, sink):
    b, s, hq, dh = q.shape
    n_kv = k.shape[2]
    g = hq // n_kv
    nb = s // Q_BLOCK
    nw = 3 * Q_BLOCK
    n_ctx = kc.shape[1]

    def bands(u):
        up = jnp.pad(u, ((0, 0), (Q_BLOCK, Q_BLOCK), (0, 0), (0, 0))).reshape(b, nb + 2, Q_BLOCK, n_kv, dh)
        return jnp.concatenate([up[:, :-2], up[:, 1:-1], up[:, 2:]], axis=2)

    kw, vw = bands(k), bands(v)
    qb = q.reshape(b, nb, Q_BLOCK, n_kv, g, dh)
    s_loc = jnp.einsum('bnqhgd,bnkhd->bnhgqk', qb, kw, preferred_element_type=jnp.float32)
    qi = jnp.arange(Q_BLOCK)[:, None]
    kj = jnp.arange(nw)[None, :]
    band = (kj >= qi + Q_BLOCK - WINDOW) & (kj <= qi + Q_BLOCK + WINDOW)
    kpos = (jnp.arange(nb)[:, None] - 1) * Q_BLOCK + jnp.arange(nw)[None, :]
    valid = (kpos >= 0) & (kpos < s)
    mask = band[None] & valid[:, None, :]
    s_loc = jnp.where(mask[None, :, None, None], s_loc, -jnp.inf)
    s_ctx = jnp.einsum('bnqhgd,bkhd->bnhgqk', qb, kc, preferred_element_type=jnp.float32)
    s_sink = jnp.broadcast_to(sink.astype(jnp.float32).reshape(n_kv, g)[None, None, :, :, None, None],
                              s_ctx.shape[:-1] + (1,))
    p = jax.nn.softmax(jnp.concatenate([s_loc, s_ctx, s_sink], axis=-1), axis=-1)
    o = (jnp.einsum('bnhgqk,bnkhd->bnqhgd', p[..., :nw].astype(vw.dtype), vw)
         + jnp.einsum('bnhgqk,bkhd->bnqhgd', p[..., nw:nw + n_ctx].astype(vc.dtype), vc))
    return o.reshape(b, s, hq * dh)


def context_attention(qc, kc, vc, sink=None):
    b, l, hq, dh = qc.shape
    n_kv = kc.shape[2]
    g = hq // n_kv
    sc = jnp.einsum('bqhgd,bkhd->bhgqk', qc.reshape(b, l, n_kv, g, dh), kc,
                    preferred_element_type=jnp.float32)
    if sink is not None:
        s_sink = jnp.broadcast_to(sink.astype(jnp.float32).reshape(1, n_kv, g, 1, 1), sc.shape[:-1] + (1,))
        sc = jnp.concatenate([sc, s_sink], axis=-1)
    p = jax.nn.softmax(sc, axis=-1)[..., :l]
    o = jnp.einsum('bhgqk,bkhd->bqhgd', p.astype(vc.dtype), vc)
    return o.reshape(b, l, hq * dh)


def ssd_chunked(xh, dt, a, bm, cm, h0, want_y):
    b, t, g, e, p = xh.shape
    n = bm.shape[-1]
    nc = t // SSD_CHUNK
    f32 = jnp.float32
    shp = (b, nc, SSD_CHUNK)
    xdt = (xh.astype(f32) * dt[..., None]).reshape(shp + (g, e, p))
    a_cs = jnp.cumsum((dt * a).reshape(shp + (g, e)), axis=2)
    bm = bm.astype(f32).reshape(shp + (g, n))
    cm = cm.astype(f32).reshape(shp + (g, n))
    a_end = a_cs[:, :, -1]
    chunk_states = jnp.einsum('bclgn,bclgep->bcgepn', bm,
                              xdt * jnp.exp(a_end[:, :, None] - a_cs)[..., None])

    def carry(h, inp):
        dec, st = inp
        return h * dec[..., None, None] + st, h

    h_final, h_in = lax.scan(carry, h0, (jnp.moveaxis(jnp.exp(a_end), 1, 0),
                                         jnp.moveaxis(chunk_states, 1, 0)))
    if not want_y:
        return None, h_final
    h_in = jnp.moveaxis(h_in, 0, 1)
    a_l = jnp.moveaxis(a_cs, 2, -1)
    causal = jnp.tril(jnp.ones((SSD_CHUNK, SSD_CHUNK), dtype=bool))
    decay = jnp.exp(jnp.where(causal, a_l[..., :, None] - a_l[..., None, :], -jnp.inf))
    cb = jnp.einsum('bclgn,bcsgn->bcgls', cm, bm)
    y_diag = jnp.einsum('bcgels,bcsgep->bclgep', cb[:, :, :, None] * decay, xdt)
    y_off = jnp.einsum('bclgn,bcgepn->bclgep', cm, h_in) * jnp.exp(a_cs)[..., None]
    return (y_diag + y_off).reshape(b, t, g, e, p).astype(xh.dtype), h_final


def ssd_mixer(lat, ctx, conv_w, conv_b, a_log, dt_bias, d_skip, norm_g, need_ctx):
    hpg = SSD_HEADS // SSD_GROUPS
    f32 = jnp.float32

    def prep(z, xbc, dt_raw):
        b, t, _ = xbc.shape
        xbc = jax.nn.silu(dwconv(xbc, conv_w, conv_b))
        xs, bm, cm = jnp.split(xbc, [D_SSD, D_SSD + SSD_GROUPS * SSD_STATE], axis=-1)
        return (z, xs.reshape(b, t, SSD_GROUPS, hpg, SSD_HEAD_DIM),
                bm.reshape(b, t, SSD_GROUPS, SSD_STATE), cm.reshape(b, t, SSD_GROUPS, SSD_STATE),
                dt_raw.reshape(b, t, N_DIRS, SSD_GROUPS, hpg))

    zl, xl, bl, cl, dl = prep(*lat)
    zc, xc, bc, cc, dc = prep(*ctx)
    dsk = d_skip.reshape(SSD_GROUPS, hpg, 1)
    y_lat = dsk * xl
    y_ctx = dsk * xc if need_ctx else None
    for d in range(N_DIRS):
        a = -jnp.exp(a_log[d].astype(f32)).reshape(SSD_GROUPS, hpg)
        bias = dt_bias[d].astype(f32).reshape(SSD_GROUPS, hpg)

        def rev(u):
            return jnp.flip(u, axis=1) if d == 1 else u

        def run(xs, bm, cm, dtr, h0, want_y):
            dt = jax.nn.softplus(dtr[:, :, d].astype(f32) + bias)
            y, hf = ssd_chunked(rev(xs), rev(dt), a, rev(bm), rev(cm), h0, want_y)
            return (rev(y) if want_y else None), hf

        h0 = jnp.zeros((xc.shape[0], SSD_GROUPS, hpg, SSD_HEAD_DIM, SSD_STATE), f32)
        yc, hc = run(xc, bc, cc, dc, h0, need_ctx)
        yl, _ = run(xl, bl, cl, dl, hc, True)
        y_lat = y_lat + yl
        if need_ctx:
            y_ctx = y_ctx + yc

    def out(y, z):
        return rms_norm(y.reshape(z.shape) * jax.nn.silu(z), norm_g)

    return out(y_lat, zl), (out(y_ctx, zc) if need_ctx else None)


def _linear_recurrence(e1, e2):
    a1, b1 = e1
    a2, b2 = e2
    return a1 * a2, a2 * b1 + b2


def s5_mixer(u_lat, u_ctx, lam_re, lam_im, log_step, b_re, b_im, c_re, c_im, d_skip, w_glu, b_glu, need_ctx):
    f32 = jnp.float32

    def grouped(u):
        return u.astype(f32).reshape(u.shape[0], u.shape[1], S5_GROUPS, S5_GROUP_CH)

    ul, uc = grouped(u_lat), grouped(u_ctx)
    dsk = d_skip.astype(f32).reshape(S5_GROUPS, S5_GROUP_CH)
    y_lat = dsk * ul
    y_ctx = dsk * uc if need_ctx else None
    for d in range(N_DIRS):
        lam = lax.complex(jnp.minimum(lam_re[d].astype(f32), S5_MAX_RE), lam_im[d].astype(f32))
        lam_bar = jnp.exp(lam * jnp.exp(log_step[d].astype(f32))[:, None])
        b_bar = ((lam_bar - 1.0) / lam)[..., None] * lax.complex(b_re[d].astype(f32), b_im[d].astype(f32))
        c_mat = lax.complex(c_re[d].astype(f32), c_im[d].astype(f32))

        def rev(u):
            return jnp.flip(u, axis=1) if d == 1 else u

        def run(u, h0, want_y):
            bu = jnp.einsum('gnk,btgk->btgn', b_bar, rev(u).astype(jnp.complex64))
            bu = bu.at[:, 0].add(lam_bar * h0)
            _, h = lax.associative_scan(_linear_recurrence, (jnp.broadcast_to(lam_bar, bu.shape), bu), axis=1)
            y = rev(jnp.real(jnp.einsum('gkn,btgn->btgk', c_mat, h))) if want_y else None
            return y, h[:, -1]

        h0 = jnp.zeros((uc.shape[0], S5_GROUPS, S5_STATE), jnp.complex64)
        yc, hc = run(uc, h0, need_ctx)
        yl, _ = run(ul, hc, True)
        y_lat = y_lat + yl
        if need_ctx:
            y_ctx = y_ctx + yc

    def glu(y, like):
        y = jax.nn.gelu(y).reshape(like.shape).astype(like.dtype)
        a, g = jnp.split(y @ w_glu + b_glu, 2, axis=-1)
        return a * jax.nn.sigmoid(g)

    return glu(y_lat, u_lat), (glu(y_ctx, u_ctx) if need_ctx else None)


def conv_ffn(h, w_gate, w_up, conv_w, conv_b, w_down):
    gate = dwconv(h @ w_gate, conv_w, conv_b)
    return (jax.nn.silu(gate) * (h @ w_up)) @ w_down


def setup_inputs(seed: int = 0) -> dict:
    key = jax.random.key(seed)
    ks = iter(jax.random.split(key, 40))
    f32 = jnp.float32
    L = DEPTH

    def nrm(shape, scale):
        return scale * jax.random.normal(next(ks), shape, f32)

    def gain(shape):
        return 1.0 + nrm(shape, 0.02)

    def unif(shape, lo, hi):
        return jax.random.uniform(next(ks), shape, f32, lo, hi)

    inp = {}
    inp['x'] = nrm((BATCH, SEQ, D_MODEL), 1.0)
    inp['c'] = nrm((BATCH, D_MODEL), 1.0)
    inp['ctx'] = nrm((BATCH, CTX_LEN, D_MODEL), 1.0)
    inp['c_ctx'] = nrm((D_MODEL,), 1.0)
    inp['w_mod'] = nrm((L, D_MODEL, 6 * D_MODEL), 0.5 * D_MODEL ** -0.5)
    inp['b_mod'] = nrm((L, 6 * D_MODEL), 0.02)
    inp['g_mix'] = gain((L, D_MODEL))
    inp['w_in'] = nrm((L, D_MODEL, D_IN), D_MODEL ** -0.5)
    inp['ssd_conv_w'] = nrm((L, SSD_CONV, SSD_XBC), SSD_CONV ** -0.5)
    inp['ssd_conv_b'] = nrm((L, SSD_XBC), 0.02)
    inp['ssd_a_log'] = jnp.log(unif((L, N_DIRS, SSD_HEADS), 1.0, 16.0))
    dt0 = jnp.exp(unif((L, N_DIRS, SSD_HEADS), math.log(1e-3), math.log(1e-1)))
    inp['ssd_dt_bias'] = dt0 + jnp.log(-jnp.expm1(-dt0))
    inp['ssd_d'] = gain((L, SSD_HEADS))
    inp['ssd_norm_g'] = gain((L, D_SSD))
    inp['ga_q_norm'] = gain((L, HEAD_DIM))
    inp['ga_k_norm'] = gain((L, HEAD_DIM))
    inp['wa_sink'] = nrm((L, WA_HEADS), 0.5)
    inp['s5_lambda_re'] = -0.5 + nrm((L, N_DIRS, S5_GROUPS, S5_STATE), 0.01)
    inp['s5_lambda_im'] = jnp.pi * jnp.arange(S5_STATE, dtype=f32) + nrm((L, N_DIRS, S5_GROUPS, S5_STATE), 0.01)
    inp['s5_log_step'] = unif((L, N_DIRS, S5_GROUPS), math.log(1e-3), math.log(1e-1))
    inp['s5_b_re'] = nrm((L, N_DIRS, S5_GROUPS, S5_STATE, S5_GROUP_CH), (2 * S5_GROUP_CH) ** -0.5)
    inp['s5_b_im'] = nrm((L, N_DIRS, S5_GROUPS, S5_STATE, S5_GROUP_CH), (2 * S5_GROUP_CH) ** -0.5)
    inp['s5_c_re'] = nrm((L, N_DIRS, S5_GROUPS, S5_GROUP_CH, S5_STATE), S5_STATE ** -0.5)
    inp['s5_c_im'] = nrm((L, N_DIRS, S5_GROUPS, S5_GROUP_CH, S5_STATE), S5_STATE ** -0.5)
    inp['s5_d'] = nrm((L, D_S5), 1.0)
    inp['s5_w_glu'] = nrm((L, D_S5, 2 * D_S5), D_S5 ** -0.5)
    inp['s5_b_glu'] = nrm((L, 2 * D_S5), 0.02)
    inp['w_out'] = nrm((L, D_MIX, D_MODEL), D_MIX ** -0.5)
    inp['g_ffn'] = gain((L, D_MODEL))
    inp['w_gate'] = nrm((L, D_MODEL, D_FF), D_MODEL ** -0.5)
    inp['w_up'] = nrm((L, D_MODEL, D_FF), D_MODEL ** -0.5)
    inp['ffn_conv_w'] = nrm((L, FFN_CONV, D_FF), FFN_CONV ** -0.5)
    inp['ffn_conv_b'] = nrm((L, D_FF), 0.02)
    inp['w_down'] = nrm((L, D_FF, D_MODEL), D_FF ** -0.5)
    inp['g_final'] = gain((D_MODEL,))
    return inp


def reference(x, c, ctx, c_ctx, w_mod, b_mod, g_mix, w_in, ssd_conv_w, ssd_conv_b, ssd_a_log,
              ssd_dt_bias, ssd_d, ssd_norm_g, ga_q_norm, ga_k_norm, wa_sink, s5_lambda_re,
              s5_lambda_im, s5_log_step, s5_b_re, s5_b_im, s5_c_re, s5_c_im, s5_d, s5_w_glu,
              s5_b_glu, w_out, g_ffn, w_gate, w_up, ffn_conv_w, ffn_conv_b, w_down, g_final):
    seq = x.shape[1]
    cos, sin = axial_rope(seq)
    q_scale = HEAD_DIM ** -0.5
    xc = ctx
    for i in range(DEPTH):
        need_ctx = i < DEPTH - 1
        mod = jax.nn.silu(c) @ w_mod[i] + b_mod[i]
        modc = jax.nn.silu(c_ctx) @ w_mod[i] + b_mod[i]
        sh_a, sc_a, gt_a, sh_f, sc_f, gt_f = jnp.split(mod[:, None, :], 6, axis=-1)
        shc_a, scc_a, gtc_a, shc_f, scc_f, gtc_f = jnp.split(modc, 6)

        pl = jnp.split(modulate(x, g_mix[i], sh_a, sc_a) @ w_in[i], IN_OFFSETS, axis=-1)
        pc = jnp.split(modulate(xc, g_mix[i], shc_a, scc_a) @ w_in[i], IN_OFFSETS, axis=-1)

        y_ssd, yc_ssd = ssd_mixer(pl[0:3], pc[0:3], ssd_conv_w[i], ssd_conv_b[i], ssd_a_log[i],
                                  ssd_dt_bias[i], ssd_d[i], ssd_norm_g[i], need_ctx)

        kc_ga = rms_norm(heads(pc[4], GA_KV_HEADS), ga_k_norm[i])
        vc_ga = heads(pc[5], GA_KV_HEADS)
        q = apply_rope(rms_norm(heads(pl[3], GA_HEADS), ga_q_norm[i]), cos, sin) * q_scale
        k = apply_rope(rms_norm(heads(pl[4], GA_KV_HEADS), ga_k_norm[i]), cos, sin)
        y_ga = global_attention(q, k, heads(pl[5], GA_KV_HEADS), kc_ga, vc_ga)

        kc_wa = heads(pc[7], WA_KV_HEADS)
        vc_wa = heads(pc[8], WA_KV_HEADS)
        q = apply_rope(heads(pl[6], WA_HEADS), cos, sin) * q_scale
        k = apply_rope(heads(pl[7], WA_KV_HEADS), cos, sin)
        y_wa = window_attention(q, k, heads(pl[8], WA_KV_HEADS), kc_wa, vc_wa, wa_sink[i])

        y_s5, yc_s5 = s5_mixer(pl[9], pc[9], s5_lambda_re[i], s5_lambda_im[i], s5_log_step[i],
                               s5_b_re[i], s5_b_im[i], s5_c_re[i], s5_c_im[i], s5_d[i],
                               s5_w_glu[i], s5_b_glu[i], need_ctx)

        y = jnp.concatenate([y_ssd, y_ga, y_wa, y_s5], axis=-1) @ w_out[i]
        x = x + gt_a * y
        x = x + gt_f * conv_ffn(modulate(x, g_ffn[i], sh_f, sc_f), w_gate[i], w_up[i],
                                ffn_conv_w[i], ffn_conv_b[i], w_down[i])

        if need_ctx:
            yc_ga = context_attention(rms_norm(heads(pc[3], GA_HEADS), ga_q_norm[i]) * q_scale, kc_ga, vc_ga)
            yc_wa = context_attention(heads(pc[6], WA_HEADS) * q_scale, kc_wa, vc_wa, wa_sink[i])
            yc = jnp.concatenate([yc_ssd, yc_ga, yc_wa, yc_s5], axis=-1) @ w_out[i]
            xc = xc + gtc_a * yc
            xc = xc + gtc_f * conv_ffn(modulate(xc, g_ffn[i], shc_f, scc_f), w_gate[i], w_up[i],
                                       ffn_conv_w[i], ffn_conv_b[i], w_down[i])
    return rms_norm(x, g_final)
```

```python
import functools

import jax
import jax.numpy as jnp
from jax import lax
from jax.experimental import pallas as pl
from jax.experimental.pallas import tpu as pltpu

F32 = jnp.float32
BF16 = jnp.bfloat16
HIGHEST = lax.Precision.HIGHEST

HEAD_DIM = 64
GRID_W = 64
ROPE_THETA = 10000.0
NORM_EPS = 1e-6
WINDOW = 128
N_DIRS = 2
SSD_HEADS = 4
SSD_GROUPS = 2
SSD_STATE = 128
SSD_CHUNK = 128
D_SSD = SSD_HEADS * HEAD_DIM
SSD_XBC = D_SSD + 2 * SSD_GROUPS * SSD_STATE
S5_GROUPS = 16
S5_GROUP_CH = 16
S5_STATE = 64
S5_MAX_RE = -1e-4
S5_CHUNK = 32
D_S5 = S5_GROUPS * S5_GROUP_CH
LANES = 128
HALO = 8
NEG_BIG = -1e30
VMEM_LIMIT = 52 * 1024 * 1024

IN_SIZES = (D_SSD, SSD_XBC, N_DIRS * SSD_HEADS, 256, 128, 128, 256, 128, 128, D_S5)
P_Z, P_XBC, P_DT, P_GQ, P_GK, P_GV, P_WQ, P_WK, P_WV, P_U, P_END = (
    0, 256, 1024, 1152, 1408, 1536, 1664, 1920, 2048, 2176, 2432)


def _params(sem=None):
    return pltpu.CompilerParams(dimension_semantics=sem, vmem_limit_bytes=VMEM_LIMIT)


def _silu(v):
    return v * jax.nn.sigmoid(v)


def _softplus(v):
    return jnp.maximum(v, 0.0) + jnp.log1p(jnp.exp(-jnp.abs(v)))


def _dot(a, b):
    return jnp.dot(a, b, preferred_element_type=F32)


def _dot_nt(a, b, precision=None):
    return lax.dot_general(a, b, (((1,), (1,)), ((), ())), preferred_element_type=F32,
                           precision=precision)


def _dot_tn(a, b):
    return lax.dot_general(a, b, (((0,), (0,)), ((), ())), preferred_element_type=F32)


def _mod_kernel(cc_ref, w_ref, b_ref, o_ref):
    s = _silu(cc_ref[...])
    o_ref[0] = jnp.dot(s, w_ref[0], preferred_element_type=F32, precision=HIGHEST) + b_ref[0]


def _mod_call(cc, w_mod, b_mod):
    n_layers, d, n = w_mod.shape
    tn = 1536
    return pl.pallas_call(
        _mod_kernel,
        grid=(n_layers, n // tn),
        in_specs=[pl.BlockSpec((8, d), lambda l, j: (0, 0)),
                  pl.BlockSpec((1, d, tn), lambda l, j: (l, 0, j)),
                  pl.BlockSpec((1, 1, tn), lambda l, j: (l, 0, j))],
        out_specs=pl.BlockSpec((1, 8, tn), lambda l, j: (l, 0, j)),
        out_shape=jax.ShapeDtypeStruct((n_layers, 8, n), F32),
        compiler_params=_params(("arbitrary", "arbitrary")),
        name="adaln_mod",
    )(cc, w_mod, b_mod.reshape(n_layers, 1, n))


def _rms_mod(x, g, shift, scale):
    y = x * lax.rsqrt(jnp.mean(x * x, axis=-1, keepdims=True) + NORM_EPS) * g
    return y * (1.0 + scale) + shift


def _rope(t, cos, sna, snb):
    return t * cos + pltpu.roll(t, 96, 1) * sna + pltpu.roll(t, 32, 1) * snb


def _head_rms(t, gain, bd):
    t2 = t * t
    hi = t2.astype(BF16)
    lo = (t2 - hi.astype(F32)).astype(BF16)
    ms = _dot(hi, bd) + _dot(lo, bd)
    return t * lax.rsqrt(ms + NORM_EPS) * gain


def _inproj_kernel(x_ref, sh_ref, sc_ref, g_ref, w_ref, cos_ref, sna_ref, snb_ref, qn_ref, kn_ref,
                   bd_ref, z_ref, xbc_ref, dt_ref, gq_ref, gk_ref, gv_ref, wq_ref, wk_ref, wv_ref,
                   u_ref):
    h = _rms_mod(x_ref[...], g_ref[...], sh_ref[0], sc_ref[0])
    p = _dot(h.astype(BF16), w_ref[...])
    cos, sna, snb = cos_ref[...], sna_ref[...], snb_ref[...]
    bd = bd_ref[...]
    q_scale = HEAD_DIM ** -0.5
    z_ref[...] = p[:, P_Z:P_XBC]
    xbc_ref[...] = p[:, P_XBC:P_DT]
    dt_ref[...] = p[:, P_DT:P_GQ]
    for j in range(2):
        lo = P_GQ + j * LANES
        q = _rope(_head_rms(p[:, lo:lo + LANES], qn_ref[...], bd), cos, sna, snb) * q_scale
        gq_ref[:, j * LANES:(j + 1) * LANES] = q.astype(BF16)
        lo = P_WQ + j * LANES
        q = _rope(p[:, lo:lo + LANES], cos, sna, snb) * q_scale
        wq_ref[:, j * LANES:(j + 1) * LANES] = q.astype(BF16)
    gk_ref[...] = _rope(_head_rms(p[:, P_GK:P_GV], kn_ref[...], bd), cos, sna, snb).astype(BF16)
    gv_ref[...] = p[:, P_GV:P_WQ].astype(BF16)
    wk_ref[...] = _rope(p[:, P_WK:P_WV], cos, sna, snb).astype(BF16)
    wv_ref[...] = p[:, P_WV:P_U].astype(BF16)
    u_ref[...] = p[:, P_U:P_END]


def _inproj_call(x2, shift, scale, g, w_pad, tabs, qn, kn, bd, *, tm, rows_per_seg, seq):
    r, d = x2.shape
    tps = rows_per_seg // tm
    tpq = seq // tm
    row = lambda w: pl.BlockSpec((tm, w), lambda i: (i, 0))
    seg = pl.BlockSpec((1, 1, d), lambda i: (i // tps, 0, 0))
    full = lambda a: pl.BlockSpec(a.shape, lambda i: (0,) * a.ndim)
    tab = pl.BlockSpec((tm, LANES), lambda i: (i % tpq, 0))
    widths = (256, SSD_XBC, LANES, 256, 128, 128, 256, 128, 128, D_S5)
    dtypes = (F32, F32, F32, BF16, BF16, BF16, BF16, BF16, BF16, F32)
    return pl.pallas_call(
        _inproj_kernel,
        grid=(r // tm,),
        in_specs=[row(d), seg, seg, full(g), full(w_pad), tab, tab, tab, full(qn), full(kn), full(bd)],
        out_specs=[row(w) for w in widths],
        out_shape=[jax.ShapeDtypeStruct((r, w), t) for w, t in zip(widths, dtypes)],
        compiler_params=_params(("arbitrary",)),
        name="inproj",
    )(x2, shift, scale, g, w_pad, *tabs, qn, kn, bd)


def _ssd_dir(d, ce, nc, x_ref, xp_ref, xn_ref, dtc_ref, dtr_ref, cw_ref, cb_ref, al_ref, bi_ref,
             alc_ref, bic_ref, dsk_ref, h_sc, y_ref):
    q = SSD_CHUNK
    x = x_ref[0]
    row = lax.broadcasted_iota(jnp.int32, (q, 1), 0)
    prev = jnp.where(ce > 0, xp_ref[0][HALO - 1:HALO, :], 0.0)
    nxt = jnp.where(ce < nc - 1, xn_ref[0][0:1, :], 0.0)
    up = jnp.where(row == 0, prev, pltpu.roll(x, 1, 0))
    dn = jnp.where(row == q - 1, nxt, pltpu.roll(x, q - 1, 0))
    cw = cw_ref[...]
    act = _silu(cw[0:1] * up + cw[1:2] * x + cw[2:3] * dn + cb_ref[...])
    xs = act[:, 0:D_SSD]
    dt_c = _softplus(dtc_ref[0] + bi_ref[...])
    dta_c = dt_c * (-jnp.exp(al_ref[...]))
    dt_r = _softplus(dtr_ref[0] + bic_ref[...])
    dta_r = dt_r * (-jnp.exp(alc_ref[...]))
    ri = lax.broadcasted_iota(jnp.int32, (q, q), 0)
    ci = lax.broadcasted_iota(jnp.int32, (q, q), 1)
    mask = (ri >= ci) if d == 0 else (ri <= ci)
    maskf = mask.astype(F32)
    acs_c = jnp.dot(maskf, dta_c, preferred_element_type=F32, precision=HIGHEST)
    acs_r = _dot_nt(dta_r, maskf, precision=HIGHEST)
    tot = jnp.sum(dta_c, axis=0, keepdims=True)
    lane = lax.broadcasted_iota(jnp.int32, (1, LANES), 1)
    first = lane < HEAD_DIM
    hpg = SSD_HEADS // SSD_GROUPS
    for g in range(SSD_GROUPS):
        e0 = d * SSD_HEADS + g * hpg
        bm = act[:, D_SSD + g * SSD_STATE:D_SSD + (g + 1) * SSD_STATE].astype(BF16)
        cm = act[:, D_SSD + (SSD_GROUPS + g) * SSD_STATE:D_SSD + (SSD_GROUPS + g + 1) * SSD_STATE].astype(BF16)
        pick = lambda v: jnp.where(first, v[:, e0:e0 + 1], v[:, e0 + 1:e0 + 2])
        dt_g = pick(dt_c)
        acs_g = pick(acs_c)
        tot_g = pick(tot)
        xs_g = xs[:, g * LANES:(g + 1) * LANES]
        xdt = xs_g * dt_g
        xdt_b = xdt.astype(BF16)
        cb = _dot_nt(cm, bm)
        yd = []
        for a in range(hpg):
            e = e0 + a
            dec = jnp.where(mask, jnp.exp(acs_c[:, e:e + 1] - acs_r[e:e + 1, :]), 0.0)
            yd.append(_dot((cb * dec).astype(BF16), xdt_b))
        h_old = h_sc[d, g]
        y = jnp.where(first, yd[0], yd[1]) + _dot(cm, h_old.astype(BF16)) * jnp.exp(acs_g)
        if d == 0:
            y = y + dsk_ref[:, g * LANES:(g + 1) * LANES] * xs_g
        y_ref[0, :, g * LANES:(g + 1) * LANES] = y
        h_sc[d, g] = h_old * jnp.exp(tot_g) + _dot_tn(bm, (xdt * jnp.exp(tot_g - acs_g)).astype(BF16))


def _ssd_kernel(xf_ref, xfp_ref, xfn_ref, xb_ref, xbp_ref, xbn_ref, dtcf_ref, dtcb_ref, dtrf_ref,
                dtrb_ref, h0_ref, cw_ref, cb_ref, al_ref, bi_ref, alc_ref, bic_ref, dsk_ref,
                yf_ref, yb_ref, hout_ref, h_sc, *, nc):
    c = pl.program_id(1)

    @pl.when(c == 0)
    def _():
        h_sc[...] = h0_ref[0]

    shared = (cw_ref, cb_ref, al_ref, bi_ref, alc_ref, bic_ref, dsk_ref, h_sc)
    _ssd_dir(0, c, nc, xf_ref, xfp_ref, xfn_ref, dtcf_ref, dtrf_ref, *shared, yf_ref)
    _ssd_dir(1, nc - 1 - c, nc, xb_ref, xbp_ref, xbn_ref, dtcb_ref, dtrb_ref, *shared, yb_ref)

    @pl.when(c == nc - 1)
    def _():
        hout_ref[0] = h_sc[...]


def _ssd_call(xbc, dt_pad, dt_t, h0, conv_w, conv_b, alog_r, bias_r, alog_c, bias_c, dsk):
    b, s, w = xbc.shape
    q = SSD_CHUNK
    nc = s // q
    hb = q // HALO
    last = s // HALO - 1
    fwd = lambda c: c
    bwd = lambda c: nc - 1 - c
    xspec = lambda f: pl.BlockSpec((1, q, w), lambda i, c: (i, f(c), 0))
    pspec = lambda f: pl.BlockSpec((1, HALO, w), lambda i, c: (i, jnp.maximum(f(c) * hb - 1, 0), 0))
    nspec = lambda f: pl.BlockSpec((1, HALO, w), lambda i, c: (i, jnp.minimum(f(c) * hb + hb, last), 0))
    dcspec = lambda f: pl.BlockSpec((1, q, LANES), lambda i, c: (i, f(c), 0))
    drspec = lambda f: pl.BlockSpec((1, 8, q), lambda i, c: (i, 0, f(c)))
    full = lambda a: pl.BlockSpec(a.shape, lambda i, c: (0,) * a.ndim)
    hspec = pl.BlockSpec((1,) + h0.shape[1:], lambda i, c: (i, 0, 0, 0, 0))
    yspec = lambda f: pl.BlockSpec((1, q, D_SSD), lambda i, c: (i, f(c), 0))
    return pl.pallas_call(
        functools.partial(_ssd_kernel, nc=nc),
        grid=(b, nc),
        in_specs=[xspec(fwd), pspec(fwd), nspec(fwd), xspec(bwd), pspec(bwd), nspec(bwd),
                  dcspec(fwd), dcspec(bwd), drspec(fwd), drspec(bwd), hspec,
                  full(conv_w), full(conv_b), full(alog_r), full(bias_r), full(alog_c), full(bias_c),
                  full(dsk)],
        out_specs=[yspec(fwd), yspec(bwd), hspec],
        out_shape=[jax.ShapeDtypeStruct((b, s, D_SSD), F32), jax.ShapeDtypeStruct((b, s, D_SSD), F32),
                   jax.ShapeDtypeStruct(h0.shape, F32)],
        scratch_shapes=[pltpu.VMEM(h0.shape[1:], F32)],
        compiler_params=_params(("arbitrary", "arbitrary")),
        name="ssd",
    )(xbc, xbc, xbc, xbc, xbc, xbc, dt_pad, dt_pad, dt_t, dt_t, h0, conv_w, conv_b, alog_r, bias_r,
      alog_c, bias_c, dsk)


def _flash_kernel(sink_ref, qt_ref, k_ref, vt_ref, o_ref, m_sc, l_sc, acc_sc, *, nk, has_sink):
    g = pl.program_id(1)
    ki = pl.program_id(3)
    hpg = qt_ref.shape[2]

    @pl.when(ki == 0)
    def _():
        for a in range(hpg):
            if has_sink:
                m_sc[a] = jnp.full(m_sc.shape[1:], sink_ref[g * hpg + a], F32)
                l_sc[a] = jnp.ones(l_sc.shape[1:], F32)
            else:
                m_sc[a] = jnp.full(m_sc.shape[1:], NEG_BIG, F32)
                l_sc[a] = jnp.zeros(l_sc.shape[1:], F32)
        acc_sc[...] = jnp.zeros(acc_sc.shape, F32)

    k = k_ref[0, 0]
    vt = vt_ref[0, 0]
    for a in range(hpg):
        s = _dot(k, qt_ref[0, 0, a])
        m_prev = m_sc[a]
        m_new = jnp.maximum(m_prev, jnp.max(s, axis=0, keepdims=True))
        alpha = jnp.exp(m_prev - m_new)
        p = jnp.exp(s - m_new)
        l_sc[a] = alpha * l_sc[a] + jnp.sum(p, axis=0, keepdims=True)
        acc_sc[a] = alpha * acc_sc[a] + _dot(vt, p.astype(BF16))
        m_sc[a] = m_new

    @pl.when(ki == nk - 1)
    def _():
        for a in range(hpg):
            o_ref[0, 0, a] = (acc_sc[a] / l_sc[a]).astype(o_ref.dtype)


def _pick(n, cands):
    for c in cands:
        if n % c == 0:
            return c
    return n


def _flash_call(qt, k, vt, sink):
    b, g, hpg, dh, sq = qt.shape
    sk = k.shape[2]
    tq = _pick(sq, (512, 256))
    tk = _pick(sk, (768, 512, 256))
    nk = sk // tk
    has_sink = sink is not None
    if sink is None:
        sink = jnp.zeros((g * hpg,), F32)
    return pl.pallas_call(
        functools.partial(_flash_kernel, nk=nk, has_sink=has_sink),
        grid=(b, g, sq // tq, nk),
        in_specs=[pl.BlockSpec(memory_space=pltpu.SMEM),
                  pl.BlockSpec((1, 1, hpg, dh, tq), lambda i, j, qi, ki: (i, j, 0, 0, qi)),
                  pl.BlockSpec((1, 1, tk, dh), lambda i, j, qi, ki: (i, j, ki, 0)),
                  pl.BlockSpec((1, 1, dh, tk), lambda i, j, qi, ki: (i, j, 0, ki))],
        out_specs=pl.BlockSpec((1, 1, hpg, dh, tq), lambda i, j, qi, ki: (i, j, 0, 0, qi)),
        out_shape=jax.ShapeDtypeStruct(qt.shape, BF16),
        scratch_shapes=[pltpu.VMEM((hpg, 1, tq), F32), pltpu.VMEM((hpg, 1, tq), F32),
                        pltpu.VMEM((hpg, dh, tq), F32)],
        compiler_params=_params(("arbitrary", "arbitrary", "arbitrary", "arbitrary")),
        name="flash_sink" if has_sink else "flash",
    )(sink, qt, k, vt)


def _win_kernel(sink_ref, qt_ref, kp_ref, kc_ref, kn_ref, kx_ref, vtp_ref, vtc_ref, vtn_ref, vtx_ref,
                o_ref, *, nq):
    g = pl.program_id(1)
    j = pl.program_id(2)
    hpg = qt_ref.shape[2]
    tq = qt_ref.shape[4]
    blk = kp_ref.shape[2]
    rc = lax.broadcasted_iota(jnp.int32, (tq, tq), 0) - lax.broadcasted_iota(jnp.int32, (tq, tq), 1)
    m_cur = (rc <= WINDOW) & (rc >= -WINDOW)
    rp = lax.broadcasted_iota(jnp.int32, (blk, tq), 0) - lax.broadcasted_iota(jnp.int32, (blk, tq), 1)
    m_prev = (rp - blk >= -WINDOW) & (j > 0)
    m_next = (rp + tq <= WINDOW) & (j < nq - 1)
    for a in range(hpg):
        qt = qt_ref[0, 0, a]
        sink = sink_ref[g * hpg + a]
        s_c = jnp.where(m_cur, _dot(kc_ref[0, 0], qt), NEG_BIG)
        s_p = jnp.where(m_prev, _dot(kp_ref[0, 0], qt), NEG_BIG)
        s_n = jnp.where(m_next, _dot(kn_ref[0, 0], qt), NEG_BIG)
        s_x = _dot(kx_ref[0, 0], qt)
        cmax = lambda v: jnp.max(v, axis=0, keepdims=True)
        m = jnp.maximum(jnp.maximum(cmax(s_c), cmax(s_p)), jnp.maximum(cmax(s_n), cmax(s_x)))
        m = jnp.maximum(m, sink)
        p_c, p_p, p_n, p_x = (jnp.exp(v - m) for v in (s_c, s_p, s_n, s_x))
        csum = lambda v: jnp.sum(v, axis=0, keepdims=True)
        l = csum(p_c) + csum(p_p) + csum(p_n) + csum(p_x) + jnp.exp(sink - m)
        o = (_dot(vtc_ref[0, 0], p_c.astype(BF16)) + _dot(vtp_ref[0, 0], p_p.astype(BF16))
             + _dot(vtn_ref[0, 0], p_n.astype(BF16)) + _dot(vtx_ref[0, 0], p_x.astype(BF16)))
        o_ref[0, 0, a] = (o / l).astype(o_ref.dtype)


def _win_call(qt, k, vt, kx, vtx, sink):
    b, g, hpg, dh, s = qt.shape
    lc = kx.shape[2]
    blk = WINDOW
    tq = _pick(s, (512, 256, 128))
    nq = s // tq
    r = tq // blk
    nblk = s // blk
    im = lambda f: (lambda i, j, t: f(i, j, t))
    return pl.pallas_call(
        functools.partial(_win_kernel, nq=nq),
        grid=(b, g, nq),
        in_specs=[pl.BlockSpec(memory_space=pltpu.SMEM),
                  pl.BlockSpec((1, 1, hpg, dh, tq), lambda i, j, t: (i, j, 0, 0, t)),
                  pl.BlockSpec((1, 1, blk, dh), lambda i, j, t: (i, j, jnp.maximum(t * r - 1, 0), 0)),
                  pl.BlockSpec((1, 1, tq, dh), lambda i, j, t: (i, j, t, 0)),
                  pl.BlockSpec((1, 1, blk, dh), lambda i, j, t: (i, j, jnp.minimum(t * r + r, nblk - 1), 0)),
                  pl.BlockSpec((1, 1, lc, dh), lambda i, j, t: (i, j, 0, 0)),
                  pl.BlockSpec((1, 1, dh, blk), lambda i, j, t: (i, j, 0, jnp.maximum(t * r - 1, 0))),
                  pl.BlockSpec((1, 1, dh, tq), lambda i, j, t: (i, j, 0, t)),
                  pl.BlockSpec((1, 1, dh, blk), lambda i, j, t: (i, j, 0, jnp.minimum(t * r + r, nblk - 1))),
                  pl.BlockSpec((1, 1, dh, lc), lambda i, j, t: (i, j, 0, 0))],
        out_specs=pl.BlockSpec((1, 1, hpg, dh, tq), lambda i, j, t: (i, j, 0, 0, t)),
        out_shape=jax.ShapeDtypeStruct(qt.shape, BF16),
        compiler_params=_params(("arbitrary", "arbitrary", "arbitrary")),
        name="window_attn",
    )(sink, qt, k, k, k, kx, vt, vt, vt, vtx)


def _rep_rows(p, n):
    return jnp.concatenate([jnp.broadcast_to(p[s:s + 1, :], (n, p.shape[1])) for s in range(p.shape[0])], axis=0)


def _ctab(pr, pi, mr, mi):
    big_l, k = pr.shape[0], mr.shape[0]
    er, ei = _rep_rows(pr, k), _rep_rows(pi, k)
    tr, ti = jnp.tile(mr, (big_l, 1)), jnp.tile(mi, (big_l, 1))
    return er * tr - ei * ti, er * ti + ei * tr


def _cmul_rows(cr, ci, s):
    n = cr.shape[1]
    return jnp.concatenate([cr, cr], axis=1) * s + jnp.concatenate([-ci, ci], axis=1) * pltpu.roll(s, n, 1)


def _seg_scan(x, cr, ci, rowm, nper, reverse):
    rows = x.shape[0]
    s, sh = x, 1
    while sh < nper:
        if reverse:
            shifted, valid = pltpu.roll(s, rows - sh, 0), rowm < nper - sh
        else:
            shifted, valid = pltpu.roll(s, sh, 0), rowm >= sh
        s = s + jnp.where(valid, _cmul_rows(cr, ci, shifted), 0.0)
        cr, ci = cr * cr - ci * ci, 2.0 * cr * ci
        sh *= 2
    return s


def _s5_kernel(ul_ref, ux_ref, lre_ref, lim_ref, ls_ref, bre_ref, bim_ref, cre_ref, cim_ref,
               yl_ref, yx_ref, t_sc, *, nb, ncl, ncx):
    big_l = S5_CHUNK
    k = S5_GROUP_CH
    n = S5_STATE
    w = big_l * k
    tau = lax.broadcasted_iota(jnp.int32, (big_l, 1), 0).astype(F32)
    tabs = []
    for d in range(N_DIRS):
        lr = jnp.minimum(lre_ref[0, d], S5_MAX_RE)
        li = lim_ref[0, d]
        dl = jnp.exp(ls_ref[0, d])
        ar, th = lr * dl, li * dl

        def power(t, ar=ar, th=th):
            mag = jnp.exp(t * ar)
            return mag * jnp.cos(t * th), mag * jnp.sin(t * th)

        lbr, lbi = power(1.0)
        den = lr * lr + li * li
        zr = ((lbr - 1.0) * lr + lbi * li) / den
        zi = (lbi * lr - (lbr - 1.0) * li) / den
        br, bi = bre_ref[0, d], bim_ref[0, d]
        bbr, bbi = zr * br - zi * bi, zr * bi + zi * br
        cr, ci = cre_ref[0, d], cim_ref[0, d]
        tabs.append(dict(power=power, bbr=bbr, bbi=bbi, cr=cr, ci=ci, lam_l=power(float(big_l))))
    f, b = tabs
    e_f = _ctab(*f["power"](big_l - 1.0 - tau), f["bbr"], f["bbi"])
    e_b = _ctab(*b["power"](tau), b["bbr"], b["bbi"])
    g_f = _ctab(*f["power"](tau + 1.0), f["cr"], f["ci"])
    g_b = _ctab(*b["power"](big_l - tau), b["cr"], b["ci"])
    a_b = _ctab(*b["power"](big_l - 1.0 - tau), b["cr"], b["ci"])
    c_f = jnp.tile(f["cr"], (big_l, 1)), jnp.tile(f["ci"], (big_l, 1))
    bb2 = lambda t: jnp.concatenate([t["bbr"], t["bbi"]], axis=1)
    neg = lambda t: jnp.concatenate([t[0], -t[1]], axis=1)
    ka = _dot_nt(bb2(b), neg(a_b), precision=HIGHEST)
    kb = _dot_nt(bb2(f), neg(g_f), precision=HIGHEST)
    kc = _dot_nt(bb2(f), neg(c_f), precision=HIGHEST)
    lane = lax.broadcasted_iota(jnp.int32, (1, w), 1)
    ka = ka + jnp.where(lane >= w - k, kc, 0.0)
    kall = jnp.concatenate([ka, kb], axis=1)
    for s in range(big_l):
        off = (big_l - 1 - s) * k
        t_sc[s * k:(s + 1) * k, :] = kall[:, off:off + w].astype(BF16)
    ecat = jnp.concatenate([e_f[0], e_f[1], e_b[0], e_b[1]], axis=1).astype(BF16)
    gcat = jnp.concatenate([g_f[0], -g_f[1], g_b[0], -g_b[1]], axis=1).astype(BF16)
    tmat = t_sc[...]

    def rowmod(nper):
        return jnp.concatenate([lax.broadcasted_iota(jnp.int32, (nper, 1), 0)] * nb, axis=0)

    def by_batch(rows, nper):
        return jnp.concatenate([jnp.broadcast_to(r, (nper, r.shape[1])) for r in rows], axis=0)

    ux = ux_ref[0]
    hx = _dot(ux, ecat)
    rmx = rowmod(ncx)
    rx = nb * ncx
    sxf = _seg_scan(hx[:, :2 * n], *f["lam_l"], rmx, ncx, False)
    sxb = _seg_scan(hx[:, 2 * n:], *b["lam_l"], rmx, ncx, True)
    hin_xf = jnp.where(rmx >= 1, pltpu.roll(sxf, 1, 0), 0.0)
    hin_xb = jnp.where(rmx < ncx - 1, pltpu.roll(sxb, rx - 1, 0), 0.0)
    hin_x = jnp.concatenate([hin_xf, hin_xb], axis=1).astype(BF16)
    yx_ref[0] = _dot(ux, tmat) + _dot_nt(hin_x, gcat)
    hc_f = [sxf[i * ncx + ncx - 1:i * ncx + ncx, :] for i in range(nb)]
    hc_b = [sxb[i * ncx:i * ncx + 1, :] for i in range(nb)]
    ul = ul_ref[0]
    hl = _dot(ul, ecat)
    rml = rowmod(ncl)
    rl = nb * ncl
    xf = jnp.where(rml == 0, by_batch(hc_f, ncl), pltpu.roll(hl[:, :2 * n], 1, 0))
    xb = jnp.where(rml == ncl - 1, by_batch(hc_b, ncl), pltpu.roll(hl[:, 2 * n:], rl - 1, 0))
    hin_f = _seg_scan(xf, *f["lam_l"], rml, ncl, False)
    hin_b = _seg_scan(xb, *b["lam_l"], rml, ncl, True)
    hin = jnp.concatenate([hin_f, hin_b], axis=1).astype(BF16)
    yl_ref[0] = _dot(ul, tmat) + _dot_nt(hin, gcat)


def _s5_call(ul, ux, lre, lim, ls, bre, bim, cre, cim, *, nb):
    g, rl, w = ul.shape
    rx = ux.shape[1]
    grp = lambda a: pl.BlockSpec((1,) + a.shape[1:], lambda i: (i,) + (0,) * (a.ndim - 1))
    return pl.pallas_call(
        functools.partial(_s5_kernel, nb=nb, ncl=rl // nb, ncx=rx // nb),
        grid=(g,),
        in_specs=[grp(a) for a in (ul, ux, lre, lim, ls, bre, bim, cre, cim)],
        out_specs=[grp(ul), grp(ux)],
        out_shape=[jax.ShapeDtypeStruct(ul.shape, F32), jax.ShapeDtypeStruct(ux.shape, F32)],
        scratch_shapes=[pltpu.VMEM((w, w), BF16)],
        compiler_params=_params(("arbitrary",)),
        name="s5",
    )(ul, ux, lre, lim, ls, bre, bim, cre, cim)


def _outproj_kernel(x_ref, gt_ref, yf_ref, yb_ref, z_ref, ga_ref, wa_ref, s5_ref, u_ref, ng_ref,
                    d5_ref, wglu_ref, bglu_ref, wout_ref, o_ref):
    ys = (yf_ref[...] + yb_ref[...]) * _silu(z_ref[...])
    ys = ys * lax.rsqrt(jnp.mean(ys * ys, axis=-1, keepdims=True) + NORM_EPS) * ng_ref[...]
    t = jax.nn.gelu(s5_ref[...] + d5_ref[...] * u_ref[...])
    t = _dot(t.astype(BF16), wglu_ref[...]) + bglu_ref[...]
    s5o = t[:, :D_S5] * jax.nn.sigmoid(t[:, D_S5:])
    cat = jnp.concatenate([ys.astype(BF16), ga_ref[...], wa_ref[...], s5o.astype(BF16)], axis=1)
    o_ref[...] = x_ref[...] + gt_ref[0] * _dot(cat, wout_ref[...])


def _outproj_call(x2, gate, yf, yb, z, ga, wa, s5, u, ng, d5, wglu, bglu, wout, *, tm, rows_per_seg):
    r, d = x2.shape
    tps = rows_per_seg // tm
    row = lambda a: pl.BlockSpec((tm, a.shape[1]), lambda i: (i, 0))
    seg = pl.BlockSpec((1, 1, d), lambda i: (i // tps, 0, 0))
    full = lambda a: pl.BlockSpec(a.shape, lambda i: (0,) * a.ndim)
    return pl.pallas_call(
        _outproj_kernel,
        grid=(r // tm,),
        in_specs=[row(x2), seg] + [row(a) for a in (yf, yb, z, ga, wa, s5, u)]
                 + [full(a) for a in (ng, d5, wglu, bglu, wout)],
        out_specs=row(x2),
        out_shape=jax.ShapeDtypeStruct(x2.shape, F32),
        compiler_params=_params(("arbitrary",)),
        name="outproj",
    )(x2, gate, yf, yb, z, ga, wa, s5, u, ng, d5, wglu, bglu, wout)


def _ffn_kernel(x_ref, xp_ref, xn_ref, sh_ref, sc_ref, gt_ref, g_ref, wgu_ref, cw_ref, cb_ref,
                wd_ref, gfin_ref, o_ref, *, tiles_per_seq, d_ff, final_norm):
    i = pl.program_id(0)
    tm = x_ref.shape[0]
    x = x_ref[...]
    g, sh, sc = g_ref[...], sh_ref[0], sc_ref[0]
    gu = _dot(_rms_mod(x, g, sh, sc).astype(BF16), wgu_ref[...])
    gate = gu[:, :d_ff]
    halo = jnp.concatenate([xp_ref[...], xn_ref[...]], axis=0)
    gh = _dot(_rms_mod(halo, g, sh, sc).astype(BF16), wgu_ref[:, :d_ff])
    t = i % tiles_per_seq
    prev = jnp.where(t > 0, gh[HALO - 1:HALO, :], 0.0)
    nxt = jnp.where(t < tiles_per_seq - 1, gh[HALO:HALO + 1, :], 0.0)
    row = lax.broadcasted_iota(jnp.int32, (tm, 1), 0)
    up = jnp.where(row == 0, prev, pltpu.roll(gate, 1, 0))
    dn = jnp.where(row == tm - 1, nxt, pltpu.roll(gate, tm - 1, 0))
    cw = cw_ref[...]
    conv = cw[0:1] * up + cw[1:2] * gate + cw[2:3] * dn + cb_ref[...]
    act = (_silu(conv) * gu[:, d_ff:]).astype(BF16)
    y = x + gt_ref[0] * _dot(act, wd_ref[...])
    if final_norm:
        y = y * lax.rsqrt(jnp.mean(y * y, axis=-1, keepdims=True) + NORM_EPS) * gfin_ref[...]
    o_ref[...] = y


def _ffn_call(x2, shift, scale, gate, g, wgu, cw, cb, wd, gfin, *, tm, rows_per_seg, seq, final_norm):
    r, d = x2.shape
    d_ff = wd.shape[0]
    tps = rows_per_seg // tm
    hb = tm // HALO
    last = r // HALO - 1
    seg = pl.BlockSpec((1, 1, d), lambda i: (i // tps, 0, 0))
    full = lambda a: pl.BlockSpec(a.shape, lambda i: (0,) * a.ndim)
    once = lambda a: pl.BlockSpec(a.shape, lambda i: (0,) * a.ndim, pipeline_mode=pl.Buffered(1))
    return pl.pallas_call(
        functools.partial(_ffn_kernel, tiles_per_seq=seq // tm, d_ff=d_ff, final_norm=final_norm),
        grid=(r // tm,),
        in_specs=[pl.BlockSpec((tm, d), lambda i: (i, 0)),
                  pl.BlockSpec((HALO, d), lambda i: (jnp.maximum(i * hb - 1, 0), 0)),
                  pl.BlockSpec((HALO, d), lambda i: (jnp.minimum(i * hb + hb, last), 0)),
                  seg, seg, seg, full(g), once(wgu), full(cw), full(cb), once(wd), full(gfin)],
        out_specs=pl.BlockSpec((tm, d), lambda i: (i, 0)),
        out_shape=jax.ShapeDtypeStruct(x2.shape, F32),
        compiler_params=_params(("arbitrary",)),
        name="conv_ffn",
    )(x2, x2, x2, shift, scale, gate, g, wgu, cw, cb, wd, gfin)


def _rope_tables(seq):
    rows = seq // GRID_W
    r = jnp.broadcast_to(jnp.arange(rows, dtype=F32)[:, None], (rows, GRID_W)).reshape(-1)
    c = jnp.broadcast_to(jnp.arange(GRID_W, dtype=F32)[None, :], (rows, GRID_W)).reshape(-1)
    n_freq = HEAD_DIM // 4
    inv = ROPE_THETA ** (-jnp.arange(n_freq, dtype=F32) / n_freq)
    ang = jnp.concatenate([r[:, None] * inv, c[:, None] * inv], axis=-1)
    cos, sin, zero = jnp.cos(ang), jnp.sin(ang), jnp.zeros_like(ang)
    reps = LANES // HEAD_DIM
    return (jnp.tile(jnp.concatenate([cos, cos], axis=-1), (1, reps)),
            jnp.tile(jnp.concatenate([-sin, zero], axis=-1), (1, reps)),
            jnp.tile(jnp.concatenate([zero, sin], axis=-1), (1, reps)))


def _pad_w_in(w):
    offs = [0]
    for s in IN_SIZES:
        offs.append(offs[-1] + s)
    parts = [w[:, offs[j]:offs[j + 1]] for j in range(len(IN_SIZES))]
    parts[2] = jnp.pad(parts[2], ((0, 0), (0, LANES - IN_SIZES[2])))
    return jnp.concatenate(parts, axis=1).astype(BF16)


def _heads_t(q2, b, hpg):
    s = q2.shape[0] // b
    g = q2.shape[1] // (hpg * HEAD_DIM)
    return q2.reshape(b, s, g, hpg, HEAD_DIM).transpose(0, 2, 3, 4, 1)


def _heads_back(ot):
    b, g, hpg, dh, s = ot.shape
    return ot.transpose(0, 4, 1, 2, 3).reshape(b * s, g * hpg * dh)


def _kv_rows(k2, b):
    s = k2.shape[0] // b
    return k2.reshape(b, s, -1, HEAD_DIM).transpose(0, 2, 1, 3)


def _kv_cols(v2, b):
    s = v2.shape[0] // b
    return v2.reshape(b, s, -1, HEAD_DIM).transpose(0, 2, 3, 1)


def _s5_rows(u2, b):
    s = u2.shape[0] // b
    nc = s // S5_CHUNK
    u = u2.reshape(b, nc, S5_CHUNK, S5_GROUPS, S5_GROUP_CH).transpose(3, 0, 1, 2, 4)
    return u.reshape(S5_GROUPS, b * nc, S5_CHUNK * S5_GROUP_CH).astype(BF16)


def _s5_back(y, b):
    g, r, _ = y.shape
    nc = r // b
    y = y.reshape(g, b, nc, S5_CHUNK, S5_GROUP_CH).transpose(1, 2, 3, 0, 4)
    return y.reshape(b * nc * S5_CHUNK, g * S5_GROUP_CH)


def _lane_pad(v, width=LANES):
    return jnp.pad(v.reshape(1, -1), ((0, 0), (0, width - v.size)))


def kernel(x, c, ctx, c_ctx, w_mod, b_mod, g_mix, w_in, ssd_conv_w, ssd_conv_b, ssd_a_log, ssd_dt_bias, ssd_d, ssd_norm_g, ga_q_norm, ga_k_norm, wa_sink, s5_lambda_re, s5_lambda_im, s5_log_step, s5_b_re, s5_b_im, s5_c_re, s5_c_im, s5_d, s5_w_glu, s5_b_glu, w_out, g_ffn, w_gate, w_up, ffn_conv_w, ffn_conv_b, w_down, g_final):
    b, s, d = x.shape
    lc = ctx.shape[1]
    depth = w_mod.shape[0]
    hpg = 2
    tm = 512

    cc = jnp.zeros((8, d), F32).at[:b].set(c).at[b].set(c_ctx)
    mod = _mod_call(cc, w_mod, b_mod).reshape(depth, 8, 6, d)

    tabs_l = _rope_tables(s)
    one, zero = jnp.ones((lc, LANES), F32), jnp.zeros((lc, LANES), F32)
    tabs_c = (one, zero, zero)
    bd = jnp.kron(jnp.eye(LANES // HEAD_DIM, dtype=F32),
                  jnp.full((HEAD_DIM, HEAD_DIM), 1.0 / HEAD_DIM, F32)).astype(BF16)
    gfin = g_final.reshape(1, d)

    x2 = x.reshape(b * s, d)
    xc2 = ctx.reshape(b * lc, d)
    for i in range(depth):
        need_ctx = i < depth - 1
        ml = lambda j: mod[i, :b, j].reshape(b, 1, d)
        mc = lambda j: mod[i, b, j].reshape(1, 1, d)
        g_i = g_mix[i].reshape(1, d)
        w_pad = _pad_w_in(w_in[i])
        qn = jnp.tile(ga_q_norm[i], LANES // HEAD_DIM).reshape(1, LANES)
        kn = jnp.tile(ga_k_norm[i], LANES // HEAD_DIM).reshape(1, LANES)
        pl_ = _inproj_call(x2, ml(0), ml(1), g_i, w_pad, tabs_l, qn, kn, bd,
                           tm=tm, rows_per_seg=s, seq=s)
        pc_ = _inproj_call(xc2, mc(0), mc(1), g_i, w_pad, tabs_c, qn, kn, bd,
                           tm=lc, rows_per_seg=b * lc, seq=lc)
        zl, xbcl, dtl, gql, gkl, gvl, wql, wkl, wvl, ul = pl_
        zc, xbcc, dtc, gqc, gkc, gvc, wqc, wkc, wvc, uc = pc_

        conv_w = ssd_conv_w[i]
        conv_b = ssd_conv_b[i].reshape(1, -1)
        alog_r = _lane_pad(ssd_a_log[i])
        bias_r = _lane_pad(ssd_dt_bias[i])
        alog_c = ssd_a_log[i].reshape(-1, 1)
        bias_c = ssd_dt_bias[i].reshape(-1, 1)
        dsk = jnp.repeat(ssd_d[i], HEAD_DIM).reshape(1, D_SSD)
        n_dt = N_DIRS * SSD_HEADS

        def ssd(xbc2, dt2, h0, n):
            dt3 = dt2.reshape(b, n, LANES)
            return _ssd_call(xbc2.reshape(b, n, SSD_XBC), dt3, dt3[:, :, :n_dt].transpose(0, 2, 1), h0,
                             conv_w, conv_b, alog_r, bias_r, alog_c, bias_c, dsk)

        h0 = jnp.zeros((b, N_DIRS, SSD_GROUPS, SSD_STATE, hpg * HEAD_DIM), F32)
        yfc, ybc, hc = ssd(xbcc, dtc, h0, lc)
        yfl, ybl, _ = ssd(xbcl, dtl, hc, s)

        kx_ga, vx_ga = _kv_rows(gkc, b), _kv_cols(gvc, b)
        k_all = jnp.concatenate([kx_ga, _kv_rows(gkl, b)], axis=2)
        v_all = jnp.concatenate([vx_ga, _kv_cols(gvl, b)], axis=3)
        y_ga = _heads_back(_flash_call(_heads_t(gql, b, hpg), k_all, v_all, None))

        kx_wa, vx_wa = _kv_rows(wkc, b), _kv_cols(wvc, b)
        sink = wa_sink[i].astype(F32)
        y_wa = _heads_back(_win_call(_heads_t(wql, b, hpg), _kv_rows(wkl, b), _kv_cols(wvl, b),
                                     kx_wa, vx_wa, sink))

        gm = lambda a: jnp.moveaxis(a, 1, 0)
        lre = gm(s5_lambda_re[i])[:, :, None, :]
        lim = gm(s5_lambda_im[i])[:, :, None, :]
        ls = gm(s5_log_step[i])[:, :, None, None]
        bre = gm(s5_b_re[i]).transpose(0, 1, 3, 2)
        bim = gm(s5_b_im[i]).transpose(0, 1, 3, 2)
        cre, cim = gm(s5_c_re[i]), gm(s5_c_im[i])
        y5l, y5c = _s5_call(_s5_rows(ul, b), _s5_rows(uc, b), lre, lim, ls, bre, bim, cre, cim, nb=b)
        y5l, y5c = _s5_back(y5l, b), _s5_back(y5c, b)

        ng = ssd_norm_g[i].reshape(1, D_SSD)
        d5 = s5_d[i].reshape(1, D_S5)
        wglu = s5_w_glu[i].astype(BF16)
        bglu = s5_b_glu[i].reshape(1, -1)
        wout = w_out[i].astype(BF16)
        x2 = _outproj_call(x2, ml(2), yfl.reshape(b * s, -1), ybl.reshape(b * s, -1), zl, y_ga, y_wa,
                           y5l, ul, ng, d5, wglu, bglu, wout, tm=tm, rows_per_seg=s)

        gf_i = g_ffn[i].reshape(1, d)
        wgu = jnp.concatenate([w_gate[i], w_up[i]], axis=1).astype(BF16)
        cw = ffn_conv_w[i]
        cb = ffn_conv_b[i].reshape(1, -1)
        wd = w_down[i].astype(BF16)
        x2 = _ffn_call(x2, ml(3), ml(4), ml(5), gf_i, wgu, cw, cb, wd, gfin, tm=256, rows_per_seg=s,
                       seq=s, final_norm=not need_ctx)

        if need_ctx:
            yc_ga = _heads_back(_flash_call(_heads_t(gqc, b, hpg), kx_ga, vx_ga, None))
            yc_wa = _heads_back(_flash_call(_heads_t(wqc, b, hpg), kx_wa, vx_wa, sink))
            xc2 = _outproj_call(xc2, mc(2), yfc.reshape(b * lc, -1), ybc.reshape(b * lc, -1), zc, yc_ga,
                                yc_wa, y5c, uc, ng, d5, wglu, bglu, wout, tm=lc, rows_per_seg=b * lc)
            xc2 = _ffn_call(xc2, mc(3), mc(4), mc(5), gf_i, wgu, cw, cb, wd, gfin, tm=lc,
                            rows_per_seg=b * lc, seq=lc, final_norm=False)
    return x2.reshape(b, s, d)
```

```python
import functools

import jax
import jax.numpy as jnp
from jax import lax
from jax.experimental import pallas as pl
from jax.experimental.pallas import tpu as pltpu

F32 = jnp.float32
BF16 = jnp.bfloat16
HIGHEST = lax.Precision.HIGHEST

HEAD_DIM = 64
GRID_W = 64
ROPE_THETA = 10000.0
NORM_EPS = 1e-6
WINDOW = 128
N_DIRS = 2
SSD_HEADS = 4
SSD_GROUPS = 2
SSD_STATE = 128
SSD_CHUNK = 128
D_SSD = SSD_HEADS * HEAD_DIM
SSD_XBC = D_SSD + 2 * SSD_GROUPS * SSD_STATE
S5_GROUPS = 16
S5_GROUP_CH = 16
S5_STATE = 64
S5_MAX_RE = -1e-4
S5_CHUNK = 32
D_S5 = S5_GROUPS * S5_GROUP_CH
LANES = 128
HALO = 8
NEG_BIG = -1e30
LOG2E = 1.4426950408889634
Q_HEADS = 4
KV_HEADS = 2
HPG = Q_HEADS // KV_HEADS
ONES_ROWS = 16
FLASH_SUB_K = 128
FLASH_SUB_Q = 256
VMEM_LIMIT = 52 * 1024 * 1024

IN_SIZES = (D_SSD, SSD_XBC, N_DIRS * SSD_HEADS, 256, 128, 128, 256, 128, 128, D_S5)
P_Z, P_XBC, P_DT, P_GQ, P_GK, P_GV, P_WQ, P_WK, P_WV, P_U, P_END = (
    0, 256, 1024, 1152, 1408, 1536, 1664, 1920, 2048, 2176, 2432)


def _params(sem=None):
    return pltpu.CompilerParams(dimension_semantics=sem, vmem_limit_bytes=VMEM_LIMIT)


def _silu(v):
    return v * jax.nn.sigmoid(v)


def _softplus(v):
    return jnp.maximum(v, 0.0) + jnp.log1p(jnp.exp(-jnp.abs(v)))


def _dot(a, b):
    return jnp.dot(a, b, preferred_element_type=F32)


def _dot_nt(a, b, precision=None):
    return lax.dot_general(a, b, (((1,), (1,)), ((), ())), preferred_element_type=F32,
                           precision=precision)


def _dot_tn(a, b):
    return lax.dot_general(a, b, (((0,), (0,)), ((), ())), preferred_element_type=F32)


def _mod_kernel(cc_ref, w_ref, b_ref, o_ref):
    s = _silu(cc_ref[...])
    o_ref[0] = jnp.dot(s, w_ref[0], preferred_element_type=F32, precision=HIGHEST) + b_ref[0]


def _mod_call(cc, w_mod, b_mod):
    n_layers, d, n = w_mod.shape
    tn = 1536
    return pl.pallas_call(
        _mod_kernel,
        grid=(n_layers, n // tn),
        in_specs=[pl.BlockSpec((8, d), lambda l, j: (0, 0)),
                  pl.BlockSpec((1, d, tn), lambda l, j: (l, 0, j)),
                  pl.BlockSpec((1, 1, tn), lambda l, j: (l, 0, j))],
        out_specs=pl.BlockSpec((1, 8, tn), lambda l, j: (l, 0, j)),
        out_shape=jax.ShapeDtypeStruct((n_layers, 8, n), F32),
        compiler_params=_params(("arbitrary", "arbitrary")),
        name="adaln_mod",
    )(cc, w_mod, b_mod.reshape(n_layers, 1, n))


def _rms_mod(x, g, shift, scale):
    y = x * lax.rsqrt(jnp.mean(x * x, axis=-1, keepdims=True) + NORM_EPS) * g
    return y * (1.0 + scale) + shift


def _rope(t, cos, sna, snb):
    return t * cos + pltpu.roll(t, 96, 1) * sna + pltpu.roll(t, 32, 1) * snb


def _head_rms(t, gain, bd):
    t2 = t * t
    hi = t2.astype(BF16)
    lo = (t2 - hi.astype(F32)).astype(BF16)
    ms = _dot(hi, bd) + _dot(lo, bd)
    return t * lax.rsqrt(ms + NORM_EPS) * gain


def _inproj_kernel(x_ref, sh_ref, sc_ref, g_ref, w_ref, cos_ref, sna_ref, snb_ref, qn_ref, kn_ref,
                   bd_ref, z_ref, xbc_ref, dt_ref, gq_ref, gk_ref, gv_ref, wq_ref, wk_ref, wv_ref,
                   u_ref):
    h = _rms_mod(x_ref[...], g_ref[...], sh_ref[0], sc_ref[0])
    p = _dot(h.astype(BF16), w_ref[...])
    cos, sna, snb = cos_ref[...], sna_ref[...], snb_ref[...]
    bd = bd_ref[...]
    q_scale = HEAD_DIM ** -0.5 * LOG2E
    z_ref[...] = p[:, P_Z:P_XBC]
    xbc_ref[...] = p[:, P_XBC:P_DT]
    dt_ref[...] = p[:, P_DT:P_GQ]
    for j in range(2):
        lo = P_GQ + j * LANES
        q = _rope(_head_rms(p[:, lo:lo + LANES], qn_ref[...], bd), cos, sna, snb) * q_scale
        gq_ref[:, j * LANES:(j + 1) * LANES] = q.astype(BF16)
        lo = P_WQ + j * LANES
        q = _rope(p[:, lo:lo + LANES], cos, sna, snb) * q_scale
        wq_ref[:, j * LANES:(j + 1) * LANES] = q.astype(BF16)
    gk_ref[...] = _rope(_head_rms(p[:, P_GK:P_GV], kn_ref[...], bd), cos, sna, snb).astype(BF16)
    gv_ref[0] = jnp.transpose(p[:, P_GV:P_WQ]).astype(BF16)
    wk_ref[...] = _rope(p[:, P_WK:P_WV], cos, sna, snb).astype(BF16)
    wv_ref[0] = jnp.transpose(p[:, P_WV:P_U]).astype(BF16)
    u_ref[...] = p[:, P_U:P_END]


def _inproj_call(x2, shift, scale, g, w_pad, tabs, qn, kn, bd, *, tm, rows_per_seg, seq):
    r, d = x2.shape
    tps = rows_per_seg // tm
    tpq = seq // tm
    row = lambda w: pl.BlockSpec((tm, w), lambda i: (i, 0))
    seg = pl.BlockSpec((1, 1, d), lambda i: (i // tps, 0, 0))
    full = lambda a: pl.BlockSpec(a.shape, lambda i: (0,) * a.ndim)
    tab = pl.BlockSpec((tm, LANES), lambda i: (i % tpq, 0))
    widths = (256, SSD_XBC, LANES, 256, 128, None, 256, 128, None, D_S5)
    dtypes = (F32, F32, F32, BF16, BF16, BF16, BF16, BF16, BF16, F32)
    vt_spec = pl.BlockSpec((1, LANES, tm), lambda i: (i // tpq, 0, i % tpq))
    vt_shape = (r // seq, LANES, seq)
    return pl.pallas_call(
        _inproj_kernel,
        grid=(r // tm,),
        in_specs=[row(d), seg, seg, full(g), full(w_pad), tab, tab, tab, full(qn), full(kn), full(bd)],
        out_specs=[vt_spec if w is None else row(w) for w in widths],
        out_shape=[jax.ShapeDtypeStruct(vt_shape if w is None else (r, w), t) for w, t in zip(widths, dtypes)],
        compiler_params=_params(("arbitrary",)),
        name="inproj",
    )(x2, shift, scale, g, w_pad, *tabs, qn, kn, bd)


def _ssd_dir(d, ce, nc, x_ref, xp_ref, xn_ref, dtc_ref, dtr_ref, cw_ref, cb_ref, al_ref, bi_ref,
             alc_ref, bic_ref, dsk_ref, h_sc, y_ref):
    q = SSD_CHUNK
    x = x_ref[0]
    row = lax.broadcasted_iota(jnp.int32, (q, 1), 0)
    prev = jnp.where(ce > 0, xp_ref[0][HALO - 1:HALO, :], 0.0)
    nxt = jnp.where(ce < nc - 1, xn_ref[0][0:1, :], 0.0)
    up = jnp.where(row == 0, prev, pltpu.roll(x, 1, 0))
    dn = jnp.where(row == q - 1, nxt, pltpu.roll(x, q - 1, 0))
    cw = cw_ref[...]
    act = _silu(cw[0:1] * up + cw[1:2] * x + cw[2:3] * dn + cb_ref[...])
    xs = act[:, 0:D_SSD]
    dt_c = _softplus(dtc_ref[0] + bi_ref[...])
    dta_c = dt_c * (-jnp.exp(al_ref[...]))
    dt_r = _softplus(dtr_ref[0] + bic_ref[...])
    dta_r = dt_r * (-jnp.exp(alc_ref[...]))
    ri = lax.broadcasted_iota(jnp.int32, (q, q), 0)
    ci = lax.broadcasted_iota(jnp.int32, (q, q), 1)
    mask = (ri >= ci) if d == 0 else (ri <= ci)
    maskf = mask.astype(F32)
    acs_c = jnp.dot(maskf, dta_c, preferred_element_type=F32, precision=HIGHEST)
    acs_r = _dot_nt(dta_r, maskf, precision=HIGHEST)
    tot = jnp.sum(dta_c, axis=0, keepdims=True)
    lane = lax.broadcasted_iota(jnp.int32, (1, LANES), 1)
    first = lane < HEAD_DIM
    hpg = SSD_HEADS // SSD_GROUPS
    for g in range(SSD_GROUPS):
        e0 = d * SSD_HEADS + g * hpg
        bm = act[:, D_SSD + g * SSD_STATE:D_SSD + (g + 1) * SSD_STATE].astype(BF16)
        cm = act[:, D_SSD + (SSD_GROUPS + g) * SSD_STATE:D_SSD + (SSD_GROUPS + g + 1) * SSD_STATE].astype(BF16)
        pick = lambda v: jnp.where(first, v[:, e0:e0 + 1], v[:, e0 + 1:e0 + 2])
        dt_g = pick(dt_c)
        acs_g = pick(acs_c)
        tot_g = pick(tot)
        xs_g = xs[:, g * LANES:(g + 1) * LANES]
        xdt = xs_g * dt_g
        xdt_b = xdt.astype(BF16)
        cb = _dot_nt(cm, bm)
        yd = []
        for a in range(hpg):
            e = e0 + a
            dec = jnp.where(mask, jnp.exp(acs_c[:, e:e + 1] - acs_r[e:e + 1, :]), 0.0)
            yd.append(_dot((cb * dec).astype(BF16), xdt_b))
        h_old = h_sc[d, g]
        y = jnp.where(first, yd[0], yd[1]) + _dot(cm, h_old.astype(BF16)) * jnp.exp(acs_g)
        if d == 0:
            y = y + dsk_ref[:, g * LANES:(g + 1) * LANES] * xs_g
        y_ref[0, :, g * LANES:(g + 1) * LANES] = y
        h_sc[d, g] = h_old * jnp.exp(tot_g) + _dot_tn(bm, (xdt * jnp.exp(tot_g - acs_g)).astype(BF16))


def _ssd_kernel(xf_ref, xfp_ref, xfn_ref, xb_ref, xbp_ref, xbn_ref, dtcf_ref, dtcb_ref, dtrf_ref,
                dtrb_ref, h0_ref, cw_ref, cb_ref, al_ref, bi_ref, alc_ref, bic_ref, dsk_ref,
                yf_ref, yb_ref, hout_ref, h_sc, *, nc):
    c = pl.program_id(1)

    @pl.when(c == 0)
    def _():
        h_sc[...] = h0_ref[0]

    shared = (cw_ref, cb_ref, al_ref, bi_ref, alc_ref, bic_ref, dsk_ref, h_sc)
    _ssd_dir(0, c, nc, xf_ref, xfp_ref, xfn_ref, dtcf_ref, dtrf_ref, *shared, yf_ref)
    _ssd_dir(1, nc - 1 - c, nc, xb_ref, xbp_ref, xbn_ref, dtcb_ref, dtrb_ref, *shared, yb_ref)

    @pl.when(c == nc - 1)
    def _():
        hout_ref[0] = h_sc[...]


def _ssd_call(xbc, dt_pad, dt_t, h0, conv_w, conv_b, alog_r, bias_r, alog_c, bias_c, dsk):
    b, s, w = xbc.shape
    q = SSD_CHUNK
    nc = s // q
    hb = q // HALO
    last = s // HALO - 1
    fwd = lambda c: c
    bwd = lambda c: nc - 1 - c
    xspec = lambda f: pl.BlockSpec((1, q, w), lambda i, c: (i, f(c), 0))
    pspec = lambda f: pl.BlockSpec((1, HALO, w), lambda i, c: (i, jnp.maximum(f(c) * hb - 1, 0), 0))
    nspec = lambda f: pl.BlockSpec((1, HALO, w), lambda i, c: (i, jnp.minimum(f(c) * hb + hb, last), 0))
    dcspec = lambda f: pl.BlockSpec((1, q, LANES), lambda i, c: (i, f(c), 0))
    drspec = lambda f: pl.BlockSpec((1, 8, q), lambda i, c: (i, 0, f(c)))
    full = lambda a: pl.BlockSpec(a.shape, lambda i, c: (0,) * a.ndim)
    hspec = pl.BlockSpec((1,) + h0.shape[1:], lambda i, c: (i, 0, 0, 0, 0))
    yspec = lambda f: pl.BlockSpec((1, q, D_SSD), lambda i, c: (i, f(c), 0))
    return pl.pallas_call(
        functools.partial(_ssd_kernel, nc=nc),
        grid=(b, nc),
        in_specs=[xspec(fwd), pspec(fwd), nspec(fwd), xspec(bwd), pspec(bwd), nspec(bwd),
                  dcspec(fwd), dcspec(bwd), drspec(fwd), drspec(bwd), hspec,
                  full(conv_w), full(conv_b), full(alog_r), full(bias_r), full(alog_c), full(bias_c),
                  full(dsk)],
        out_specs=[yspec(fwd), yspec(bwd), hspec],
        out_shape=[jax.ShapeDtypeStruct((b, s, D_SSD), F32), jax.ShapeDtypeStruct((b, s, D_SSD), F32),
                   jax.ShapeDtypeStruct(h0.shape, F32)],
        scratch_shapes=[pltpu.VMEM(h0.shape[1:], F32)],
        compiler_params=_params(("arbitrary", "arbitrary")),
        name="ssd",
    )(xbc, xbc, xbc, xbc, xbc, xbc, dt_pad, dt_pad, dt_t, dt_t, h0, conv_w, conv_b, alog_r, bias_r,
      alog_c, bias_c, dsk)


def _q_transposed(q):
    qt = jnp.transpose(q.astype(F32))
    zero = jnp.zeros((HEAD_DIM, q.shape[0]), BF16)
    out = []
    for h in range(Q_HEADS):
        blk = qt[h * HEAD_DIM:(h + 1) * HEAD_DIM].astype(BF16)
        out.append(jnp.concatenate([blk, zero] if h // HPG == 0 else [zero, blk], axis=0))
    return out


def _flash_kernel(sink_ref, q_ref, k_ref, vt_ref, o_ref, qt_sc, m_sc, acc_sc, *, nk, has_sink):
    ki = pl.program_id(2)
    tq = q_ref.shape[0]
    tk = k_ref.shape[1]

    @pl.when(ki == 0)
    def _():
        qts = _q_transposed(q_ref[...])
        for h in range(Q_HEADS):
            qt_sc[h] = qts[h]
            if has_sink:
                m_sc[h] = jnp.full(m_sc.shape[1:], sink_ref[h] * LOG2E, F32)
                acc_sc[h] = jnp.concatenate([jnp.zeros((HEAD_DIM, tq), F32), jnp.ones((ONES_ROWS, tq), F32)], axis=0)
            else:
                m_sc[h] = jnp.full(m_sc.shape[1:], NEG_BIG, F32)
                acc_sc[h] = jnp.zeros(acc_sc.shape[1:], F32)

    k = k_ref[0]
    ones = jnp.ones((ONES_ROWS, tk), BF16)
    vte = [jnp.concatenate([vt_ref[0, g * HEAD_DIM:(g + 1) * HEAD_DIM, :], ones], axis=0) for g in range(KV_HEADS)]
    units = [(h, slice(jq * FLASH_SUB_Q, (jq + 1) * FLASH_SUB_Q))
             for h in range(Q_HEADS) for jq in range(tq // FLASH_SUB_Q)]

    def scores(u):
        h, cs = u
        return _dot(k, qt_sc[h, :, cs])

    def finish(u, s):
        h, cs = u
        m_prev = m_sc[h, :, cs]
        m_new = jnp.maximum(m_prev, jnp.max(s, axis=0, keepdims=True))
        p = jnp.exp2(s - m_new).astype(BF16)
        acc_sc[h, :, cs] = jnp.exp2(m_prev - m_new) * acc_sc[h, :, cs] + _dot(vte[h // HPG], p)
        m_sc[h, :, cs] = m_new

    s_next = scores(units[0])
    for i, u in enumerate(units):
        s_cur = s_next
        if i + 1 < len(units):
            s_next = scores(units[i + 1])
        finish(u, s_cur)

    @pl.when(ki == nk - 1)
    def _():
        outs = []
        for h in range(Q_HEADS):
            acc = acc_sc[h]
            outs.append(acc[:HEAD_DIM] / acc[HEAD_DIM:HEAD_DIM + 1])
        o_ref[...] = jnp.transpose(jnp.concatenate(outs, axis=0)).astype(o_ref.dtype)


def _pick(n, cands):
    for c in cands:
        if n % c == 0:
            return c
    return n


def _flash_call(q2, k, vt, sink):
    b, sk, kw = k.shape
    sq = q2.shape[0] // b
    tq = _pick(sq, (512, 256))
    tk = _pick(sk, (768, 512, 256))
    nq = sq // tq
    nk = sk // tk
    has_sink = sink is not None
    if sink is None:
        sink = jnp.zeros((Q_HEADS,), F32)
    return pl.pallas_call(
        functools.partial(_flash_kernel, nk=nk, has_sink=has_sink),
        grid=(b, nq, nk),
        in_specs=[pl.BlockSpec(memory_space=pltpu.SMEM),
                  pl.BlockSpec((tq, q2.shape[1]), lambda i, qi, ki: (i * nq + qi, 0)),
                  pl.BlockSpec((1, tk, kw), lambda i, qi, ki: (i, ki, 0)),
                  pl.BlockSpec((1, kw, tk), lambda i, qi, ki: (i, 0, ki))],
        out_specs=pl.BlockSpec((tq, q2.shape[1]), lambda i, qi, ki: (i * nq + qi, 0)),
        out_shape=jax.ShapeDtypeStruct(q2.shape, BF16),
        scratch_shapes=[pltpu.VMEM((Q_HEADS, kw, tq), BF16), pltpu.VMEM((Q_HEADS, 1, tq), F32),
                        pltpu.VMEM((Q_HEADS, HEAD_DIM + ONES_ROWS, tq), F32)],
        compiler_params=_params(("arbitrary", "arbitrary", "arbitrary")),
        name="flash_sink" if has_sink else "flash",
    )(sink, q2, k, vt)


def _win_kernel(sink_ref, q_ref, kp_ref, kc_ref, kn_ref, kx_ref, vtp_ref, vtc_ref, vtn_ref, vtx_ref,
                o_ref, *, nq):
    j = pl.program_id(1)
    tq = q_ref.shape[0]
    blk = kp_ref.shape[1]
    rc = lax.broadcasted_iota(jnp.int32, (tq, tq), 0) - lax.broadcasted_iota(jnp.int32, (tq, tq), 1)
    m_cur = (rc <= WINDOW) & (rc >= -WINDOW)
    rp = lax.broadcasted_iota(jnp.int32, (blk, tq), 0) - lax.broadcasted_iota(jnp.int32, (blk, tq), 1)
    m_prev = (rp - blk >= -WINDOW) & (j > 0)
    m_next = (rp + tq <= WINDOW) & (j < nq - 1)
    qts = _q_transposed(q_ref[...])
    outs = []
    for h in range(Q_HEADS):
        qt = qts[h]
        rows = slice((h // HPG) * HEAD_DIM, (h // HPG + 1) * HEAD_DIM)
        sink = sink_ref[h] * LOG2E
        s_c = jnp.where(m_cur, _dot(kc_ref[0], qt), NEG_BIG)
        s_p = jnp.where(m_prev, _dot(kp_ref[0], qt), NEG_BIG)
        s_n = jnp.where(m_next, _dot(kn_ref[0], qt), NEG_BIG)
        s_x = _dot(kx_ref[0], qt)
        cmax = lambda v: jnp.max(v, axis=0, keepdims=True)
        m = jnp.maximum(jnp.maximum(cmax(s_c), cmax(s_p)), jnp.maximum(cmax(s_n), cmax(s_x)))
        m = jnp.maximum(m, sink)
        p_c, p_p, p_n, p_x = (jnp.exp2(v - m) for v in (s_c, s_p, s_n, s_x))
        csum = lambda v: jnp.sum(v, axis=0, keepdims=True)
        l = csum(p_c) + csum(p_p) + csum(p_n) + csum(p_x) + jnp.exp2(sink - m)
        o = (_dot(vtc_ref[0, rows, :], p_c.astype(BF16)) + _dot(vtp_ref[0, rows, :], p_p.astype(BF16))
             + _dot(vtn_ref[0, rows, :], p_n.astype(BF16)) + _dot(vtx_ref[0, rows, :], p_x.astype(BF16)))
        outs.append(o / l)
    o_ref[...] = jnp.transpose(jnp.concatenate(outs, axis=0)).astype(o_ref.dtype)


def _win_call(q2, k, vt, kx, vtx, sink):
    b, s, kw = k.shape
    lc = kx.shape[1]
    blk = WINDOW
    tq = _pick(s, (512, 256, 128))
    nq = s // tq
    r = tq // blk
    nblk = s // blk
    prev = lambda t: jnp.maximum(t * r - 1, 0)
    nxt = lambda t: jnp.minimum(t * r + r, nblk - 1)
    return pl.pallas_call(
        functools.partial(_win_kernel, nq=nq),
        grid=(b, nq),
        in_specs=[pl.BlockSpec(memory_space=pltpu.SMEM),
                  pl.BlockSpec((tq, q2.shape[1]), lambda i, t: (i * nq + t, 0)),
                  pl.BlockSpec((1, blk, kw), lambda i, t: (i, prev(t), 0)),
                  pl.BlockSpec((1, tq, kw), lambda i, t: (i, t, 0)),
                  pl.BlockSpec((1, blk, kw), lambda i, t: (i, nxt(t), 0)),
                  pl.BlockSpec((1, lc, kw), lambda i, t: (i, 0, 0)),
                  pl.BlockSpec((1, kw, blk), lambda i, t: (i, 0, prev(t))),
                  pl.BlockSpec((1, kw, tq), lambda i, t: (i, 0, t)),
                  pl.BlockSpec((1, kw, blk), lambda i, t: (i, 0, nxt(t))),
                  pl.BlockSpec((1, kw, lc), lambda i, t: (i, 0, 0))],
        out_specs=pl.BlockSpec((tq, q2.shape[1]), lambda i, t: (i * nq + t, 0)),
        out_shape=jax.ShapeDtypeStruct(q2.shape, BF16),
        compiler_params=_params(("arbitrary", "arbitrary")),
        name="window_attn",
    )(sink, q2, k, k, k, kx, vt, vt, vt, vtx)


def _rep_rows(p, n):
    return jnp.concatenate([jnp.broadcast_to(p[s:s + 1, :], (n, p.shape[1])) for s in range(p.shape[0])], axis=0)


def _ctab(pr, pi, mr, mi):
    big_l, k = pr.shape[0], mr.shape[0]
    er, ei = _rep_rows(pr, k), _rep_rows(pi, k)
    tr, ti = jnp.tile(mr, (big_l, 1)), jnp.tile(mi, (big_l, 1))
    return er * tr - ei * ti, er * ti + ei * tr


def _cmul_rows(cr, ci, s):
    n = cr.shape[1]
    return jnp.concatenate([cr, cr], axis=1) * s + jnp.concatenate([-ci, ci], axis=1) * pltpu.roll(s, n, 1)


def _seg_scan(x, cr, ci, rowm, nper, reverse):
    rows = x.shape[0]
    s, sh = x, 1
    while sh < nper:
        if reverse:
            shifted, valid = pltpu.roll(s, rows - sh, 0), rowm < nper - sh
        else:
            shifted, valid = pltpu.roll(s, sh, 0), rowm >= sh
        s = s + jnp.where(valid, _cmul_rows(cr, ci, shifted), 0.0)
        cr, ci = cr * cr - ci * ci, 2.0 * cr * ci
        sh *= 2
    return s


def _s5_kernel(ul_ref, ux_ref, lre_ref, lim_ref, ls_ref, bre_ref, bim_ref, cre_ref, cim_ref,
               yl_ref, yx_ref, t_sc, *, nb, ncl, ncx):
    big_l = S5_CHUNK
    k = S5_GROUP_CH
    n = S5_STATE
    w = big_l * k
    tau = lax.broadcasted_iota(jnp.int32, (big_l, 1), 0).astype(F32)
    tabs = []
    for d in range(N_DIRS):
        lr = jnp.minimum(lre_ref[0, d], S5_MAX_RE)
        li = lim_ref[0, d]
        dl = jnp.exp(ls_ref[0, d])
        ar, th = lr * dl, li * dl

        def power(t, ar=ar, th=th):
            mag = jnp.exp(t * ar)
            return mag * jnp.cos(t * th), mag * jnp.sin(t * th)

        lbr, lbi = power(1.0)
        den = lr * lr + li * li
        zr = ((lbr - 1.0) * lr + lbi * li) / den
        zi = (lbi * lr - (lbr - 1.0) * li) / den
        br, bi = bre_ref[0, d], bim_ref[0, d]
        bbr, bbi = zr * br - zi * bi, zr * bi + zi * br
        cr, ci = cre_ref[0, d], cim_ref[0, d]
        tabs.append(dict(power=power, bbr=bbr, bbi=bbi, cr=cr, ci=ci, lam_l=power(float(big_l))))
    f, b = tabs
    e_f = _ctab(*f["power"](big_l - 1.0 - tau), f["bbr"], f["bbi"])
    e_b = _ctab(*b["power"](tau), b["bbr"], b["bbi"])
    g_f = _ctab(*f["power"](tau + 1.0), f["cr"], f["ci"])
    g_b = _ctab(*b["power"](big_l - tau), b["cr"], b["ci"])
    a_b = _ctab(*b["power"](big_l - 1.0 - tau), b["cr"], b["ci"])
    c_f = jnp.tile(f["cr"], (big_l, 1)), jnp.tile(f["ci"], (big_l, 1))
    bb2 = lambda t: jnp.concatenate([t["bbr"], t["bbi"]], axis=1)
    neg = lambda t: jnp.concatenate([t[0], -t[1]], axis=1)
    ka = _dot_nt(bb2(b), neg(a_b), precision=HIGHEST)
    kb = _dot_nt(bb2(f), neg(g_f), precision=HIGHEST)
    kc = _dot_nt(bb2(f), neg(c_f), precision=HIGHEST)
    lane = lax.broadcasted_iota(jnp.int32, (1, w), 1)
    ka = ka + jnp.where(lane >= w - k, kc, 0.0)
    kall = jnp.concatenate([ka, kb], axis=1)
    for s in range(big_l):
        off = (big_l - 1 - s) * k
        t_sc[s * k:(s + 1) * k, :] = kall[:, off:off + w].astype(BF16)
    ecat = jnp.concatenate([e_f[0], e_f[1], e_b[0], e_b[1]], axis=1).astype(BF16)
    gcat = jnp.concatenate([g_f[0], -g_f[1], g_b[0], -g_b[1]], axis=1).astype(BF16)
    tmat = t_sc[...]

    def rowmod(nper):
        return jnp.concatenate([lax.broadcasted_iota(jnp.int32, (nper, 1), 0)] * nb, axis=0)

    def by_batch(rows, nper):
        return jnp.concatenate([jnp.broadcast_to(r, (nper, r.shape[1])) for r in rows], axis=0)

    ux = ux_ref[0]
    hx = _dot(ux, ecat)
    rmx = rowmod(ncx)
    rx = nb * ncx
    sxf = _seg_scan(hx[:, :2 * n], *f["lam_l"], rmx, ncx, False)
    sxb = _seg_scan(hx[:, 2 * n:], *b["lam_l"], rmx, ncx, True)
    hin_xf = jnp.where(rmx >= 1, pltpu.roll(sxf, 1, 0), 0.0)
    hin_xb = jnp.where(rmx < ncx - 1, pltpu.roll(sxb, rx - 1, 0), 0.0)
    hin_x = jnp.concatenate([hin_xf, hin_xb], axis=1).astype(BF16)
    yx_ref[0] = _dot(ux, tmat) + _dot_nt(hin_x, gcat)
    hc_f = [sxf[i * ncx + ncx - 1:i * ncx + ncx, :] for i in range(nb)]
    hc_b = [sxb[i * ncx:i * ncx + 1, :] for i in range(nb)]
    ul = ul_ref[0]
    hl = _dot(ul, ecat)
    rml = rowmod(ncl)
    rl = nb * ncl
    xf = jnp.where(rml == 0, by_batch(hc_f, ncl), pltpu.roll(hl[:, :2 * n], 1, 0))
    xb = jnp.where(rml == ncl - 1, by_batch(hc_b, ncl), pltpu.roll(hl[:, 2 * n:], rl - 1, 0))
    hin_f = _seg_scan(xf, *f["lam_l"], rml, ncl, False)
    hin_b = _seg_scan(xb, *b["lam_l"], rml, ncl, True)
    hin = jnp.concatenate([hin_f, hin_b], axis=1).astype(BF16)
    yl_ref[0] = _dot(ul, tmat) + _dot_nt(hin, gcat)


def _s5_call(ul, ux, lre, lim, ls, bre, bim, cre, cim, *, nb):
    g, rl, w = ul.shape
    rx = ux.shape[1]
    grp = lambda a: pl.BlockSpec((1,) + a.shape[1:], lambda i: (i,) + (0,) * (a.ndim - 1))
    return pl.pallas_call(
        functools.partial(_s5_kernel, nb=nb, ncl=rl // nb, ncx=rx // nb),
        grid=(g,),
        in_specs=[grp(a) for a in (ul, ux, lre, lim, ls, bre, bim, cre, cim)],
        out_specs=[grp(ul), grp(ux)],
        out_shape=[jax.ShapeDtypeStruct(ul.shape, F32), jax.ShapeDtypeStruct(ux.shape, F32)],
        scratch_shapes=[pltpu.VMEM((w, w), BF16)],
        compiler_params=_params(("arbitrary",)),
        name="s5",
    )(ul, ux, lre, lim, ls, bre, bim, cre, cim)


def _outproj_kernel(x_ref, gt_ref, yf_ref, yb_ref, z_ref, ga_ref, wa_ref, s5_ref, u_ref, ng_ref,
                    d5_ref, wglu_ref, bglu_ref, wout_ref, o_ref):
    ys = (yf_ref[...] + yb_ref[...]) * _silu(z_ref[...])
    ys = ys * lax.rsqrt(jnp.mean(ys * ys, axis=-1, keepdims=True) + NORM_EPS) * ng_ref[...]
    t = jax.nn.gelu(s5_ref[...] + d5_ref[...] * u_ref[...])
    t = _dot(t.astype(BF16), wglu_ref[...]) + bglu_ref[...]
    s5o = t[:, :D_S5] * jax.nn.sigmoid(t[:, D_S5:])
    cat = jnp.concatenate([ys.astype(BF16), ga_ref[...], wa_ref[...], s5o.astype(BF16)], axis=1)
    o_ref[...] = x_ref[...] + gt_ref[0] * _dot(cat, wout_ref[...])


def _outproj_call(x2, gate, yf, yb, z, ga, wa, s5, u, ng, d5, wglu, bglu, wout, *, tm, rows_per_seg):
    r, d = x2.shape
    tps = rows_per_seg // tm
    row = lambda a: pl.BlockSpec((tm, a.shape[1]), lambda i: (i, 0))
    seg = pl.BlockSpec((1, 1, d), lambda i: (i // tps, 0, 0))
    full = lambda a: pl.BlockSpec(a.shape, lambda i: (0,) * a.ndim)
    return pl.pallas_call(
        _outproj_kernel,
        grid=(r // tm,),
        in_specs=[row(x2), seg] + [row(a) for a in (yf, yb, z, ga, wa, s5, u)]
                 + [full(a) for a in (ng, d5, wglu, bglu, wout)],
        out_specs=row(x2),
        out_shape=jax.ShapeDtypeStruct(x2.shape, F32),
        compiler_params=_params(("arbitrary",)),
        name="outproj",
    )(x2, gate, yf, yb, z, ga, wa, s5, u, ng, d5, wglu, bglu, wout)


def _ffn_kernel(x_ref, xp_ref, xn_ref, sh_ref, sc_ref, gt_ref, g_ref, wgu_ref, cw_ref, cb_ref,
                wd_ref, gfin_ref, o_ref, *, tiles_per_seq, d_ff, final_norm):
    i = pl.program_id(0)
    tm = x_ref.shape[0]
    x = x_ref[...]
    g, sh, sc = g_ref[...], sh_ref[0], sc_ref[0]
    gu = _dot(_rms_mod(x, g, sh, sc).astype(BF16), wgu_ref[...])
    gate = gu[:, :d_ff]
    halo = jnp.concatenate([xp_ref[...], xn_ref[...]], axis=0)
    gh = _dot(_rms_mod(halo, g, sh, sc).astype(BF16), wgu_ref[:, :d_ff])
    t = i % tiles_per_seq
    prev = jnp.where(t > 0, gh[HALO - 1:HALO, :], 0.0)
    nxt = jnp.where(t < tiles_per_seq - 1, gh[HALO:HALO + 1, :], 0.0)
    row = lax.broadcasted_iota(jnp.int32, (tm, 1), 0)
    up = jnp.where(row == 0, prev, pltpu.roll(gate, 1, 0))
    dn = jnp.where(row == tm - 1, nxt, pltpu.roll(gate, tm - 1, 0))
    cw = cw_ref[...]
    conv = cw[0:1] * up + cw[1:2] * gate + cw[2:3] * dn + cb_ref[...]
    act = (_silu(conv) * gu[:, d_ff:]).astype(BF16)
    y = x + gt_ref[0] * _dot(act, wd_ref[...])
    if final_norm:
        y = y * lax.rsqrt(jnp.mean(y * y, axis=-1, keepdims=True) + NORM_EPS) * gfin_ref[...]
    o_ref[...] = y


def _ffn_call(x2, shift, scale, gate, g, wgu, cw, cb, wd, gfin, *, tm, rows_per_seg, seq, final_norm):
    r, d = x2.shape
    d_ff = wd.shape[0]
    tps = rows_per_seg // tm
    hb = tm // HALO
    last = r // HALO - 1
    seg = pl.BlockSpec((1, 1, d), lambda i: (i // tps, 0, 0))
    full = lambda a: pl.BlockSpec(a.shape, lambda i: (0,) * a.ndim)
    once = lambda a: pl.BlockSpec(a.shape, lambda i: (0,) * a.ndim, pipeline_mode=pl.Buffered(1))
    return pl.pallas_call(
        functools.partial(_ffn_kernel, tiles_per_seq=seq // tm, d_ff=d_ff, final_norm=final_norm),
        grid=(r // tm,),
        in_specs=[pl.BlockSpec((tm, d), lambda i: (i, 0)),
                  pl.BlockSpec((HALO, d), lambda i: (jnp.maximum(i * hb - 1, 0), 0)),
                  pl.BlockSpec((HALO, d), lambda i: (jnp.minimum(i * hb + hb, last), 0)),
                  seg, seg, seg, full(g), once(wgu), full(cw), full(cb), once(wd), full(gfin)],
        out_specs=pl.BlockSpec((tm, d), lambda i: (i, 0)),
        out_shape=jax.ShapeDtypeStruct(x2.shape, F32),
        compiler_params=_params(("arbitrary",)),
        name="conv_ffn",
    )(x2, x2, x2, shift, scale, gate, g, wgu, cw, cb, wd, gfin)


def _rope_tables(seq):
    rows = seq // GRID_W
    r = jnp.broadcast_to(jnp.arange(rows, dtype=F32)[:, None], (rows, GRID_W)).reshape(-1)
    c = jnp.broadcast_to(jnp.arange(GRID_W, dtype=F32)[None, :], (rows, GRID_W)).reshape(-1)
    n_freq = HEAD_DIM // 4
    inv = ROPE_THETA ** (-jnp.arange(n_freq, dtype=F32) / n_freq)
    ang = jnp.concatenate([r[:, None] * inv, c[:, None] * inv], axis=-1)
    cos, sin, zero = jnp.cos(ang), jnp.sin(ang), jnp.zeros_like(ang)
    reps = LANES // HEAD_DIM
    return (jnp.tile(jnp.concatenate([cos, cos], axis=-1), (1, reps)),
            jnp.tile(jnp.concatenate([-sin, zero], axis=-1), (1, reps)),
            jnp.tile(jnp.concatenate([zero, sin], axis=-1), (1, reps)))


def _pad_w_in(w):
    offs = [0]
    for s in IN_SIZES:
        offs.append(offs[-1] + s)
    parts = [w[:, offs[j]:offs[j + 1]] for j in range(len(IN_SIZES))]
    parts[2] = jnp.pad(parts[2], ((0, 0), (0, LANES - IN_SIZES[2])))
    return jnp.concatenate(parts, axis=1).astype(BF16)


def _s5_rows(u2, b):
    s = u2.shape[0] // b
    nc = s // S5_CHUNK
    u = u2.reshape(b, nc, S5_CHUNK, S5_GROUPS, S5_GROUP_CH).transpose(3, 0, 1, 2, 4)
    return u.reshape(S5_GROUPS, b * nc, S5_CHUNK * S5_GROUP_CH).astype(BF16)


def _s5_back(y, b):
    g, r, _ = y.shape
    nc = r // b
    y = y.reshape(g, b, nc, S5_CHUNK, S5_GROUP_CH).transpose(1, 2, 3, 0, 4)
    return y.reshape(b * nc * S5_CHUNK, g * S5_GROUP_CH)


def _lane_pad(v, width=LANES):
    return jnp.pad(v.reshape(1, -1), ((0, 0), (0, width - v.size)))


def kernel(x, c, ctx, c_ctx, w_mod, b_mod, g_mix, w_in, ssd_conv_w, ssd_conv_b, ssd_a_log, ssd_dt_bias, ssd_d, ssd_norm_g, ga_q_norm, ga_k_norm, wa_sink, s5_lambda_re, s5_lambda_im, s5_log_step, s5_b_re, s5_b_im, s5_c_re, s5_c_im, s5_d, s5_w_glu, s5_b_glu, w_out, g_ffn, w_gate, w_up, ffn_conv_w, ffn_conv_b, w_down, g_final):
    b, s, d = x.shape
    lc = ctx.shape[1]
    depth = w_mod.shape[0]
    hpg = 2
    tm = 512

    cc = jnp.zeros((8, d), F32).at[:b].set(c).at[b].set(c_ctx)
    mod = _mod_call(cc, w_mod, b_mod).reshape(depth, 8, 6, d)

    tabs_l = _rope_tables(s)
    one, zero = jnp.ones((lc, LANES), F32), jnp.zeros((lc, LANES), F32)
    tabs_c = (one, zero, zero)
    bd = jnp.kron(jnp.eye(LANES // HEAD_DIM, dtype=F32),
                  jnp.full((HEAD_DIM, HEAD_DIM), 1.0 / HEAD_DIM, F32)).astype(BF16)
    gfin = g_final.reshape(1, d)

    x2 = x.reshape(b * s, d)
    xc2 = ctx.reshape(b * lc, d)
    for i in range(depth):
        need_ctx = i < depth - 1
        ml = lambda j: mod[i, :b, j].reshape(b, 1, d)
        mc = lambda j: mod[i, b, j].reshape(1, 1, d)
        g_i = g_mix[i].reshape(1, d)
        w_pad = _pad_w_in(w_in[i])
        qn = jnp.tile(ga_q_norm[i], LANES // HEAD_DIM).reshape(1, LANES)
        kn = jnp.tile(ga_k_norm[i], LANES // HEAD_DIM).reshape(1, LANES)
        pl_ = _inproj_call(x2, ml(0), ml(1), g_i, w_pad, tabs_l, qn, kn, bd,
                           tm=tm, rows_per_seg=s, seq=s)
        pc_ = _inproj_call(xc2, mc(0), mc(1), g_i, w_pad, tabs_c, qn, kn, bd,
                           tm=lc, rows_per_seg=b * lc, seq=lc)
        zl, xbcl, dtl, gql, gkl, gvl, wql, wkl, wvl, ul = pl_
        zc, xbcc, dtc, gqc, gkc, gvc, wqc, wkc, wvc, uc = pc_

        conv_w = ssd_conv_w[i]
        conv_b = ssd_conv_b[i].reshape(1, -1)
        alog_r = _lane_pad(ssd_a_log[i])
        bias_r = _lane_pad(ssd_dt_bias[i])
        alog_c = ssd_a_log[i].reshape(-1, 1)
        bias_c = ssd_dt_bias[i].reshape(-1, 1)
        dsk = jnp.repeat(ssd_d[i], HEAD_DIM).reshape(1, D_SSD)
        n_dt = N_DIRS * SSD_HEADS

        def ssd(xbc2, dt2, h0, n):
            dt3 = dt2.reshape(b, n, LANES)
            return _ssd_call(xbc2.reshape(b, n, SSD_XBC), dt3, dt3[:, :, :n_dt].transpose(0, 2, 1), h0,
                             conv_w, conv_b, alog_r, bias_r, alog_c, bias_c, dsk)

        h0 = jnp.zeros((b, N_DIRS, SSD_GROUPS, SSD_STATE, hpg * HEAD_DIM), F32)
        yfc, ybc, hc = ssd(xbcc, dtc, h0, lc)
        yfl, ybl, _ = ssd(xbcl, dtl, hc, s)

        kx_ga = gkc.reshape(b, lc, -1)
        k_all = jnp.concatenate([kx_ga, gkl.reshape(b, s, -1)], axis=1)
        v_all = jnp.concatenate([gvc, gvl], axis=2)
        y_ga = _flash_call(gql, k_all, v_all, None)

        kx_wa = wkc.reshape(b, lc, -1)
        sink = wa_sink[i].astype(F32)
        y_wa = _win_call(wql, wkl.reshape(b, s, -1), wvl, kx_wa, wvc, sink)

        gm = lambda a: jnp.moveaxis(a, 1, 0)
        lre = gm(s5_lambda_re[i])[:, :, None, :]
        lim = gm(s5_lambda_im[i])[:, :, None, :]
        ls = gm(s5_log_step[i])[:, :, None, None]
        bre = gm(s5_b_re[i]).transpose(0, 1, 3, 2)
        bim = gm(s5_b_im[i]).transpose(0, 1, 3, 2)
        cre, cim = gm(s5_c_re[i]), gm(s5_c_im[i])
        y5l, y5c = _s5_call(_s5_rows(ul, b), _s5_rows(uc, b), lre, lim, ls, bre, bim, cre, cim, nb=b)
        y5l, y5c = _s5_back(y5l, b), _s5_back(y5c, b)

        ng = ssd_norm_g[i].reshape(1, D_SSD)
        d5 = s5_d[i].reshape(1, D_S5)
        wglu = s5_w_glu[i].astype(BF16)
        bglu = s5_b_glu[i].reshape(1, -1)
        wout = w_out[i].astype(BF16)
        x2 = _outproj_call(x2, ml(2), yfl.reshape(b * s, -1), ybl.reshape(b * s, -1), zl, y_ga, y_wa,
                           y5l, ul, ng, d5, wglu, bglu, wout, tm=tm, rows_per_seg=s)

        gf_i = g_ffn[i].reshape(1, d)
        wgu = jnp.concatenate([w_gate[i], w_up[i]], axis=1).astype(BF16)
        cw = ffn_conv_w[i]
        cb = ffn_conv_b[i].reshape(1, -1)
        wd = w_down[i].astype(BF16)
        x2 = _ffn_call(x2, ml(3), ml(4), ml(5), gf_i, wgu, cw, cb, wd, gfin, tm=256, rows_per_seg=s,
                       seq=s, final_norm=not need_ctx)

        if need_ctx:
            yc_ga = _flash_call(gqc, kx_ga, gvc, None)
            yc_wa = _flash_call(wqc, kx_wa, wvc, sink)
            xc2 = _outproj_call(xc2, mc(2), yfc.reshape(b * lc, -1), ybc.reshape(b * lc, -1), zc, yc_ga,
                                yc_wa, y5c, uc, ng, d5, wglu, bglu, wout, tm=lc, rows_per_seg=b * lc)
            xc2 = _ffn_call(xc2, mc(3), mc(4), mc(5), gf_i, wgu, cw, cb, wd, gfin, tm=lc,
                            rows_per_seg=b * lc, seq=lc, final_norm=False)
    return x2.reshape(b, s, d)
```

```python
import functools

import jax
import jax.numpy as jnp
from jax import lax
from jax.experimental import pallas as pl
from jax.experimental.pallas import tpu as pltpu

F32 = jnp.float32
BF16 = jnp.bfloat16
HIGHEST = lax.Precision.HIGHEST

HEAD_DIM = 64
GRID_W = 64
ROPE_THETA = 10000.0
NORM_EPS = 1e-6
WINDOW = 128
N_DIRS = 2
SSD_HEADS = 4
SSD_GROUPS = 2
SSD_STATE = 128
SSD_CHUNK = 128
D_SSD = SSD_HEADS * HEAD_DIM
SSD_XBC = D_SSD + 2 * SSD_GROUPS * SSD_STATE
S5_GROUPS = 16
S5_GROUP_CH = 16
S5_STATE = 64
S5_MAX_RE = -1e-4
S5_CHUNK = 32
D_S5 = S5_GROUPS * S5_GROUP_CH
LANES = 128
HALO = 8
NEG_BIG = -1e30
LOG2E = 1.4426950408889634
Q_HEADS = 4
KV_HEADS = 2
HPG = Q_HEADS // KV_HEADS
ONES_ROWS = 16
FLASH_SUB_K = 256
FLASH_SUB_Q = 256
FLASH_DEPTH = 6
VMEM_LIMIT = 52 * 1024 * 1024

IN_SIZES = (D_SSD, SSD_XBC, N_DIRS * SSD_HEADS, 256, 128, 128, 256, 128, 128, D_S5)
P_Z, P_XBC, P_DT, P_GQ, P_GK, P_GV, P_WQ, P_WK, P_WV, P_U, P_END = (
    0, 256, 1024, 1152, 1408, 1536, 1664, 1920, 2048, 2176, 2432)


def _params(sem=None, flags=None):
    return pltpu.CompilerParams(dimension_semantics=sem, vmem_limit_bytes=VMEM_LIMIT, flags=flags)


def _silu(v):
    return v * jax.nn.sigmoid(v)


def _softplus(v):
    return jnp.maximum(v, 0.0) + jnp.log1p(jnp.exp(-jnp.abs(v)))


def _dot(a, b):
    return jnp.dot(a, b, preferred_element_type=F32)


def _dot_nt(a, b, precision=None):
    return lax.dot_general(a, b, (((1,), (1,)), ((), ())), preferred_element_type=F32,
                           precision=precision)


def _dot_tn(a, b):
    return lax.dot_general(a, b, (((0,), (0,)), ((), ())), preferred_element_type=F32)


def _mod_kernel(cc_ref, w_ref, b_ref, o_ref):
    s = _silu(cc_ref[...])
    o_ref[0] = jnp.dot(s, w_ref[0], preferred_element_type=F32, precision=HIGHEST) + b_ref[0]


def _mod_call(cc, w_mod, b_mod):
    n_layers, d, n = w_mod.shape
    tn = 1536
    return pl.pallas_call(
        _mod_kernel,
        grid=(n_layers, n // tn),
        in_specs=[pl.BlockSpec((8, d), lambda l, j: (0, 0)),
                  pl.BlockSpec((1, d, tn), lambda l, j: (l, 0, j)),
                  pl.BlockSpec((1, 1, tn), lambda l, j: (l, 0, j))],
        out_specs=pl.BlockSpec((1, 8, tn), lambda l, j: (l, 0, j)),
        out_shape=jax.ShapeDtypeStruct((n_layers, 8, n), F32),
        compiler_params=_params(("arbitrary", "arbitrary")),
        name="adaln_mod",
    )(cc, w_mod, b_mod.reshape(n_layers, 1, n))


def _rms_mod(x, g, shift, scale):
    y = x * lax.rsqrt(jnp.mean(x * x, axis=-1, keepdims=True) + NORM_EPS) * g
    return y * (1.0 + scale) + shift


def _rope(t, cos, sna, snb):
    return t * cos + pltpu.roll(t, 96, 1) * sna + pltpu.roll(t, 32, 1) * snb


def _head_rms(t, gain, bd):
    t2 = t * t
    hi = t2.astype(BF16)
    lo = (t2 - hi.astype(F32)).astype(BF16)
    ms = _dot(hi, bd) + _dot(lo, bd)
    return t * lax.rsqrt(ms + NORM_EPS) * gain


def _conv3_silu(v, prev, nxt, cw, cb):
    n = v.shape[0]
    row = lax.broadcasted_iota(jnp.int32, (n, 1), 0)
    up = jnp.where(row == 0, prev, pltpu.roll(v, 1, 0))
    dn = jnp.where(row == n - 1, nxt, pltpu.roll(v, n - 1, 0))
    return _silu(cw[0:1] * up + cw[1:2] * v + cw[2:3] * dn + cb)


def _inproj_kernel(x_ref, xp_ref, xn_ref, sh_ref, sc_ref, g_ref, w_ref, cos_ref, sna_ref, snb_ref,
                   qn_ref, kn_ref, bd_ref, cw_ref, cb_ref, z_ref, xs_ref, bt_ref, cm_ref, dt_ref, gq_ref, gk_ref,
                   gv_ref, wq_ref, wk_ref, wv_ref, u_ref, *, tiles_per_seq):
    g, sh, sc = g_ref[...], sh_ref[0], sc_ref[0]
    h = _rms_mod(x_ref[...], g, sh, sc)
    p = _dot(h.astype(BF16), w_ref[...])
    cos, sna, snb = cos_ref[...], sna_ref[...], snb_ref[...]
    bd = bd_ref[...]
    q_scale = HEAD_DIM ** -0.5 * LOG2E
    z_ref[...] = p[:, P_Z:P_XBC]
    halo = jnp.concatenate([xp_ref[...], xn_ref[...]], axis=0)
    ph = _dot(_rms_mod(halo, g, sh, sc).astype(BF16), w_ref[:, P_XBC:P_DT])
    t = pl.program_id(0) % tiles_per_seq
    prev = jnp.where(t > 0, ph[HALO - 1:HALO, :], 0.0)
    nxt = jnp.where(t < tiles_per_seq - 1, ph[HALO:HALO + 1, :], 0.0)
    act = _conv3_silu(p[:, P_XBC:P_DT], prev, nxt, cw_ref[...], cb_ref[...])
    xs_ref[...] = act[:, :D_SSD]
    n_bc = SSD_GROUPS * SSD_STATE
    bt_ref[0] = jnp.transpose(act[:, D_SSD:D_SSD + n_bc]).astype(BF16)
    cm_ref[...] = act[:, D_SSD + n_bc:].astype(BF16)
    dt_ref[...] = p[:, P_DT:P_GQ]
    for j in range(2):
        lo = P_GQ + j * LANES
        q = _rope(_head_rms(p[:, lo:lo + LANES], qn_ref[...], bd), cos, sna, snb) * q_scale
        gq_ref[:, j * LANES:(j + 1) * LANES] = q.astype(BF16)
        lo = P_WQ + j * LANES
        q = _rope(p[:, lo:lo + LANES], cos, sna, snb) * q_scale
        wq_ref[:, j * LANES:(j + 1) * LANES] = q.astype(BF16)
    gk_ref[...] = _rope(_head_rms(p[:, P_GK:P_GV], kn_ref[...], bd), cos, sna, snb).astype(BF16)
    gv_ref[0] = jnp.transpose(p[:, P_GV:P_WQ]).astype(BF16)
    wk_ref[...] = _rope(p[:, P_WK:P_WV], cos, sna, snb).astype(BF16)
    wv_ref[0] = jnp.transpose(p[:, P_WV:P_U]).astype(BF16)
    u_ref[...] = p[:, P_U:P_END]


def _inproj_call(x2, shift, scale, g, w_pad, tabs, qn, kn, bd, cw, cb, *, tm, rows_per_seg, seq):
    r, d = x2.shape
    tps = rows_per_seg // tm
    tpq = seq // tm
    hb = tm // HALO
    last = r // HALO - 1
    row = lambda w: pl.BlockSpec((tm, w), lambda i: (i, 0))
    seg = pl.BlockSpec((1, 1, d), lambda i: (i // tps, 0, 0))
    full = lambda a: pl.BlockSpec(a.shape, lambda i: (0,) * a.ndim)
    tab = pl.BlockSpec((tm, LANES), lambda i: (i % tpq, 0))
    n_bc = SSD_GROUPS * SSD_STATE
    widths = (256, D_SSD, -n_bc, n_bc, LANES, 256, 128, -LANES, 256, 128, -LANES, D_S5)
    dtypes = (F32, F32, BF16, BF16, F32, BF16, BF16, BF16, BF16, BF16, BF16, F32)
    spec = lambda w: row(w) if w > 0 else pl.BlockSpec((1, -w, tm), lambda i: (i // tpq, 0, i % tpq))
    shape = lambda w: (r, w) if w > 0 else (r // seq, -w, seq)
    return pl.pallas_call(
        functools.partial(_inproj_kernel, tiles_per_seq=tpq),
        grid=(r // tm,),
        in_specs=[row(d),
                  pl.BlockSpec((HALO, d), lambda i: (jnp.maximum(i * hb - 1, 0), 0)),
                  pl.BlockSpec((HALO, d), lambda i: (jnp.minimum(i * hb + hb, last), 0)),
                  seg, seg, full(g), full(w_pad), tab, tab, tab, full(qn), full(kn), full(bd),
                  full(cw), full(cb)],
        out_specs=[spec(w) for w in widths],
        out_shape=[jax.ShapeDtypeStruct(shape(w), t) for w, t in zip(widths, dtypes)],
        compiler_params=_params(("arbitrary",)),
        name="inproj",
    )(x2, x2, x2, shift, scale, g, w_pad, *tabs, qn, kn, bd, cw, cb)


def _ssd_kernel(xsf_ref, btf_ref, cmf_ref, xsb_ref, btb_ref, cmb_ref, dtrf_ref, dtrb_ref, h0_ref, alc_ref,
                bic_ref, dsk_ref, yf_ref, yb_ref, hout_ref, h_sc, *, nc):
    c = pl.program_id(1)

    @pl.when(c == 0)
    def _():
        h_sc[...] = h0_ref[0]

    q = SSD_CHUNK
    hpg = SSD_HEADS // SSD_GROUPS
    dirs = ((0, xsf_ref, None, dtrf_ref, yf_ref), (1, xsb_ref, None, dtrb_ref, yb_ref))
    bt_refs, cm_refs = (btf_ref, btb_ref), (cmf_ref, cmb_ref)
    pairs = [(d, g) for d in range(N_DIRS) for g in range(SSD_GROUPS)]
    bmat_t = lambda d, g: bt_refs[d][0, g * SSD_STATE:(g + 1) * SSD_STATE, :]
    cmat = lambda d, g: cm_refs[d][0, :, g * SSD_STATE:(g + 1) * SSD_STATE]
    ri = lax.broadcasted_iota(jnp.int32, (q, q), 0)
    ci = lax.broadcasted_iota(jnp.int32, (q, q), 1)
    mask = (ri >= ci, ri <= ci)
    lane = lax.broadcasted_iota(jnp.int32, (1, LANES), 1)
    first = lane < HEAD_DIM

    h_old = {p: h_sc[p[0], p[1]] for p in pairs}
    cb = {(d, g): _dot(cmat(d, g), bmat_t(d, g)) for d, g in pairs}
    yoff = {(d, g): _dot(cmat(d, g), h_old[d, g].astype(BF16)) for d, g in pairs}

    dt_r, acs_c, acs_r, dt_c, tot = {}, {}, {}, {}, {}
    for d, _, _, dtr_ref, _ in dirs:
        dt_r[d] = _softplus(dtr_ref[0] + bic_ref[...])
    for d in range(N_DIRS):
        dta_r = dt_r[d] * (-jnp.exp(alc_ref[...]))
        maskf = mask[d].astype(F32)
        pad = jnp.zeros((LANES - dta_r.shape[0], q), F32)
        acs_c[d] = _dot_nt(maskf, jnp.concatenate([dta_r, pad], axis=0), precision=HIGHEST)
        acs_r[d] = _dot_nt(dta_r, maskf, precision=HIGHEST)
        dt_c[d] = jnp.transpose(jnp.concatenate([dt_r[d], pad], axis=0))
        tot[d] = acs_c[d][q - 1:q] if d == 0 else acs_c[d][0:1]

    xdt, acs_g, tot_g = {}, {}, {}
    for d, g in pairs:
        e0 = d * SSD_HEADS + g * hpg
        pick = lambda v: jnp.where(first, v[:, e0:e0 + 1], v[:, e0 + 1:e0 + 2])
        acs_g[d, g], tot_g[d, g] = pick(acs_c[d]), pick(tot[d])
        xs_g = dirs[d][1][0, :, g * LANES:(g + 1) * LANES]
        xdt[d, g] = xs_g * pick(dt_c[d])
        xdt_b = xdt[d, g].astype(BF16)
        yd = []
        for a in range(hpg):
            e = e0 + a
            dec = jnp.where(mask[d], jnp.exp(acs_c[d][:, e:e + 1] - acs_r[d][e:e + 1, :]), 0.0)
            yd.append(_dot((cb[d, g] * dec).astype(BF16), xdt_b))
        y = jnp.where(first, yd[0], yd[1]) + yoff[d, g] * jnp.exp(acs_g[d, g])
        if d == 0:
            y = y + dsk_ref[:, g * LANES:(g + 1) * LANES] * xs_g
        dirs[d][4][0, :, g * LANES:(g + 1) * LANES] = y

    for d, g in pairs:
        w = (xdt[d, g] * jnp.exp(tot_g[d, g] - acs_g[d, g])).astype(BF16)
        h_sc[d, g] = h_old[d, g] * jnp.exp(tot_g[d, g]) + _dot(bmat_t(d, g), w)

    @pl.when(c == nc - 1)
    def _():
        hout_ref[0] = h_sc[...]


def _ssd_call(xs, bt, cm, dt_t, h0, alog_c, bias_c, dsk):
    b, s, _ = xs.shape
    q = SSD_CHUNK
    nc = s // q
    fwd = lambda c: c
    bwd = lambda c: nc - 1 - c
    rows = lambda a, f: pl.BlockSpec((1, q, a.shape[2]), lambda i, c: (i, f(c), 0))
    cols = lambda a, f: pl.BlockSpec((1, a.shape[1], q), lambda i, c: (i, 0, f(c)))
    drspec = lambda f: cols(dt_t, f)
    full = lambda a: pl.BlockSpec(a.shape, lambda i, c: (0,) * a.ndim)
    hspec = pl.BlockSpec((1,) + h0.shape[1:], lambda i, c: (i, 0, 0, 0, 0))
    return pl.pallas_call(
        functools.partial(_ssd_kernel, nc=nc),
        grid=(b, nc),
        in_specs=[rows(xs, fwd), cols(bt, fwd), rows(cm, fwd), rows(xs, bwd), cols(bt, bwd), rows(cm, bwd),
                  drspec(fwd), drspec(bwd), hspec, full(alog_c), full(bias_c), full(dsk)],
        out_specs=[rows(xs, fwd), rows(xs, bwd), hspec],
        out_shape=[jax.ShapeDtypeStruct(xs.shape, F32), jax.ShapeDtypeStruct(xs.shape, F32),
                   jax.ShapeDtypeStruct(h0.shape, F32)],
        scratch_shapes=[pltpu.VMEM(h0.shape[1:], F32)],
        compiler_params=_params(("arbitrary", "arbitrary")),
        name="ssd",
    )(xs, bt, cm, xs, bt, cm, dt_t, dt_t, h0, alog_c, bias_c, dsk)


def _q_transposed(q):
    qt = jnp.transpose(q.astype(F32))
    zero = jnp.zeros((HEAD_DIM, q.shape[0]), BF16)
    out = []
    for h in range(Q_HEADS):
        blk = qt[h * HEAD_DIM:(h + 1) * HEAD_DIM].astype(BF16)
        out.append(jnp.concatenate([blk, zero] if h // HPG == 0 else [zero, blk], axis=0))
    return out


def _flash_kernel(sink_ref, q_ref, k_ref, vt_ref, o_ref, qt_sc, m_sc, acc_sc, *, nk, has_sink):
    ki = pl.program_id(2)
    tq = q_ref.shape[0]
    tk = k_ref.shape[1]

    @pl.when(ki == 0)
    def _():
        qts = _q_transposed(q_ref[...])
        for h in range(Q_HEADS):
            qt_sc[h] = qts[h]
            if has_sink:
                m_sc[h] = jnp.full(m_sc.shape[1:], sink_ref[h] * LOG2E, F32)
                acc_sc[h] = jnp.concatenate([jnp.zeros((HEAD_DIM, tq), F32), jnp.ones((ONES_ROWS, tq), F32)], axis=0)
            else:
                m_sc[h] = jnp.full(m_sc.shape[1:], NEG_BIG, F32)
                acc_sc[h] = jnp.zeros(acc_sc.shape[1:], F32)

    ones = jnp.ones((ONES_ROWS, FLASH_SUB_K), BF16)
    chains = [(h, slice(jq * FLASH_SUB_Q, (jq + 1) * FLASH_SUB_Q))
              for h in range(Q_HEADS) for jq in range(tq // FLASH_SUB_Q)]
    m = [m_sc[h, :, cs] for h, cs in chains]
    acc = [acc_sc[h, :, cs] for h, cs in chains]
    units = [(jk, c) for jk in range(tk // FLASH_SUB_K) for c in range(len(chains))]

    def scores(u):
        jk, c = u
        h, cs = chains[c]
        return _dot(k_ref[0, jk * FLASH_SUB_K:(jk + 1) * FLASH_SUB_K, :], qt_sc[h, :, cs])

    def finish(u, s):
        jk, c = u
        g = chains[c][0] // HPG
        vte = jnp.concatenate(
            [vt_ref[0, g * HEAD_DIM:(g + 1) * HEAD_DIM, jk * FLASH_SUB_K:(jk + 1) * FLASH_SUB_K], ones], axis=0)
        m_new = jnp.maximum(m[c], jnp.max(s, axis=0, keepdims=True))
        p = jnp.exp2(s - m_new).astype(BF16)
        acc[c] = jnp.exp2(m[c] - m_new) * acc[c] + _dot(vte, p)
        m[c] = m_new

    pending = [scores(u) for u in units[:FLASH_DEPTH]]
    for i, u in enumerate(units):
        s_cur = pending.pop(0)
        if i + FLASH_DEPTH < len(units):
            pending.append(scores(units[i + FLASH_DEPTH]))
        finish(u, s_cur)
    for c, (h, cs) in enumerate(chains):
        m_sc[h, :, cs] = m[c]
        acc_sc[h, :, cs] = acc[c]

    @pl.when(ki == nk - 1)
    def _():
        outs = []
        for h in range(Q_HEADS):
            acc = acc_sc[h]
            outs.append(acc[:HEAD_DIM] / acc[HEAD_DIM:HEAD_DIM + 1])
        o_ref[...] = jnp.transpose(jnp.concatenate(outs, axis=0)).astype(o_ref.dtype)


def _pick(n, cands):
    for c in cands:
        if n % c == 0:
            return c
    return n


def _flash_call(q2, k, vt, sink):
    b, sk, kw = k.shape
    sq = q2.shape[0] // b
    tq = _pick(sq, (512, 256))
    tk = _pick(sk, (768, 512, 256))
    nq = sq // tq
    nk = sk // tk
    has_sink = sink is not None
    if sink is None:
        sink = jnp.zeros((Q_HEADS,), F32)
    return pl.pallas_call(
        functools.partial(_flash_kernel, nk=nk, has_sink=has_sink),
        grid=(b, nq, nk),
        in_specs=[pl.BlockSpec(memory_space=pltpu.SMEM),
                  pl.BlockSpec((tq, q2.shape[1]), lambda i, qi, ki: (i * nq + qi, 0)),
                  pl.BlockSpec((1, tk, kw), lambda i, qi, ki: (i, ki, 0)),
                  pl.BlockSpec((1, kw, tk), lambda i, qi, ki: (i, 0, ki))],
        out_specs=pl.BlockSpec((tq, q2.shape[1]), lambda i, qi, ki: (i * nq + qi, 0)),
        out_shape=jax.ShapeDtypeStruct(q2.shape, BF16),
        scratch_shapes=[pltpu.VMEM((Q_HEADS, kw, tq), BF16), pltpu.VMEM((Q_HEADS, 1, tq), F32),
                        pltpu.VMEM((Q_HEADS, HEAD_DIM + ONES_ROWS, tq), F32)],
        compiler_params=_params(("arbitrary", "arbitrary", "arbitrary")),
        name="flash_sink" if has_sink else "flash",
    )(sink, q2, k, vt)


def _win_kernel(sink_ref, q_ref, kp_ref, kc_ref, kn_ref, kx_ref, vtp_ref, vtc_ref, vtn_ref, vtx_ref,
                o_ref, *, nq):
    j = pl.program_id(1)
    tq = q_ref.shape[0]
    blk = kp_ref.shape[1]
    rc = lax.broadcasted_iota(jnp.int32, (tq, tq), 0) - lax.broadcasted_iota(jnp.int32, (tq, tq), 1)
    m_cur = (rc <= WINDOW) & (rc >= -WINDOW)
    rp = lax.broadcasted_iota(jnp.int32, (blk, tq), 0) - lax.broadcasted_iota(jnp.int32, (blk, tq), 1)
    m_prev = (rp - blk >= -WINDOW) & (j > 0)
    m_next = (rp + tq <= WINDOW) & (j < nq - 1)
    qts = _q_transposed(q_ref[...])
    raw = [[_dot(kr[0], qt) for kr in (kc_ref, kp_ref, kn_ref, kx_ref)] for qt in qts]
    outs = []
    for h in range(Q_HEADS):
        rows = slice((h // HPG) * HEAD_DIM, (h // HPG + 1) * HEAD_DIM)
        sink = sink_ref[h] * LOG2E
        s_c = jnp.where(m_cur, raw[h][0], NEG_BIG)
        s_p = jnp.where(m_prev, raw[h][1], NEG_BIG)
        s_n = jnp.where(m_next, raw[h][2], NEG_BIG)
        s_x = raw[h][3]
        cmax = lambda v: jnp.max(v, axis=0, keepdims=True)
        m = jnp.maximum(jnp.maximum(cmax(s_c), cmax(s_p)), jnp.maximum(cmax(s_n), cmax(s_x)))
        m = jnp.maximum(m, sink)
        p_c, p_p, p_n, p_x = (jnp.exp2(v - m) for v in (s_c, s_p, s_n, s_x))
        csum = lambda v: jnp.sum(v, axis=0, keepdims=True)
        l = csum(p_c) + csum(p_p) + csum(p_n) + csum(p_x) + jnp.exp2(sink - m)
        o = (_dot(vtc_ref[0, rows, :], p_c.astype(BF16)) + _dot(vtp_ref[0, rows, :], p_p.astype(BF16))
             + _dot(vtn_ref[0, rows, :], p_n.astype(BF16)) + _dot(vtx_ref[0, rows, :], p_x.astype(BF16)))
        outs.append(o / l)
    o_ref[...] = jnp.transpose(jnp.concatenate(outs, axis=0)).astype(o_ref.dtype)


def _win_call(q2, k, vt, kx, vtx, sink):
    b, s, kw = k.shape
    lc = kx.shape[1]
    blk = WINDOW
    tq = _pick(s, (512, 256, 128))
    nq = s // tq
    r = tq // blk
    nblk = s // blk
    prev = lambda t: jnp.maximum(t * r - 1, 0)
    nxt = lambda t: jnp.minimum(t * r + r, nblk - 1)
    return pl.pallas_call(
        functools.partial(_win_kernel, nq=nq),
        grid=(b, nq),
        in_specs=[pl.BlockSpec(memory_space=pltpu.SMEM),
                  pl.BlockSpec((tq, q2.shape[1]), lambda i, t: (i * nq + t, 0)),
                  pl.BlockSpec((1, blk, kw), lambda i, t: (i, prev(t), 0)),
                  pl.BlockSpec((1, tq, kw), lambda i, t: (i, t, 0)),
                  pl.BlockSpec((1, blk, kw), lambda i, t: (i, nxt(t), 0)),
                  pl.BlockSpec((1, lc, kw), lambda i, t: (i, 0, 0)),
                  pl.BlockSpec((1, kw, blk), lambda i, t: (i, 0, prev(t))),
                  pl.BlockSpec((1, kw, tq), lambda i, t: (i, 0, t)),
                  pl.BlockSpec((1, kw, blk), lambda i, t: (i, 0, nxt(t))),
                  pl.BlockSpec((1, kw, lc), lambda i, t: (i, 0, 0))],
        out_specs=pl.BlockSpec((tq, q2.shape[1]), lambda i, t: (i * nq + t, 0)),
        out_shape=jax.ShapeDtypeStruct(q2.shape, BF16),
        compiler_params=_params(("arbitrary", "arbitrary")),
        name="window_attn",
    )(sink, q2, k, k, k, kx, vt, vt, vt, vtx)


def _rep_rows(p, n):
    return jnp.concatenate([jnp.broadcast_to(p[s:s + 1, :], (n, p.shape[1])) for s in range(p.shape[0])], axis=0)


def _ctab(pr, pi, mr, mi):
    big_l, k = pr.shape[0], mr.shape[0]
    er, ei = _rep_rows(pr, k), _rep_rows(pi, k)
    tr, ti = jnp.tile(mr, (big_l, 1)), jnp.tile(mi, (big_l, 1))
    return er * tr - ei * ti, er * ti + ei * tr


def _cmul_rows(cr, ci, s):
    n = cr.shape[1]
    return jnp.concatenate([cr, cr], axis=1) * s + jnp.concatenate([-ci, ci], axis=1) * pltpu.roll(s, n, 1)


def _seg_scan(x, cr, ci, rowm, nper, reverse):
    rows = x.shape[0]
    s, sh = x, 1
    while sh < nper:
        if reverse:
            shifted, valid = pltpu.roll(s, rows - sh, 0), rowm < nper - sh
        else:
            shifted, valid = pltpu.roll(s, sh, 0), rowm >= sh
        s = s + jnp.where(valid, _cmul_rows(cr, ci, shifted), 0.0)
        cr, ci = cr * cr - ci * ci, 2.0 * cr * ci
        sh *= 2
    return s


def _s5_kernel(ul_ref, ux_ref, lre_ref, lim_ref, ls_ref, bre_ref, bim_ref, cre_ref, cim_ref,
               yl_ref, yx_ref, t_sc, *, nb, ncl, ncx):
    big_l = S5_CHUNK
    k = S5_GROUP_CH
    n = S5_STATE
    w = big_l * k
    tau = lax.broadcasted_iota(jnp.int32, (big_l, 1), 0).astype(F32)
    tabs = []
    for d in range(N_DIRS):
        lr = jnp.minimum(lre_ref[0, d], S5_MAX_RE)
        li = lim_ref[0, d]
        dl = jnp.exp(ls_ref[0, d])
        ar, th = lr * dl, li * dl

        def power(t, ar=ar, th=th):
            mag = jnp.exp(t * ar)
            return mag * jnp.cos(t * th), mag * jnp.sin(t * th)

        lbr, lbi = power(1.0)
        den = lr * lr + li * li
        zr = ((lbr - 1.0) * lr + lbi * li) / den
        zi = (lbi * lr - (lbr - 1.0) * li) / den
        br, bi = bre_ref[0, d], bim_ref[0, d]
        bbr, bbi = zr * br - zi * bi, zr * bi + zi * br
        cr, ci = cre_ref[0, d], cim_ref[0, d]
        tabs.append(dict(power=power, bbr=bbr, bbi=bbi, cr=cr, ci=ci, lam_l=power(float(big_l))))
    f, b = tabs
    e_f = _ctab(*f["power"](big_l - 1.0 - tau), f["bbr"], f["bbi"])
    e_b = _ctab(*b["power"](tau), b["bbr"], b["bbi"])
    g_f = _ctab(*f["power"](tau + 1.0), f["cr"], f["ci"])
    g_b = _ctab(*b["power"](big_l - tau), b["cr"], b["ci"])
    a_b = _ctab(*b["power"](big_l - 1.0 - tau), b["cr"], b["ci"])
    c_f = jnp.tile(f["cr"], (big_l, 1)), jnp.tile(f["ci"], (big_l, 1))
    bb2 = lambda t: jnp.concatenate([t["bbr"], t["bbi"]], axis=1)
    neg = lambda t: jnp.concatenate([t[0], -t[1]], axis=1)
    ka = _dot_nt(bb2(b), neg(a_b), precision=HIGHEST)
    kb = _dot_nt(bb2(f), neg(g_f), precision=HIGHEST)
    kc = _dot_nt(bb2(f), neg(c_f), precision=HIGHEST)
    lane = lax.broadcasted_iota(jnp.int32, (1, w), 1)
    ka = ka + jnp.where(lane >= w - k, kc, 0.0)
    kall = jnp.concatenate([ka, kb], axis=1)
    for s in range(big_l):
        off = (big_l - 1 - s) * k
        t_sc[s * k:(s + 1) * k, :] = kall[:, off:off + w].astype(BF16)
    ecat = jnp.concatenate([e_f[0], e_f[1], e_b[0], e_b[1]], axis=1).astype(BF16)
    gcat = jnp.concatenate([g_f[0], -g_f[1], g_b[0], -g_b[1]], axis=1).astype(BF16)
    tmat = t_sc[...]

    def rowmod(nper):
        return jnp.concatenate([lax.broadcasted_iota(jnp.int32, (nper, 1), 0)] * nb, axis=0)

    def by_batch(rows, nper):
        return jnp.concatenate([jnp.broadcast_to(r, (nper, r.shape[1])) for r in rows], axis=0)

    ux = ux_ref[0]
    hx = _dot(ux, ecat)
    rmx = rowmod(ncx)
    rx = nb * ncx
    sxf = _seg_scan(hx[:, :2 * n], *f["lam_l"], rmx, ncx, False)
    sxb = _seg_scan(hx[:, 2 * n:], *b["lam_l"], rmx, ncx, True)
    hin_xf = jnp.where(rmx >= 1, pltpu.roll(sxf, 1, 0), 0.0)
    hin_xb = jnp.where(rmx < ncx - 1, pltpu.roll(sxb, rx - 1, 0), 0.0)
    hin_x = jnp.concatenate([hin_xf, hin_xb], axis=1).astype(BF16)
    yx_ref[0] = _dot(ux, tmat) + _dot_nt(hin_x, gcat)
    hc_f = [sxf[i * ncx + ncx - 1:i * ncx + ncx, :] for i in range(nb)]
    hc_b = [sxb[i * ncx:i * ncx + 1, :] for i in range(nb)]
    ul = ul_ref[0]
    hl = _dot(ul, ecat)
    rml = rowmod(ncl)
    rl = nb * ncl
    xf = jnp.where(rml == 0, by_batch(hc_f, ncl), pltpu.roll(hl[:, :2 * n], 1, 0))
    xb = jnp.where(rml == ncl - 1, by_batch(hc_b, ncl), pltpu.roll(hl[:, 2 * n:], rl - 1, 0))
    hin_f = _seg_scan(xf, *f["lam_l"], rml, ncl, False)
    hin_b = _seg_scan(xb, *b["lam_l"], rml, ncl, True)
    hin = jnp.concatenate([hin_f, hin_b], axis=1).astype(BF16)
    yl_ref[0] = _dot(ul, tmat) + _dot_nt(hin, gcat)


def _s5_call(ul, ux, lre, lim, ls, bre, bim, cre, cim, *, nb):
    g, rl, w = ul.shape
    rx = ux.shape[1]
    grp = lambda a: pl.BlockSpec((1,) + a.shape[1:], lambda i: (i,) + (0,) * (a.ndim - 1))
    return pl.pallas_call(
        functools.partial(_s5_kernel, nb=nb, ncl=rl // nb, ncx=rx // nb),
        grid=(g,),
        in_specs=[grp(a) for a in (ul, ux, lre, lim, ls, bre, bim, cre, cim)],
        out_specs=[grp(ul), grp(ux)],
        out_shape=[jax.ShapeDtypeStruct(ul.shape, F32), jax.ShapeDtypeStruct(ux.shape, F32)],
        scratch_shapes=[pltpu.VMEM((w, w), BF16)],
        compiler_params=_params(("arbitrary",)),
        name="s5",
    )(ul, ux, lre, lim, ls, bre, bim, cre, cim)


def _outproj_kernel(x_ref, gt_ref, yf_ref, yb_ref, z_ref, ga_ref, wa_ref, s5_ref, u_ref, ng_ref,
                    d5_ref, wglu_ref, bglu_ref, wout_ref, o_ref):
    ys = (yf_ref[...] + yb_ref[...]) * _silu(z_ref[...])
    ys = ys * lax.rsqrt(jnp.mean(ys * ys, axis=-1, keepdims=True) + NORM_EPS) * ng_ref[...]
    t = jax.nn.gelu(s5_ref[...] + d5_ref[...] * u_ref[...])
    t = _dot(t.astype(BF16), wglu_ref[...]) + bglu_ref[...]
    s5o = t[:, :D_S5] * jax.nn.sigmoid(t[:, D_S5:])
    cat = jnp.concatenate([ys.astype(BF16), ga_ref[...], wa_ref[...], s5o.astype(BF16)], axis=1)
    o_ref[...] = x_ref[...] + gt_ref[0] * _dot(cat, wout_ref[...])


def _outproj_call(x2, gate, yf, yb, z, ga, wa, s5, u, ng, d5, wglu, bglu, wout, *, tm, rows_per_seg):
    r, d = x2.shape
    tps = rows_per_seg // tm
    row = lambda a: pl.BlockSpec((tm, a.shape[1]), lambda i: (i, 0))
    seg = pl.BlockSpec((1, 1, d), lambda i: (i // tps, 0, 0))
    full = lambda a: pl.BlockSpec(a.shape, lambda i: (0,) * a.ndim)
    return pl.pallas_call(
        _outproj_kernel,
        grid=(r // tm,),
        in_specs=[row(x2), seg] + [row(a) for a in (yf, yb, z, ga, wa, s5, u)]
                 + [full(a) for a in (ng, d5, wglu, bglu, wout)],
        out_specs=row(x2),
        out_shape=jax.ShapeDtypeStruct(x2.shape, F32),
        compiler_params=_params(("arbitrary",)),
        name="outproj",
    )(x2, gate, yf, yb, z, ga, wa, s5, u, ng, d5, wglu, bglu, wout)


def _ffn_kernel(x_ref, xp_ref, xn_ref, sh_ref, sc_ref, gt_ref, g_ref, wgu_ref, cw_ref, cb_ref,
                wd_ref, gfin_ref, o_ref, *, tiles_per_seq, d_ff, final_norm):
    i = pl.program_id(0)
    tm = x_ref.shape[0]
    x = x_ref[...]
    g, sh, sc = g_ref[...], sh_ref[0], sc_ref[0]
    gu = _dot(_rms_mod(x, g, sh, sc).astype(BF16), wgu_ref[...])
    gate = gu[:, :d_ff]
    halo = jnp.concatenate([xp_ref[...], xn_ref[...]], axis=0)
    gh = _dot(_rms_mod(halo, g, sh, sc).astype(BF16), wgu_ref[:, :d_ff])
    t = i % tiles_per_seq
    prev = jnp.where(t > 0, gh[HALO - 1:HALO, :], 0.0)
    nxt = jnp.where(t < tiles_per_seq - 1, gh[HALO:HALO + 1, :], 0.0)
    row = lax.broadcasted_iota(jnp.int32, (tm, 1), 0)
    up = jnp.where(row == 0, prev, pltpu.roll(gate, 1, 0))
    dn = jnp.where(row == tm - 1, nxt, pltpu.roll(gate, tm - 1, 0))
    cw = cw_ref[...]
    conv = cw[0:1] * up + cw[1:2] * gate + cw[2:3] * dn + cb_ref[...]
    act = (_silu(conv) * gu[:, d_ff:]).astype(BF16)
    y = x + gt_ref[0] * _dot(act, wd_ref[...])
    if final_norm:
        y = y * lax.rsqrt(jnp.mean(y * y, axis=-1, keepdims=True) + NORM_EPS) * gfin_ref[...]
    o_ref[...] = y


def _ffn_call(x2, shift, scale, gate, g, wgu, cw, cb, wd, gfin, *, tm, rows_per_seg, seq, final_norm):
    r, d = x2.shape
    d_ff = wd.shape[0]
    tps = rows_per_seg // tm
    hb = tm // HALO
    last = r // HALO - 1
    seg = pl.BlockSpec((1, 1, d), lambda i: (i // tps, 0, 0))
    full = lambda a: pl.BlockSpec(a.shape, lambda i: (0,) * a.ndim)
    once = lambda a: pl.BlockSpec(a.shape, lambda i: (0,) * a.ndim, pipeline_mode=pl.Buffered(1))
    return pl.pallas_call(
        functools.partial(_ffn_kernel, tiles_per_seq=seq // tm, d_ff=d_ff, final_norm=final_norm),
        grid=(r // tm,),
        in_specs=[pl.BlockSpec((tm, d), lambda i: (i, 0)),
                  pl.BlockSpec((HALO, d), lambda i: (jnp.maximum(i * hb - 1, 0), 0)),
                  pl.BlockSpec((HALO, d), lambda i: (jnp.minimum(i * hb + hb, last), 0)),
                  seg, seg, seg, full(g), once(wgu), full(cw), full(cb), once(wd), full(gfin)],
        out_specs=pl.BlockSpec((tm, d), lambda i: (i, 0)),
        out_shape=jax.ShapeDtypeStruct(x2.shape, F32),
        compiler_params=_params(("arbitrary",)),
        name="conv_ffn",
    )(x2, x2, x2, shift, scale, gate, g, wgu, cw, cb, wd, gfin)


def _rope_tables(seq):
    rows = seq // GRID_W
    r = jnp.broadcast_to(jnp.arange(rows, dtype=F32)[:, None], (rows, GRID_W)).reshape(-1)
    c = jnp.broadcast_to(jnp.arange(GRID_W, dtype=F32)[None, :], (rows, GRID_W)).reshape(-1)
    n_freq = HEAD_DIM // 4
    inv = ROPE_THETA ** (-jnp.arange(n_freq, dtype=F32) / n_freq)
    ang = jnp.concatenate([r[:, None] * inv, c[:, None] * inv], axis=-1)
    cos, sin, zero = jnp.cos(ang), jnp.sin(ang), jnp.zeros_like(ang)
    reps = LANES // HEAD_DIM
    return (jnp.tile(jnp.concatenate([cos, cos], axis=-1), (1, reps)),
            jnp.tile(jnp.concatenate([-sin, zero], axis=-1), (1, reps)),
            jnp.tile(jnp.concatenate([zero, sin], axis=-1), (1, reps)))


def _pad_w_in(w):
    offs = [0]
    for s in IN_SIZES:
        offs.append(offs[-1] + s)
    parts = [w[:, offs[j]:offs[j + 1]] for j in range(len(IN_SIZES))]
    parts[2] = jnp.pad(parts[2], ((0, 0), (0, LANES - IN_SIZES[2])))
    return jnp.concatenate(parts, axis=1).astype(BF16)


def _s5_rows(u2, b):
    s = u2.shape[0] // b
    nc = s // S5_CHUNK
    u = u2.reshape(b, nc, S5_CHUNK, S5_GROUPS, S5_GROUP_CH).transpose(3, 0, 1, 2, 4)
    return u.reshape(S5_GROUPS, b * nc, S5_CHUNK * S5_GROUP_CH).astype(BF16)


def _s5_back(y, b):
    g, r, _ = y.shape
    nc = r // b
    y = y.reshape(g, b, nc, S5_CHUNK, S5_GROUP_CH).transpose(1, 2, 3, 0, 4)
    return y.reshape(b * nc * S5_CHUNK, g * S5_GROUP_CH)


def kernel(x, c, ctx, c_ctx, w_mod, b_mod, g_mix, w_in, ssd_conv_w, ssd_conv_b, ssd_a_log, ssd_dt_bias, ssd_d, ssd_norm_g, ga_q_norm, ga_k_norm, wa_sink, s5_lambda_re, s5_lambda_im, s5_log_step, s5_b_re, s5_b_im, s5_c_re, s5_c_im, s5_d, s5_w_glu, s5_b_glu, w_out, g_ffn, w_gate, w_up, ffn_conv_w, ffn_conv_b, w_down, g_final):
    b, s, d = x.shape
    lc = ctx.shape[1]
    depth = w_mod.shape[0]
    hpg = 2
    tm = 512

    cc = jnp.zeros((8, d), F32).at[:b].set(c).at[b].set(c_ctx)
    mod = _mod_call(cc, w_mod, b_mod).reshape(depth, 8, 6, d)

    tabs_l = _rope_tables(s)
    one, zero = jnp.ones((lc, LANES), F32), jnp.zeros((lc, LANES), F32)
    tabs_c = (one, zero, zero)
    bd = jnp.kron(jnp.eye(LANES // HEAD_DIM, dtype=F32),
                  jnp.full((HEAD_DIM, HEAD_DIM), 1.0 / HEAD_DIM, F32)).astype(BF16)
    gfin = g_final.reshape(1, d)

    x2 = x.reshape(b * s, d)
    xc2 = ctx.reshape(b * lc, d)
    for i in range(depth):
        need_ctx = i < depth - 1
        ml = lambda j: mod[i, :b, j].reshape(b, 1, d)
        mc = lambda j: mod[i, b, j].reshape(1, 1, d)
        g_i = g_mix[i].reshape(1, d)
        w_pad = _pad_w_in(w_in[i])
        qn = jnp.tile(ga_q_norm[i], LANES // HEAD_DIM).reshape(1, LANES)
        kn = jnp.tile(ga_k_norm[i], LANES // HEAD_DIM).reshape(1, LANES)
        conv_w = ssd_conv_w[i]
        conv_b = ssd_conv_b[i].reshape(1, -1)
        pl_ = _inproj_call(x2, ml(0), ml(1), g_i, w_pad, tabs_l, qn, kn, bd, conv_w, conv_b,
                           tm=tm, rows_per_seg=s, seq=s)
        pc_ = _inproj_call(xc2, mc(0), mc(1), g_i, w_pad, tabs_c, qn, kn, bd, conv_w, conv_b,
                           tm=lc, rows_per_seg=b * lc, seq=lc)
        zl, xsl, btl, cml, dtl, gql, gkl, gvl, wql, wkl, wvl, ul = pl_
        zc, xsc, btc, cmc, dtc, gqc, gkc, gvc, wqc, wkc, wvc, uc = pc_

        alog_c = ssd_a_log[i].reshape(-1, 1)
        bias_c = ssd_dt_bias[i].reshape(-1, 1)
        dsk = jnp.repeat(ssd_d[i], HEAD_DIM).reshape(1, D_SSD)
        n_dt = N_DIRS * SSD_HEADS

        def ssd(xs2, bt, cm2, dt2, h0, n):
            dt_t = dt2.reshape(b, n, LANES)[:, :, :n_dt].transpose(0, 2, 1)
            return _ssd_call(xs2.reshape(b, n, -1), bt, cm2.reshape(b, n, -1), dt_t, h0, alog_c, bias_c, dsk)

        h0 = jnp.zeros((b, N_DIRS, SSD_GROUPS, SSD_STATE, hpg * HEAD_DIM), F32)
        yfc, ybc, hc = ssd(xsc, btc, cmc, dtc, h0, lc)
        yfl, ybl, _ = ssd(xsl, btl, cml, dtl, hc, s)

        kx_ga = gkc.reshape(b, lc, -1)
        k_all = jnp.concatenate([kx_ga, gkl.reshape(b, s, -1)], axis=1)
        v_all = jnp.concatenate([gvc, gvl], axis=2)
        y_ga = _flash_call(gql, k_all, v_all, None)

        kx_wa = wkc.reshape(b, lc, -1)
        sink = wa_sink[i].astype(F32)
        y_wa = _win_call(wql, wkl.reshape(b, s, -1), wvl, kx_wa, wvc, sink)

        gm = lambda a: jnp.moveaxis(a, 1, 0)
        lre = gm(s5_lambda_re[i])[:, :, None, :]
        lim = gm(s5_lambda_im[i])[:, :, None, :]
        ls = gm(s5_log_step[i])[:, :, None, None]
        bre = gm(s5_b_re[i]).transpose(0, 1, 3, 2)
        bim = gm(s5_b_im[i]).transpose(0, 1, 3, 2)
        cre, cim = gm(s5_c_re[i]), gm(s5_c_im[i])
        y5l, y5c = _s5_call(_s5_rows(ul, b), _s5_rows(uc, b), lre, lim, ls, bre, bim, cre, cim, nb=b)
        y5l, y5c = _s5_back(y5l, b), _s5_back(y5c, b)

        ng = ssd_norm_g[i].reshape(1, D_SSD)
        d5 = s5_d[i].reshape(1, D_S5)
        wglu = s5_w_glu[i].astype(BF16)
        bglu = s5_b_glu[i].reshape(1, -1)
        wout = w_out[i].astype(BF16)
        x2 = _outproj_call(x2, ml(2), yfl.reshape(b * s, -1), ybl.reshape(b * s, -1), zl, y_ga, y_wa,
                           y5l, ul, ng, d5, wglu, bglu, wout, tm=tm, rows_per_seg=s)

        gf_i = g_ffn[i].reshape(1, d)
        wgu = jnp.concatenate([w_gate[i], w_up[i]], axis=1).astype(BF16)
        cw = ffn_conv_w[i]
        cb = ffn_conv_b[i].reshape(1, -1)
        wd = w_down[i].astype(BF16)
        x2 = _ffn_call(x2, ml(3), ml(4), ml(5), gf_i, wgu, cw, cb, wd, gfin, tm=256, rows_per_seg=s,
                       seq=s, final_norm=not need_ctx)

        if need_ctx:
            yc_ga = _flash_call(gqc, kx_ga, gvc, None)
            yc_wa = _flash_call(wqc, kx_wa, wvc, sink)
            xc2 = _outproj_call(xc2, mc(2), yfc.reshape(b * lc, -1), ybc.reshape(b * lc, -1), zc, yc_ga,
                                yc_wa, y5c, uc, ng, d5, wglu, bglu, wout, tm=lc, rows_per_seg=b * lc)
            xc2 = _ffn_call(xc2, mc(3), mc(4), mc(5), gf_i, wgu, cw, cb, wd, gfin, tm=lc,
                            rows_per_seg=b * lc, seq=lc, final_norm=False)
    return x2.reshape(b, s, d)
```

```python
import functools

import jax
import jax.numpy as jnp
from jax import lax
from jax.experimental import pallas as pl
from jax.experimental.pallas import tpu as pltpu

F32 = jnp.float32
BF16 = jnp.bfloat16
HIGHEST = lax.Precision.HIGHEST

HEAD_DIM = 64
GRID_W = 64
ROPE_THETA = 10000.0
NORM_EPS = 1e-6
WINDOW = 128
N_DIRS = 2
SSD_HEADS = 4
SSD_GROUPS = 2
SSD_STATE = 128
SSD_CHUNK = 128
D_SSD = SSD_HEADS * HEAD_DIM
SSD_XBC = D_SSD + 2 * SSD_GROUPS * SSD_STATE
S5_GROUPS = 16
S5_GROUP_CH = 16
S5_STATE = 64
S5_MAX_RE = -1e-4
S5_CHUNK = 32
D_S5 = S5_GROUPS * S5_GROUP_CH
LANES = 128
HALO = 8
NEG_BIG = -1e30
LOG2E = 1.4426950408889634
Q_HEADS = 4
KV_HEADS = 2
HPG = Q_HEADS // KV_HEADS
ONES_ROWS = 16
FLASH_SUB_K = 256
FLASH_SUB_Q = 256
FLASH_DEPTH = 6
VMEM_LIMIT = 52 * 1024 * 1024

IN_SIZES = (D_SSD, SSD_XBC, N_DIRS * SSD_HEADS, 256, 128, 128, 256, 128, 128, D_S5)
P_Z, P_XBC, P_DT, P_GQ, P_GK, P_GV, P_WQ, P_WK, P_WV, P_U, P_END = (
    0, 256, 1024, 1152, 1408, 1536, 1664, 1920, 2048, 2176, 2432)


def _params(sem=None, flags=None):
    return pltpu.CompilerParams(dimension_semantics=sem, vmem_limit_bytes=VMEM_LIMIT, flags=flags)


def _silu(v):
    return v * jax.nn.sigmoid(v)


def _softplus(v):
    return jnp.maximum(v, 0.0) + jnp.log1p(jnp.exp(-jnp.abs(v)))


def _dot(a, b):
    return jnp.dot(a, b, preferred_element_type=F32)


def _dot_nt(a, b, precision=None):
    return lax.dot_general(a, b, (((1,), (1,)), ((), ())), preferred_element_type=F32,
                           precision=precision)


def _dot_tn(a, b):
    return lax.dot_general(a, b, (((0,), (0,)), ((), ())), preferred_element_type=F32)


def _mod_kernel(cc_ref, w_ref, b_ref, o_ref):
    s = _silu(cc_ref[...])
    o_ref[0] = jnp.dot(s, w_ref[0], preferred_element_type=F32, precision=HIGHEST) + b_ref[0]


def _mod_call(cc, w_mod, b_mod):
    n_layers, d, n = w_mod.shape
    tn = 1536
    return pl.pallas_call(
        _mod_kernel,
        grid=(n_layers, n // tn),
        in_specs=[pl.BlockSpec((8, d), lambda l, j: (0, 0)),
                  pl.BlockSpec((1, d, tn), lambda l, j: (l, 0, j)),
                  pl.BlockSpec((1, 1, tn), lambda l, j: (l, 0, j))],
        out_specs=pl.BlockSpec((1, 8, tn), lambda l, j: (l, 0, j)),
        out_shape=jax.ShapeDtypeStruct((n_layers, 8, n), F32),
        compiler_params=_params(("arbitrary", "arbitrary")),
        name="adaln_mod",
    )(cc, w_mod, b_mod.reshape(n_layers, 1, n))


def _rms_mod(x, g, shift, scale):
    y = x * lax.rsqrt(jnp.mean(x * x, axis=-1, keepdims=True) + NORM_EPS) * g
    return y * (1.0 + scale) + shift


def _rope(t, cos, sna, snb):
    return t * cos + pltpu.roll(t, 96, 1) * sna + pltpu.roll(t, 32, 1) * snb


def _head_rms(t, gain, bd):
    t2 = t * t
    hi = t2.astype(BF16)
    lo = (t2 - hi.astype(F32)).astype(BF16)
    ms = _dot(hi, bd) + _dot(lo, bd)
    return t * lax.rsqrt(ms + NORM_EPS) * gain


def _conv3_silu(v, prev, nxt, cw, cb):
    n = v.shape[0]
    row = lax.broadcasted_iota(jnp.int32, (n, 1), 0)
    up = jnp.where(row == 0, prev, pltpu.roll(v, 1, 0))
    dn = jnp.where(row == n - 1, nxt, pltpu.roll(v, n - 1, 0))
    return _silu(cw[0:1] * up + cw[1:2] * v + cw[2:3] * dn + cb)


def _inproj_kernel(x_ref, xp_ref, xn_ref, sh_ref, sc_ref, g_ref, w_ref, cos_ref, sna_ref, snb_ref,
                   qn_ref, kn_ref, bd_ref, cw_ref, cb_ref, z_ref, xs_ref, bt_ref, cm_ref, dt_ref, gq_ref, gk_ref,
                   gv_ref, wq_ref, wk_ref, wv_ref, u_ref, *, tiles_per_seq):
    g, sh, sc = g_ref[...], sh_ref[0], sc_ref[0]
    hb = _rms_mod(x_ref[...], g, sh, sc).astype(BF16)
    halo = jnp.concatenate([xp_ref[...], xn_ref[...]], axis=0)
    p = _dot(hb, w_ref[:, P_Z:P_GQ])
    ph = _dot(_rms_mod(halo, g, sh, sc).astype(BF16), w_ref[:, P_XBC:P_DT])
    pg = _dot(hb, w_ref[:, P_GQ:P_WQ])
    pw = _dot(hb, w_ref[:, P_WQ:P_END])
    cos, sna, snb = cos_ref[...], sna_ref[...], snb_ref[...]
    bd = bd_ref[...]
    q_scale = HEAD_DIM ** -0.5 * LOG2E
    z_ref[...] = p[:, P_Z:P_XBC]
    t = pl.program_id(0) % tiles_per_seq
    prev = jnp.where(t > 0, ph[HALO - 1:HALO, :], 0.0)
    nxt = jnp.where(t < tiles_per_seq - 1, ph[HALO:HALO + 1, :], 0.0)
    act = _conv3_silu(p[:, P_XBC:P_DT], prev, nxt, cw_ref[...], cb_ref[...])
    xs_ref[...] = act[:, :D_SSD]
    n_bc = SSD_GROUPS * SSD_STATE
    bt_ref[0] = jnp.transpose(act[:, D_SSD:D_SSD + n_bc]).astype(BF16)
    cm_ref[...] = act[:, D_SSD + n_bc:].astype(BF16)
    dt_ref[...] = p[:, P_DT:P_GQ]
    for j in range(2):
        q = _rope(_head_rms(pg[:, j * LANES:(j + 1) * LANES], qn_ref[...], bd), cos, sna, snb) * q_scale
        gq_ref[:, j * LANES:(j + 1) * LANES] = q.astype(BF16)
    gk_ref[...] = _rope(_head_rms(pg[:, P_GK - P_GQ:P_GV - P_GQ], kn_ref[...], bd), cos, sna, snb).astype(BF16)
    gv_ref[0] = jnp.transpose(pg[:, P_GV - P_GQ:]).astype(BF16)
    for j in range(2):
        q = _rope(pw[:, j * LANES:(j + 1) * LANES], cos, sna, snb) * q_scale
        wq_ref[:, j * LANES:(j + 1) * LANES] = q.astype(BF16)
    wk_ref[...] = _rope(pw[:, P_WK - P_WQ:P_WV - P_WQ], cos, sna, snb).astype(BF16)
    wv_ref[0] = jnp.transpose(pw[:, P_WV - P_WQ:P_U - P_WQ]).astype(BF16)
    u_ref[...] = pw[:, P_U - P_WQ:]


def _inproj_call(x2, shift, scale, g, w_pad, tabs, qn, kn, bd, cw, cb, *, tm, rows_per_seg, seq):
    r, d = x2.shape
    tps = rows_per_seg // tm
    tpq = seq // tm
    hb = tm // HALO
    last = r // HALO - 1
    row = lambda w: pl.BlockSpec((tm, w), lambda i: (i, 0))
    seg = pl.BlockSpec((1, 1, d), lambda i: (i // tps, 0, 0))
    full = lambda a: pl.BlockSpec(a.shape, lambda i: (0,) * a.ndim)
    tab = pl.BlockSpec((tm, LANES), lambda i: (i % tpq, 0))
    n_bc = SSD_GROUPS * SSD_STATE
    widths = (256, D_SSD, -n_bc, n_bc, LANES, 256, 128, -LANES, 256, 128, -LANES, D_S5)
    dtypes = (F32, F32, BF16, BF16, F32, BF16, BF16, BF16, BF16, BF16, BF16, F32)
    spec = lambda w: row(w) if w > 0 else pl.BlockSpec((1, -w, tm), lambda i: (i // tpq, 0, i % tpq))
    shape = lambda w: (r, w) if w > 0 else (r // seq, -w, seq)
    return pl.pallas_call(
        functools.partial(_inproj_kernel, tiles_per_seq=tpq),
        grid=(r // tm,),
        in_specs=[row(d),
                  pl.BlockSpec((HALO, d), lambda i: (jnp.maximum(i * hb - 1, 0), 0)),
                  pl.BlockSpec((HALO, d), lambda i: (jnp.minimum(i * hb + hb, last), 0)),
                  seg, seg, full(g), full(w_pad), tab, tab, tab, full(qn), full(kn), full(bd),
                  full(cw), full(cb)],
        out_specs=[spec(w) for w in widths],
        out_shape=[jax.ShapeDtypeStruct(shape(w), t) for w, t in zip(widths, dtypes)],
        compiler_params=_params(("arbitrary",)),
        name="inproj",
    )(x2, x2, x2, shift, scale, g, w_pad, *tabs, qn, kn, bd, cw, cb)


def _ssd_kernel(xsf_ref, btf_ref, cmf_ref, xsb_ref, btb_ref, cmb_ref, dtrf_ref, dtrb_ref, h0_ref, alc_ref,
                bic_ref, dsk_ref, yf_ref, yb_ref, hout_ref, h_sc, *, nc):
    c = pl.program_id(1)

    @pl.when(c == 0)
    def _():
        h_sc[...] = h0_ref[0]

    q = SSD_CHUNK
    hpg = SSD_HEADS // SSD_GROUPS
    dirs = ((0, xsf_ref, None, dtrf_ref, yf_ref), (1, xsb_ref, None, dtrb_ref, yb_ref))
    bt_refs, cm_refs = (btf_ref, btb_ref), (cmf_ref, cmb_ref)
    pairs = [(d, g) for d in range(N_DIRS) for g in range(SSD_GROUPS)]
    bmat_t = lambda d, g: bt_refs[d][0, g * SSD_STATE:(g + 1) * SSD_STATE, :]
    cmat = lambda d, g: cm_refs[d][0, :, g * SSD_STATE:(g + 1) * SSD_STATE]
    ri = lax.broadcasted_iota(jnp.int32, (q, q), 0)
    ci = lax.broadcasted_iota(jnp.int32, (q, q), 1)
    mask = (ri >= ci, ri <= ci)
    lane = lax.broadcasted_iota(jnp.int32, (1, LANES), 1)
    first = lane < HEAD_DIM

    h_old = {p: h_sc[p[0], p[1]] for p in pairs}
    cb = {(d, g): _dot(cmat(d, g), bmat_t(d, g)) for d, g in pairs}
    yoff = {(d, g): _dot(cmat(d, g), h_old[d, g].astype(BF16)) for d, g in pairs}

    dt_r, acs_c, acs_r, dt_c, tot = {}, {}, {}, {}, {}
    for d, _, _, dtr_ref, _ in dirs:
        dt_r[d] = _softplus(dtr_ref[0] + bic_ref[...])
    for d in range(N_DIRS):
        dta_r = dt_r[d] * (-jnp.exp(alc_ref[...]))
        maskf = mask[d].astype(F32)
        pad = jnp.zeros((LANES - dta_r.shape[0], q), F32)
        acs_c[d] = _dot_nt(maskf, jnp.concatenate([dta_r, pad], axis=0), precision=HIGHEST)
        acs_r[d] = _dot_nt(dta_r, maskf, precision=HIGHEST)
        dt_c[d] = jnp.transpose(jnp.concatenate([dt_r[d], pad], axis=0))
        tot[d] = acs_c[d][q - 1:q] if d == 0 else acs_c[d][0:1]

    xdt, acs_g, tot_g = {}, {}, {}
    for d, g in pairs:
        e0 = d * SSD_HEADS + g * hpg
        pick = lambda v: jnp.where(first, v[:, e0:e0 + 1], v[:, e0 + 1:e0 + 2])
        acs_g[d, g], tot_g[d, g] = pick(acs_c[d]), pick(tot[d])
        xs_g = dirs[d][1][0, :, g * LANES:(g + 1) * LANES]
        xdt[d, g] = xs_g * pick(dt_c[d])
        xdt_b = xdt[d, g].astype(BF16)
        yd = []
        for a in range(hpg):
            e = e0 + a
            dec = jnp.where(mask[d], jnp.exp(acs_c[d][:, e:e + 1] - acs_r[d][e:e + 1, :]), 0.0)
            yd.append(_dot((cb[d, g] * dec).astype(BF16), xdt_b))
        y = jnp.where(first, yd[0], yd[1]) + yoff[d, g] * jnp.exp(acs_g[d, g])
        if d == 0:
            y = y + dsk_ref[:, g * LANES:(g + 1) * LANES] * xs_g
        dirs[d][4][0, :, g * LANES:(g + 1) * LANES] = y

    for d, g in pairs:
        w = (xdt[d, g] * jnp.exp(tot_g[d, g] - acs_g[d, g])).astype(BF16)
        h_sc[d, g] = h_old[d, g] * jnp.exp(tot_g[d, g]) + _dot(bmat_t(d, g), w)

    @pl.when(c == nc - 1)
    def _():
        hout_ref[0] = h_sc[...]


def _ssd_call(xs, bt, cm, dt_t, h0, alog_c, bias_c, dsk):
    b, s, _ = xs.shape
    q = SSD_CHUNK
    nc = s // q
    fwd = lambda c: c
    bwd = lambda c: nc - 1 - c
    rows = lambda a, f: pl.BlockSpec((1, q, a.shape[2]), lambda i, c: (i, f(c), 0))
    cols = lambda a, f: pl.BlockSpec((1, a.shape[1], q), lambda i, c: (i, 0, f(c)))
    drspec = lambda f: cols(dt_t, f)
    full = lambda a: pl.BlockSpec(a.shape, lambda i, c: (0,) * a.ndim)
    hspec = pl.BlockSpec((1,) + h0.shape[1:], lambda i, c: (i, 0, 0, 0, 0))
    return pl.pallas_call(
        functools.partial(_ssd_kernel, nc=nc),
        grid=(b, nc),
        in_specs=[rows(xs, fwd), cols(bt, fwd), rows(cm, fwd), rows(xs, bwd), cols(bt, bwd), rows(cm, bwd),
                  drspec(fwd), drspec(bwd), hspec, full(alog_c), full(bias_c), full(dsk)],
        out_specs=[rows(xs, fwd), rows(xs, bwd), hspec],
        out_shape=[jax.ShapeDtypeStruct(xs.shape, F32), jax.ShapeDtypeStruct(xs.shape, F32),
                   jax.ShapeDtypeStruct(h0.shape, F32)],
        scratch_shapes=[pltpu.VMEM(h0.shape[1:], F32)],
        compiler_params=_params(("arbitrary", "arbitrary")),
        name="ssd",
    )(xs, bt, cm, xs, bt, cm, dt_t, dt_t, h0, alog_c, bias_c, dsk)


def _q_transposed(q):
    qt = jnp.transpose(q.astype(F32))
    zero = jnp.zeros((HEAD_DIM, q.shape[0]), BF16)
    out = []
    for h in range(Q_HEADS):
        blk = qt[h * HEAD_DIM:(h + 1) * HEAD_DIM].astype(BF16)
        out.append(jnp.concatenate([blk, zero] if h // HPG == 0 else [zero, blk], axis=0))
    return out


def _flash_kernel(sink_ref, q_ref, k_ref, vt_ref, o_ref, qt_sc, m_sc, acc_sc, *, nk, has_sink):
    ki = pl.program_id(2)
    tq = q_ref.shape[0]
    tk = k_ref.shape[1]

    @pl.when(ki == 0)
    def _():
        qts = _q_transposed(q_ref[...])
        for h in range(Q_HEADS):
            qt_sc[h] = qts[h]
            if has_sink:
                m_sc[h] = jnp.full(m_sc.shape[1:], sink_ref[h] * LOG2E, F32)
                acc_sc[h] = jnp.concatenate([jnp.zeros((HEAD_DIM, tq), F32), jnp.ones((ONES_ROWS, tq), F32)], axis=0)
            else:
                m_sc[h] = jnp.full(m_sc.shape[1:], NEG_BIG, F32)
                acc_sc[h] = jnp.zeros(acc_sc.shape[1:], F32)

    ones = jnp.ones((ONES_ROWS, FLASH_SUB_K), BF16)
    chains = [(h, slice(jq * FLASH_SUB_Q, (jq + 1) * FLASH_SUB_Q))
              for h in range(Q_HEADS) for jq in range(tq // FLASH_SUB_Q)]
    m = [m_sc[h, :, cs] for h, cs in chains]
    acc = [acc_sc[h, :, cs] for h, cs in chains]
    units = [(jk, c) for jk in range(tk // FLASH_SUB_K) for c in range(len(chains))]

    def scores(u):
        jk, c = u
        h, cs = chains[c]
        return _dot(k_ref[0, jk * FLASH_SUB_K:(jk + 1) * FLASH_SUB_K, :], qt_sc[h, :, cs])

    def finish(u, s):
        jk, c = u
        g = chains[c][0] // HPG
        vte = jnp.concatenate(
            [vt_ref[0, g * HEAD_DIM:(g + 1) * HEAD_DIM, jk * FLASH_SUB_K:(jk + 1) * FLASH_SUB_K], ones], axis=0)
        m_new = jnp.maximum(m[c], jnp.max(s, axis=0, keepdims=True))
        p = jnp.exp2(s - m_new).astype(BF16)
        acc[c] = jnp.exp2(m[c] - m_new) * acc[c] + _dot(vte, p)
        m[c] = m_new

    pending = [scores(u) for u in units[:FLASH_DEPTH]]
    for i, u in enumerate(units):
        s_cur = pending.pop(0)
        if i + FLASH_DEPTH < len(units):
            pending.append(scores(units[i + FLASH_DEPTH]))
        finish(u, s_cur)
    for c, (h, cs) in enumerate(chains):
        m_sc[h, :, cs] = m[c]
        acc_sc[h, :, cs] = acc[c]

    @pl.when(ki == nk - 1)
    def _():
        outs = []
        for h in range(Q_HEADS):
            acc = acc_sc[h]
            outs.append(acc[:HEAD_DIM] / acc[HEAD_DIM:HEAD_DIM + 1])
        o_ref[...] = jnp.transpose(jnp.concatenate(outs, axis=0)).astype(o_ref.dtype)


def _pick(n, cands):
    for c in cands:
        if n % c == 0:
            return c
    return n


def _flash_call(q2, k, vt, sink):
    b, sk, kw = k.shape
    sq = q2.shape[0] // b
    tq = _pick(sq, (512, 256))
    tk = _pick(sk, (2816, 768, 512, 256))
    nq = sq // tq
    nk = sk // tk
    has_sink = sink is not None
    if sink is None:
        sink = jnp.zeros((Q_HEADS,), F32)
    return pl.pallas_call(
        functools.partial(_flash_kernel, nk=nk, has_sink=has_sink),
        grid=(b, nq, nk),
        in_specs=[pl.BlockSpec(memory_space=pltpu.SMEM),
                  pl.BlockSpec((tq, q2.shape[1]), lambda i, qi, ki: (i * nq + qi, 0)),
                  pl.BlockSpec((1, tk, kw), lambda i, qi, ki: (i, ki, 0)),
                  pl.BlockSpec((1, kw, tk), lambda i, qi, ki: (i, 0, ki))],
        out_specs=pl.BlockSpec((tq, q2.shape[1]), lambda i, qi, ki: (i * nq + qi, 0)),
        out_shape=jax.ShapeDtypeStruct(q2.shape, BF16),
        scratch_shapes=[pltpu.VMEM((Q_HEADS, kw, tq), BF16), pltpu.VMEM((Q_HEADS, 1, tq), F32),
                        pltpu.VMEM((Q_HEADS, HEAD_DIM + ONES_ROWS, tq), F32)],
        compiler_params=_params(("arbitrary", "arbitrary", "arbitrary")),
        name="flash_sink" if has_sink else "flash",
    )(sink, q2, k, vt)


def _win_kernel(sink_ref, q_ref, kp_ref, kc_ref, kn_ref, kx_ref, vtp_ref, vtc_ref, vtn_ref, vtx_ref,
                o_ref, *, nq):
    j = pl.program_id(1)
    tq = q_ref.shape[0]
    blk = kp_ref.shape[1]
    rc = lax.broadcasted_iota(jnp.int32, (tq, tq), 0) - lax.broadcasted_iota(jnp.int32, (tq, tq), 1)
    m_cur = (rc <= WINDOW) & (rc >= -WINDOW)
    rp = lax.broadcasted_iota(jnp.int32, (blk, tq), 0) - lax.broadcasted_iota(jnp.int32, (blk, tq), 1)
    m_prev = (rp - blk >= -WINDOW) & (j > 0)
    m_next = (rp + tq <= WINDOW) & (j < nq - 1)
    qts = _q_transposed(q_ref[...])
    raw = [[_dot(kr[0], qt) for kr in (kc_ref, kp_ref, kn_ref, kx_ref)] for qt in qts]
    outs = []
    for h in range(Q_HEADS):
        rows = slice((h // HPG) * HEAD_DIM, (h // HPG + 1) * HEAD_DIM)
        sink = sink_ref[h] * LOG2E
        s_c = jnp.where(m_cur, raw[h][0], NEG_BIG)
        s_p = jnp.where(m_prev, raw[h][1], NEG_BIG)
        s_n = jnp.where(m_next, raw[h][2], NEG_BIG)
        s_x = raw[h][3]
        cmax = lambda v: jnp.max(v, axis=0, keepdims=True)
        m = jnp.maximum(jnp.maximum(cmax(s_c), cmax(s_p)), jnp.maximum(cmax(s_n), cmax(s_x)))
        m = jnp.maximum(m, sink)
        p_c, p_p, p_n, p_x = (jnp.exp2(v - m) for v in (s_c, s_p, s_n, s_x))
        csum = lambda v: jnp.sum(v, axis=0, keepdims=True)
        l = csum(p_c) + csum(p_p) + csum(p_n) + csum(p_x) + jnp.exp2(sink - m)
        o = (_dot(vtc_ref[0, rows, :], p_c.astype(BF16)) + _dot(vtp_ref[0, rows, :], p_p.astype(BF16))
             + _dot(vtn_ref[0, rows, :], p_n.astype(BF16)) + _dot(vtx_ref[0, rows, :], p_x.astype(BF16)))
        outs.append(o / l)
    o_ref[...] = jnp.transpose(jnp.concatenate(outs, axis=0)).astype(o_ref.dtype)


def _win_call(q2, k, vt, kx, vtx, sink):
    b, s, kw = k.shape
    lc = kx.shape[1]
    blk = WINDOW
    tq = _pick(s, (512, 256, 128))
    nq = s // tq
    r = tq // blk
    nblk = s // blk
    prev = lambda t: jnp.maximum(t * r - 1, 0)
    nxt = lambda t: jnp.minimum(t * r + r, nblk - 1)
    return pl.pallas_call(
        functools.partial(_win_kernel, nq=nq),
        grid=(b, nq),
        in_specs=[pl.BlockSpec(memory_space=pltpu.SMEM),
                  pl.BlockSpec((tq, q2.shape[1]), lambda i, t: (i * nq + t, 0)),
                  pl.BlockSpec((1, blk, kw), lambda i, t: (i, prev(t), 0)),
                  pl.BlockSpec((1, tq, kw), lambda i, t: (i, t, 0)),
                  pl.BlockSpec((1, blk, kw), lambda i, t: (i, nxt(t), 0)),
                  pl.BlockSpec((1, lc, kw), lambda i, t: (i, 0, 0)),
                  pl.BlockSpec((1, kw, blk), lambda i, t: (i, 0, prev(t))),
                  pl.BlockSpec((1, kw, tq), lambda i, t: (i, 0, t)),
                  pl.BlockSpec((1, kw, blk), lambda i, t: (i, 0, nxt(t))),
                  pl.BlockSpec((1, kw, lc), lambda i, t: (i, 0, 0))],
        out_specs=pl.BlockSpec((tq, q2.shape[1]), lambda i, t: (i * nq + t, 0)),
        out_shape=jax.ShapeDtypeStruct(q2.shape, BF16),
        compiler_params=_params(("arbitrary", "arbitrary")),
        name="window_attn",
    )(sink, q2, k, k, k, kx, vt, vt, vt, vtx)


def _rep_rows(p, n):
    return jnp.concatenate([jnp.broadcast_to(p[s:s + 1, :], (n, p.shape[1])) for s in range(p.shape[0])], axis=0)


def _ctab(pr, pi, mr, mi):
    big_l, k = pr.shape[0], mr.shape[0]
    er, ei = _rep_rows(pr, k), _rep_rows(pi, k)
    tr, ti = jnp.tile(mr, (big_l, 1)), jnp.tile(mi, (big_l, 1))
    return er * tr - ei * ti, er * ti + ei * tr


def _cmul_rows(cr, ci, s):
    n = cr.shape[1]
    return jnp.concatenate([cr, cr], axis=1) * s + jnp.concatenate([-ci, ci], axis=1) * pltpu.roll(s, n, 1)


def _seg_scan(x, cr, ci, rowm, nper, reverse):
    rows = x.shape[0]
    s, sh = x, 1
    while sh < nper:
        if reverse:
            shifted, valid = pltpu.roll(s, rows - sh, 0), rowm < nper - sh
        else:
            shifted, valid = pltpu.roll(s, sh, 0), rowm >= sh
        s = s + jnp.where(valid, _cmul_rows(cr, ci, shifted), 0.0)
        cr, ci = cr * cr - ci * ci, 2.0 * cr * ci
        sh *= 2
    return s


def _s5_kernel(ul_ref, ux_ref, lre_ref, lim_ref, ls_ref, bre_ref, bim_ref, cre_ref, cim_ref,
               yl_ref, yx_ref, t_sc, *, nb, ncl, ncx):
    big_l = S5_CHUNK
    k = S5_GROUP_CH
    n = S5_STATE
    w = big_l * k
    tau = lax.broadcasted_iota(jnp.int32, (big_l, 1), 0).astype(F32)
    tabs = []
    for d in range(N_DIRS):
        lr = jnp.minimum(lre_ref[0, d], S5_MAX_RE)
        li = lim_ref[0, d]
        dl = jnp.exp(ls_ref[0, d])
        ar, th = lr * dl, li * dl

        def power(t, ar=ar, th=th):
            mag = jnp.exp(t * ar)
            return mag * jnp.cos(t * th), mag * jnp.sin(t * th)

        lbr, lbi = power(1.0)
        den = lr * lr + li * li
        zr = ((lbr - 1.0) * lr + lbi * li) / den
        zi = (lbi * lr - (lbr - 1.0) * li) / den
        br, bi = bre_ref[0, d], bim_ref[0, d]
        bbr, bbi = zr * br - zi * bi, zr * bi + zi * br
        cr, ci = cre_ref[0, d], cim_ref[0, d]
        tabs.append(dict(power=power, bbr=bbr, bbi=bbi, cr=cr, ci=ci, lam_l=power(float(big_l))))
    f, b = tabs
    e_f = _ctab(*f["power"](big_l - 1.0 - tau), f["bbr"], f["bbi"])
    e_b = _ctab(*b["power"](tau), b["bbr"], b["bbi"])
    g_f = _ctab(*f["power"](tau + 1.0), f["cr"], f["ci"])
    g_b = _ctab(*b["power"](big_l - tau), b["cr"], b["ci"])
    a_b = _ctab(*b["power"](big_l - 1.0 - tau), b["cr"], b["ci"])
    c_f = jnp.tile(f["cr"], (big_l, 1)), jnp.tile(f["ci"], (big_l, 1))
    bb2 = lambda t: jnp.concatenate([t["bbr"], t["bbi"]], axis=1)
    neg = lambda t: jnp.concatenate([t[0], -t[1]], axis=1)
    ka = _dot_nt(bb2(b), neg(a_b), precision=HIGHEST)
    kb = _dot_nt(bb2(f), neg(g_f), precision=HIGHEST)
    kc = _dot_nt(bb2(f), neg(c_f), precision=HIGHEST)
    lane = lax.broadcasted_iota(jnp.int32, (1, w), 1)
    ka = ka + jnp.where(lane >= w - k, kc, 0.0)
    kall = jnp.concatenate([ka, kb], axis=1)
    for s in range(big_l):
        off = (big_l - 1 - s) * k
        t_sc[s * k:(s + 1) * k, :] = kall[:, off:off + w].astype(BF16)
    ecat = jnp.concatenate([e_f[0], e_f[1], e_b[0], e_b[1]], axis=1).astype(BF16)
    gcat = jnp.concatenate([g_f[0], -g_f[1], g_b[0], -g_b[1]], axis=1).astype(BF16)
    tmat = t_sc[...]

    def rowmod(nper):
        return jnp.concatenate([lax.broadcasted_iota(jnp.int32, (nper, 1), 0)] * nb, axis=0)

    def by_batch(rows, nper):
        return jnp.concatenate([jnp.broadcast_to(r, (nper, r.shape[1])) for r in rows], axis=0)

    ux = ux_ref[0]
    hx = _dot(ux, ecat)
    rmx = rowmod(ncx)
    rx = nb * ncx
    sxf = _seg_scan(hx[:, :2 * n], *f["lam_l"], rmx, ncx, False)
    sxb = _seg_scan(hx[:, 2 * n:], *b["lam_l"], rmx, ncx, True)
    hin_xf = jnp.where(rmx >= 1, pltpu.roll(sxf, 1, 0), 0.0)
    hin_xb = jnp.where(rmx < ncx - 1, pltpu.roll(sxb, rx - 1, 0), 0.0)
    hin_x = jnp.concatenate([hin_xf, hin_xb], axis=1).astype(BF16)
    yx_ref[0] = _dot(ux, tmat) + _dot_nt(hin_x, gcat)
    hc_f = [sxf[i * ncx + ncx - 1:i * ncx + ncx, :] for i in range(nb)]
    hc_b = [sxb[i * ncx:i * ncx + 1, :] for i in range(nb)]
    ul = ul_ref[0]
    hl = _dot(ul, ecat)
    rml = rowmod(ncl)
    rl = nb * ncl
    xf = jnp.where(rml == 0, by_batch(hc_f, ncl), pltpu.roll(hl[:, :2 * n], 1, 0))
    xb = jnp.where(rml == ncl - 1, by_batch(hc_b, ncl), pltpu.roll(hl[:, 2 * n:], rl - 1, 0))
    hin_f = _seg_scan(xf, *f["lam_l"], rml, ncl, False)
    hin_b = _seg_scan(xb, *b["lam_l"], rml, ncl, True)
    hin = jnp.concatenate([hin_f, hin_b], axis=1).astype(BF16)
    yl_ref[0] = _dot(ul, tmat) + _dot_nt(hin, gcat)


def _s5_call(ul, ux, lre, lim, ls, bre, bim, cre, cim, *, nb):
    g, rl, w = ul.shape
    rx = ux.shape[1]
    grp = lambda a: pl.BlockSpec((1,) + a.shape[1:], lambda i: (i,) + (0,) * (a.ndim - 1))
    return pl.pallas_call(
        functools.partial(_s5_kernel, nb=nb, ncl=rl // nb, ncx=rx // nb),
        grid=(g,),
        in_specs=[grp(a) for a in (ul, ux, lre, lim, ls, bre, bim, cre, cim)],
        out_specs=[grp(ul), grp(ux)],
        out_shape=[jax.ShapeDtypeStruct(ul.shape, F32), jax.ShapeDtypeStruct(ux.shape, F32)],
        scratch_shapes=[pltpu.VMEM((w, w), BF16)],
        compiler_params=_params(("arbitrary",)),
        name="s5",
    )(ul, ux, lre, lim, ls, bre, bim, cre, cim)


def _outproj_kernel(x_ref, gt_ref, yf_ref, yb_ref, z_ref, ga_ref, wa_ref, s5_ref, u_ref, ng_ref,
                    d5_ref, wglu_ref, bglu_ref, wout_ref, o_ref):
    ys = (yf_ref[...] + yb_ref[...]) * _silu(z_ref[...])
    ys = ys * lax.rsqrt(jnp.mean(ys * ys, axis=-1, keepdims=True) + NORM_EPS) * ng_ref[...]
    t = jax.nn.gelu(s5_ref[...] + d5_ref[...] * u_ref[...])
    t = _dot(t.astype(BF16), wglu_ref[...]) + bglu_ref[...]
    s5o = t[:, :D_S5] * jax.nn.sigmoid(t[:, D_S5:])
    cat = jnp.concatenate([ys.astype(BF16), ga_ref[...], wa_ref[...], s5o.astype(BF16)], axis=1)
    o_ref[...] = x_ref[...] + gt_ref[0] * _dot(cat, wout_ref[...])


def _outproj_call(x2, gate, yf, yb, z, ga, wa, s5, u, ng, d5, wglu, bglu, wout, *, tm, rows_per_seg):
    r, d = x2.shape
    tps = rows_per_seg // tm
    row = lambda a: pl.BlockSpec((tm, a.shape[1]), lambda i: (i, 0))
    seg = pl.BlockSpec((1, 1, d), lambda i: (i // tps, 0, 0))
    full = lambda a: pl.BlockSpec(a.shape, lambda i: (0,) * a.ndim)
    return pl.pallas_call(
        _outproj_kernel,
        grid=(r // tm,),
        in_specs=[row(x2), seg] + [row(a) for a in (yf, yb, z, ga, wa, s5, u)]
                 + [full(a) for a in (ng, d5, wglu, bglu, wout)],
        out_specs=row(x2),
        out_shape=jax.ShapeDtypeStruct(x2.shape, F32),
        compiler_params=_params(("arbitrary",)),
        name="outproj",
    )(x2, gate, yf, yb, z, ga, wa, s5, u, ng, d5, wglu, bglu, wout)


def _ffn_kernel(x_ref, xp_ref, xn_ref, sh_ref, sc_ref, gt_ref, g_ref, wgu_ref, cw_ref, cb_ref,
                wd_ref, gfin_ref, o_ref, *, tiles_per_seq, d_ff, final_norm):
    i = pl.program_id(0)
    tm = x_ref.shape[0]
    x = x_ref[...]
    g, sh, sc = g_ref[...], sh_ref[0], sc_ref[0]
    h = _rms_mod(x, g, sh, sc)
    he = jnp.concatenate([_rms_mod(xp_ref[...], g, sh, sc), h, _rms_mod(xn_ref[...], g, sh, sc)], axis=0)
    ge = _dot(he.astype(BF16), wgu_ref[:, :d_ff])
    upj = _dot(h.astype(BF16), wgu_ref[:, d_ff:])
    t = i % tiles_per_seq
    row = lax.broadcasted_iota(jnp.int32, (tm, 1), 0)
    up = jnp.where((row == 0) & (t == 0), 0.0, ge[HALO - 1:HALO - 1 + tm])
    dn = jnp.where((row == tm - 1) & (t == tiles_per_seq - 1), 0.0, ge[HALO + 1:HALO + 1 + tm])
    cw = cw_ref[...]
    conv = cw[0:1] * up + cw[1:2] * ge[HALO:HALO + tm] + cw[2:3] * dn + cb_ref[...]
    act = (_silu(conv) * upj).astype(BF16)
    y = x + gt_ref[0] * _dot(act, wd_ref[...])
    if final_norm:
        y = y * lax.rsqrt(jnp.mean(y * y, axis=-1, keepdims=True) + NORM_EPS) * gfin_ref[...]
    o_ref[...] = y


def _ffn_call(x2, shift, scale, gate, g, wgu, cw, cb, wd, gfin, *, tm, rows_per_seg, seq, final_norm):
    r, d = x2.shape
    d_ff = wd.shape[0]
    tps = rows_per_seg // tm
    hb = tm // HALO
    last = r // HALO - 1
    seg = pl.BlockSpec((1, 1, d), lambda i: (i // tps, 0, 0))
    full = lambda a: pl.BlockSpec(a.shape, lambda i: (0,) * a.ndim)
    once = lambda a: pl.BlockSpec(a.shape, lambda i: (0,) * a.ndim, pipeline_mode=pl.Buffered(1))
    return pl.pallas_call(
        functools.partial(_ffn_kernel, tiles_per_seq=seq // tm, d_ff=d_ff, final_norm=final_norm),
        grid=(r // tm,),
        in_specs=[pl.BlockSpec((tm, d), lambda i: (i, 0)),
                  pl.BlockSpec((HALO, d), lambda i: (jnp.maximum(i * hb - 1, 0), 0)),
                  pl.BlockSpec((HALO, d), lambda i: (jnp.minimum(i * hb + hb, last), 0)),
                  seg, seg, seg, full(g), once(wgu), full(cw), full(cb), once(wd), full(gfin)],
        out_specs=pl.BlockSpec((tm, d), lambda i: (i, 0)),
        out_shape=jax.ShapeDtypeStruct(x2.shape, F32),
        compiler_params=_params(("arbitrary",)),
        name="conv_ffn",
    )(x2, x2, x2, shift, scale, gate, g, wgu, cw, cb, wd, gfin)


def _rope_tables(seq):
    rows = seq // GRID_W
    r = jnp.broadcast_to(jnp.arange(rows, dtype=F32)[:, None], (rows, GRID_W)).reshape(-1)
    c = jnp.broadcast_to(jnp.arange(GRID_W, dtype=F32)[None, :], (rows, GRID_W)).reshape(-1)
    n_freq = HEAD_DIM // 4
    inv = ROPE_THETA ** (-jnp.arange(n_freq, dtype=F32) / n_freq)
    ang = jnp.concatenate([r[:, None] * inv, c[:, None] * inv], axis=-1)
    cos, sin, zero = jnp.cos(ang), jnp.sin(ang), jnp.zeros_like(ang)
    reps = LANES // HEAD_DIM
    return (jnp.tile(jnp.concatenate([cos, cos], axis=-1), (1, reps)),
            jnp.tile(jnp.concatenate([-sin, zero], axis=-1), (1, reps)),
            jnp.tile(jnp.concatenate([zero, sin], axis=-1), (1, reps)))


def _pad_w_in(w):
    offs = [0]
    for s in IN_SIZES:
        offs.append(offs[-1] + s)
    parts = [w[:, offs[j]:offs[j + 1]] for j in range(len(IN_SIZES))]
    parts[2] = jnp.pad(parts[2], ((0, 0), (0, LANES - IN_SIZES[2])))
    return jnp.concatenate(parts, axis=1).astype(BF16)


def _s5_rows(u2, b):
    s = u2.shape[0] // b
    nc = s // S5_CHUNK
    u = u2.reshape(b, nc, S5_CHUNK, S5_GROUPS, S5_GROUP_CH).transpose(3, 0, 1, 2, 4)
    return u.reshape(S5_GROUPS, b * nc, S5_CHUNK * S5_GROUP_CH).astype(BF16)


def _s5_back(y, b):
    g, r, _ = y.shape
    nc = r // b
    y = y.reshape(g, b, nc, S5_CHUNK, S5_GROUP_CH).transpose(1, 2, 3, 0, 4)
    return y.reshape(b * nc * S5_CHUNK, g * S5_GROUP_CH)


def kernel(x, c, ctx, c_ctx, w_mod, b_mod, g_mix, w_in, ssd_conv_w, ssd_conv_b, ssd_a_log, ssd_dt_bias, ssd_d, ssd_norm_g, ga_q_norm, ga_k_norm, wa_sink, s5_lambda_re, s5_lambda_im, s5_log_step, s5_b_re, s5_b_im, s5_c_re, s5_c_im, s5_d, s5_w_glu, s5_b_glu, w_out, g_ffn, w_gate, w_up, ffn_conv_w, ffn_conv_b, w_down, g_final):
    b, s, d = x.shape
    lc = ctx.shape[1]
    depth = w_mod.shape[0]
    hpg = 2
    tm = 512

    cc = jnp.zeros((8, d), F32).at[:b].set(c).at[b].set(c_ctx)
    mod = _mod_call(cc, w_mod, b_mod).reshape(depth, 8, 6, d)

    tabs_l = _rope_tables(s)
    one, zero = jnp.ones((lc, LANES), F32), jnp.zeros((lc, LANES), F32)
    tabs_c = (one, zero, zero)
    bd = jnp.kron(jnp.eye(LANES // HEAD_DIM, dtype=F32),
                  jnp.full((HEAD_DIM, HEAD_DIM), 1.0 / HEAD_DIM, F32)).astype(BF16)
    gfin = g_final.reshape(1, d)

    x2 = x.reshape(b * s, d)
    xc2 = ctx.reshape(b * lc, d)
    for i in range(depth):
        need_ctx = i < depth - 1
        ml = lambda j: mod[i, :b, j].reshape(b, 1, d)
        mc = lambda j: mod[i, b, j].reshape(1, 1, d)
        g_i = g_mix[i].reshape(1, d)
        w_pad = _pad_w_in(w_in[i])
        qn = jnp.tile(ga_q_norm[i], LANES // HEAD_DIM).reshape(1, LANES)
        kn = jnp.tile(ga_k_norm[i], LANES // HEAD_DIM).reshape(1, LANES)
        conv_w = ssd_conv_w[i]
        conv_b = ssd_conv_b[i].reshape(1, -1)
        pl_ = _inproj_call(x2, ml(0), ml(1), g_i, w_pad, tabs_l, qn, kn, bd, conv_w, conv_b,
                           tm=tm, rows_per_seg=s, seq=s)
        pc_ = _inproj_call(xc2, mc(0), mc(1), g_i, w_pad, tabs_c, qn, kn, bd, conv_w, conv_b,
                           tm=lc, rows_per_seg=b * lc, seq=lc)
        zl, xsl, btl, cml, dtl, gql, gkl, gvl, wql, wkl, wvl, ul = pl_
        zc, xsc, btc, cmc, dtc, gqc, gkc, gvc, wqc, wkc, wvc, uc = pc_

        alog_c = ssd_a_log[i].reshape(-1, 1)
        bias_c = ssd_dt_bias[i].reshape(-1, 1)
        dsk = jnp.repeat(ssd_d[i], HEAD_DIM).reshape(1, D_SSD)
        n_dt = N_DIRS * SSD_HEADS

        def ssd(xs2, bt, cm2, dt2, h0, n):
            dt_t = dt2.reshape(b, n, LANES)[:, :, :n_dt].transpose(0, 2, 1)
            return _ssd_call(xs2.reshape(b, n, -1), bt, cm2.reshape(b, n, -1), dt_t, h0, alog_c, bias_c, dsk)

        h0 = jnp.zeros((b, N_DIRS, SSD_GROUPS, SSD_STATE, hpg * HEAD_DIM), F32)
        yfc, ybc, hc = ssd(xsc, btc, cmc, dtc, h0, lc)
        yfl, ybl, _ = ssd(xsl, btl, cml, dtl, hc, s)

        kx_ga = gkc.reshape(b, lc, -1)
        k_all = jnp.concatenate([kx_ga, gkl.reshape(b, s, -1)], axis=1)
        v_all = jnp.concatenate([gvc, gvl], axis=2)
        y_ga = _flash_call(gql, k_all, v_all, None)

        kx_wa = wkc.reshape(b, lc, -1)
        sink = wa_sink[i].astype(F32)
        y_wa = _win_call(wql, wkl.reshape(b, s, -1), wvl, kx_wa, wvc, sink)

        gm = lambda a: jnp.moveaxis(a, 1, 0)
        lre = gm(s5_lambda_re[i])[:, :, None, :]
        lim = gm(s5_lambda_im[i])[:, :, None, :]
        ls = gm(s5_log_step[i])[:, :, None, None]
        bre = gm(s5_b_re[i]).transpose(0, 1, 3, 2)
        bim = gm(s5_b_im[i]).transpose(0, 1, 3, 2)
        cre, cim = gm(s5_c_re[i]), gm(s5_c_im[i])
        y5l, y5c = _s5_call(_s5_rows(ul, b), _s5_rows(uc, b), lre, lim, ls, bre, bim, cre, cim, nb=b)
        y5l, y5c = _s5_back(y5l, b), _s5_back(y5c, b)

        ng = ssd_norm_g[i].reshape(1, D_SSD)
        d5 = s5_d[i].reshape(1, D_S5)
        wglu = s5_w_glu[i].astype(BF16)
        bglu = s5_b_glu[i].reshape(1, -1)
        wout = w_out[i].astype(BF16)
        x2 = _outproj_call(x2, ml(2), yfl.reshape(b * s, -1), ybl.reshape(b * s, -1), zl, y_ga, y_wa,
                           y5l, ul, ng, d5, wglu, bglu, wout, tm=tm, rows_per_seg=s)

        gf_i = g_ffn[i].reshape(1, d)
        wgu = jnp.concatenate([w_gate[i], w_up[i]], axis=1).astype(BF16)
        cw = ffn_conv_w[i]
        cb = ffn_conv_b[i].reshape(1, -1)
        wd = w_down[i].astype(BF16)
        x2 = _ffn_call(x2, ml(3), ml(4), ml(5), gf_i, wgu, cw, cb, wd, gfin, tm=512, rows_per_seg=s,
                       seq=s, final_norm=not need_ctx)

        if need_ctx:
            yc_ga = _flash_call(gqc, kx_ga, gvc, None)
            yc_wa = _flash_call(wqc, kx_wa, wvc, sink)
            xc2 = _outproj_call(xc2, mc(2), yfc.reshape(b * lc, -1), ybc.reshape(b * lc, -1), zc, yc_ga,
                                yc_wa, y5c, uc, ng, d5, wglu, bglu, wout, tm=lc, rows_per_seg=b * lc)
            xc2 = _ffn_call(xc2, mc(3), mc(4), mc(5), gf_i, wgu, cw, cb, wd, gfin, tm=lc,
                            rows_per_seg=b * lc, seq=lc, final_norm=False)
    return x2.reshape(b, s, d)
```

```python
import functools

import jax
import jax.numpy as jnp
from jax import lax
from jax.experimental import pallas as pl
from jax.experimental.pallas import tpu as pltpu

F32 = jnp.float32
BF16 = jnp.bfloat16
HIGHEST = lax.Precision.HIGHEST

HEAD_DIM = 64
GRID_W = 64
ROPE_THETA = 10000.0
NORM_EPS = 1e-6
WINDOW = 128
N_DIRS = 2
SSD_HEADS = 4
SSD_GROUPS = 2
SSD_STATE = 128
SSD_CHUNK = 128
D_SSD = SSD_HEADS * HEAD_DIM
SSD_XBC = D_SSD + 2 * SSD_GROUPS * SSD_STATE
S5_GROUPS = 16
S5_GROUP_CH = 16
S5_STATE = 64
S5_MAX_RE = -1e-4
S5_CHUNK = 32
D_S5 = S5_GROUPS * S5_GROUP_CH
LANES = 128
HALO = 8
NEG_BIG = -1e30
LOG2E = 1.4426950408889634
Q_HEADS = 4
KV_HEADS = 2
HPG = Q_HEADS // KV_HEADS
ONES_ROWS = 16
FLASH_SUB_K = 256
FLASH_SUB_Q = 256
FLASH_DEPTH = 6
VMEM_LIMIT = 52 * 1024 * 1024

IN_SIZES = (D_SSD, SSD_XBC, N_DIRS * SSD_HEADS, 256, 128, 128, 256, 128, 128, D_S5)
P_Z, P_XBC, P_DT, P_GQ, P_GK, P_GV, P_WQ, P_WK, P_WV, P_U, P_END = (
    0, 256, 1024, 1152, 1408, 1536, 1664, 1920, 2048, 2176, 2432)


def _params(sem=None, flags=None):
    return pltpu.CompilerParams(dimension_semantics=sem, vmem_limit_bytes=VMEM_LIMIT, flags=flags)


def _silu(v):
    return v * jax.nn.sigmoid(v)


def _softplus(v):
    return jnp.maximum(v, 0.0) + jnp.log1p(jnp.exp(-jnp.abs(v)))


def _dot(a, b):
    return jnp.dot(a, b, preferred_element_type=F32)


def _dot_nt(a, b, precision=None):
    return lax.dot_general(a, b, (((1,), (1,)), ((), ())), preferred_element_type=F32,
                           precision=precision)


def _dot_tn(a, b):
    return lax.dot_general(a, b, (((0,), (0,)), ((), ())), preferred_element_type=F32)


def _mod_kernel(cc_ref, w_ref, b_ref, o_ref):
    s = _silu(cc_ref[...])
    o_ref[0] = jnp.dot(s, w_ref[0], preferred_element_type=F32, precision=HIGHEST) + b_ref[0]


def _mod_call(cc, w_mod, b_mod):
    n_layers, d, n = w_mod.shape
    tn = 1536
    return pl.pallas_call(
        _mod_kernel,
        grid=(n_layers, n // tn),
        in_specs=[pl.BlockSpec((8, d), lambda l, j: (0, 0)),
                  pl.BlockSpec((1, d, tn), lambda l, j: (l, 0, j)),
                  pl.BlockSpec((1, 1, tn), lambda l, j: (l, 0, j))],
        out_specs=pl.BlockSpec((1, 8, tn), lambda l, j: (l, 0, j)),
        out_shape=jax.ShapeDtypeStruct((n_layers, 8, n), F32),
        compiler_params=_params(("arbitrary", "arbitrary")),
        name="adaln_mod",
    )(cc, w_mod, b_mod.reshape(n_layers, 1, n))


def _rms_mod(x, g, shift, scale):
    y = x * lax.rsqrt(jnp.mean(x * x, axis=-1, keepdims=True) + NORM_EPS) * g
    return y * (1.0 + scale) + shift


def _rope(t, cos, sna, snb):
    return t * cos + pltpu.roll(t, 96, 1) * sna + pltpu.roll(t, 32, 1) * snb


def _head_rms(t, gain, bd):
    t2 = t * t
    hi = t2.astype(BF16)
    lo = (t2 - hi.astype(F32)).astype(BF16)
    ms = _dot(hi, bd) + _dot(lo, bd)
    return t * lax.rsqrt(ms + NORM_EPS) * gain


def _conv3_silu(v, prev, nxt, cw, cb):
    n = v.shape[0]
    row = lax.broadcasted_iota(jnp.int32, (n, 1), 0)
    up = jnp.where(row == 0, prev, pltpu.roll(v, 1, 0))
    dn = jnp.where(row == n - 1, nxt, pltpu.roll(v, n - 1, 0))
    return _silu(cw[0:1] * up + cw[1:2] * v + cw[2:3] * dn + cb)


def _inproj_kernel(x_ref, xp_ref, xn_ref, sh_ref, sc_ref, g_ref, w_ref, cos_ref, sna_ref, snb_ref,
                   qn_ref, kn_ref, bd_ref, cw_ref, cb_ref, z_ref, xs_ref, bt_ref, cm_ref, dt_ref, gq_ref, gk_ref,
                   gv_ref, wq_ref, wk_ref, wv_ref, u_ref, u5_ref, u_sc, *, tiles_per_seq):
    g, sh, sc = g_ref[...], sh_ref[0], sc_ref[0]
    hb = _rms_mod(x_ref[...], g, sh, sc).astype(BF16)
    halo = jnp.concatenate([xp_ref[...], xn_ref[...]], axis=0)
    u = _dot(hb, w_ref[:, P_U:P_END])
    p = _dot(hb, w_ref[:, P_Z:P_GQ])
    ph = _dot(_rms_mod(halo, g, sh, sc).astype(BF16), w_ref[:, P_XBC:P_DT])
    pg = _dot(hb, w_ref[:, P_GQ:P_WQ])
    pw = _dot(hb, w_ref[:, P_WQ:P_U])
    u_ref[...] = u
    gpl = LANES // S5_GROUP_CH
    for half in range(D_S5 // LANES):
        u_sc[half] = u[:, half * LANES:(half + 1) * LANES]
    for s in range(S5_CHUNK):
        for half in range(D_S5 // LANES):
            rows = u_sc[half, pl.ds(s, u_sc.shape[1] // S5_CHUNK, stride=S5_CHUNK), :]
            for j in range(gpl):
                u5_ref[half * gpl + j, :, s * S5_GROUP_CH:(s + 1) * S5_GROUP_CH] = (
                    rows[:, j * S5_GROUP_CH:(j + 1) * S5_GROUP_CH])
    cos, sna, snb = cos_ref[...], sna_ref[...], snb_ref[...]
    bd = bd_ref[...]
    q_scale = HEAD_DIM ** -0.5 * LOG2E
    z_ref[...] = p[:, P_Z:P_XBC]
    t = pl.program_id(0) % tiles_per_seq
    prev = jnp.where(t > 0, ph[HALO - 1:HALO, :], 0.0)
    nxt = jnp.where(t < tiles_per_seq - 1, ph[HALO:HALO + 1, :], 0.0)
    act = _conv3_silu(p[:, P_XBC:P_DT], prev, nxt, cw_ref[...], cb_ref[...])
    xs_ref[...] = act[:, :D_SSD]
    n_bc = SSD_GROUPS * SSD_STATE
    bt_ref[0] = jnp.transpose(act[:, D_SSD:D_SSD + n_bc]).astype(BF16)
    cm_ref[...] = act[:, D_SSD + n_bc:].astype(BF16)
    dt_ref[...] = p[:, P_DT:P_GQ]
    for j in range(2):
        q = _rope(_head_rms(pg[:, j * LANES:(j + 1) * LANES], qn_ref[...], bd), cos, sna, snb) * q_scale
        gq_ref[:, j * LANES:(j + 1) * LANES] = q.astype(BF16)
    gk_ref[...] = _rope(_head_rms(pg[:, P_GK - P_GQ:P_GV - P_GQ], kn_ref[...], bd), cos, sna, snb).astype(BF16)
    gv_ref[0] = jnp.transpose(pg[:, P_GV - P_GQ:]).astype(BF16)
    for j in range(2):
        q = _rope(pw[:, j * LANES:(j + 1) * LANES], cos, sna, snb) * q_scale
        wq_ref[:, j * LANES:(j + 1) * LANES] = q.astype(BF16)
    wk_ref[...] = _rope(pw[:, P_WK - P_WQ:P_WV - P_WQ], cos, sna, snb).astype(BF16)
    wv_ref[0] = jnp.transpose(pw[:, P_WV - P_WQ:]).astype(BF16)


def _inproj_call(x2, shift, scale, g, w_pad, tabs, qn, kn, bd, cw, cb, *, tm, rows_per_seg, seq):
    r, d = x2.shape
    tps = rows_per_seg // tm
    tpq = seq // tm
    hb = tm // HALO
    last = r // HALO - 1
    row = lambda w: pl.BlockSpec((tm, w), lambda i: (i, 0))
    seg = pl.BlockSpec((1, 1, d), lambda i: (i // tps, 0, 0))
    full = lambda a: pl.BlockSpec(a.shape, lambda i: (0,) * a.ndim)
    tab = pl.BlockSpec((tm, LANES), lambda i: (i % tpq, 0))
    n_bc = SSD_GROUPS * SSD_STATE
    widths = (256, D_SSD, -n_bc, n_bc, LANES, 256, 128, -LANES, 256, 128, -LANES, D_S5)
    dtypes = (F32, F32, BF16, BF16, F32, BF16, BF16, BF16, BF16, BF16, BF16, F32)
    spec = lambda w: row(w) if w > 0 else pl.BlockSpec((1, -w, tm), lambda i: (i // tpq, 0, i % tpq))
    shape = lambda w: (r, w) if w > 0 else (r // seq, -w, seq)
    w5 = S5_CHUNK * S5_GROUP_CH
    u5_spec = pl.BlockSpec((S5_GROUPS, tm // S5_CHUNK, w5), lambda i: (0, i, 0))
    return pl.pallas_call(
        functools.partial(_inproj_kernel, tiles_per_seq=tpq),
        grid=(r // tm,),
        in_specs=[row(d),
                  pl.BlockSpec((HALO, d), lambda i: (jnp.maximum(i * hb - 1, 0), 0)),
                  pl.BlockSpec((HALO, d), lambda i: (jnp.minimum(i * hb + hb, last), 0)),
                  seg, seg, full(g), full(w_pad), tab, tab, tab, full(qn), full(kn), full(bd),
                  full(cw), full(cb)],
        out_specs=[spec(w) for w in widths] + [u5_spec],
        out_shape=[jax.ShapeDtypeStruct(shape(w), t) for w, t in zip(widths, dtypes)]
                  + [jax.ShapeDtypeStruct((S5_GROUPS, r // S5_CHUNK, w5), F32)],
        scratch_shapes=[pltpu.VMEM((D_S5 // LANES, tm, LANES), F32)],
        compiler_params=_params(("arbitrary",)),
        name="inproj",
    )(x2, x2, x2, shift, scale, g, w_pad, *tabs, qn, kn, bd, cw, cb)


def _ssd_kernel(xsf_ref, btf_ref, cmf_ref, xsb_ref, btb_ref, cmb_ref, dtrf_ref, dtrb_ref, h0_ref, alc_ref,
                bic_ref, dsk_ref, yf_ref, yb_ref, hout_ref, h_sc, *, nc):
    c = pl.program_id(1)

    @pl.when(c == 0)
    def _():
        h_sc[...] = h0_ref[0]

    q = SSD_CHUNK
    hpg = SSD_HEADS // SSD_GROUPS
    dirs = ((0, xsf_ref, None, dtrf_ref, yf_ref), (1, xsb_ref, None, dtrb_ref, yb_ref))
    bt_refs, cm_refs = (btf_ref, btb_ref), (cmf_ref, cmb_ref)
    pairs = [(d, g) for d in range(N_DIRS) for g in range(SSD_GROUPS)]
    bmat_t = lambda d, g: bt_refs[d][0, g * SSD_STATE:(g + 1) * SSD_STATE, :]
    cmat = lambda d, g: cm_refs[d][0, :, g * SSD_STATE:(g + 1) * SSD_STATE]
    ri = lax.broadcasted_iota(jnp.int32, (q, q), 0)
    ci = lax.broadcasted_iota(jnp.int32, (q, q), 1)
    mask = (ri >= ci, ri <= ci)
    lane = lax.broadcasted_iota(jnp.int32, (1, LANES), 1)
    first = lane < HEAD_DIM

    h_old = {p: h_sc[p[0], p[1]] for p in pairs}
    cb = {(d, g): _dot(cmat(d, g), bmat_t(d, g)) for d, g in pairs}
    yoff = {(d, g): _dot(cmat(d, g), h_old[d, g].astype(BF16)) for d, g in pairs}

    dt_r, acs_c, acs_r, dt_c, tot = {}, {}, {}, {}, {}
    for d, _, _, dtr_ref, _ in dirs:
        dt_r[d] = _softplus(dtr_ref[0] + bic_ref[...])
    for d in range(N_DIRS):
        dta_r = dt_r[d] * (-jnp.exp(alc_ref[...]))
        maskf = mask[d].astype(F32)
        pad = jnp.zeros((LANES - dta_r.shape[0], q), F32)
        acs_c[d] = _dot_nt(maskf, jnp.concatenate([dta_r, pad], axis=0), precision=HIGHEST)
        acs_r[d] = _dot_nt(dta_r, maskf, precision=HIGHEST)
        dt_c[d] = jnp.transpose(jnp.concatenate([dt_r[d], pad], axis=0))
        tot[d] = acs_c[d][q - 1:q] if d == 0 else acs_c[d][0:1]

    xdt, acs_g, tot_g = {}, {}, {}
    for d, g in pairs:
        e0 = d * SSD_HEADS + g * hpg
        pick = lambda v: jnp.where(first, v[:, e0:e0 + 1], v[:, e0 + 1:e0 + 2])
        acs_g[d, g], tot_g[d, g] = pick(acs_c[d]), pick(tot[d])
        xs_g = dirs[d][1][0, :, g * LANES:(g + 1) * LANES]
        xdt[d, g] = xs_g * pick(dt_c[d])
        xdt_b = xdt[d, g].astype(BF16)
        yd = []
        for a in range(hpg):
            e = e0 + a
            dec = jnp.where(mask[d], jnp.exp(acs_c[d][:, e:e + 1] - acs_r[d][e:e + 1, :]), 0.0)
            yd.append(_dot((cb[d, g] * dec).astype(BF16), xdt_b))
        y = jnp.where(first, yd[0], yd[1]) + yoff[d, g] * jnp.exp(acs_g[d, g])
        if d == 0:
            y = y + dsk_ref[:, g * LANES:(g + 1) * LANES] * xs_g
        dirs[d][4][0, :, g * LANES:(g + 1) * LANES] = y

    for d, g in pairs:
        w = (xdt[d, g] * jnp.exp(tot_g[d, g] - acs_g[d, g])).astype(BF16)
        h_sc[d, g] = h_old[d, g] * jnp.exp(tot_g[d, g]) + _dot(bmat_t(d, g), w)

    @pl.when(c == nc - 1)
    def _():
        hout_ref[0] = h_sc[...]


def _ssd_call(xs, bt, cm, dt_t, h0, alog_c, bias_c, dsk):
    b, s, _ = xs.shape
    q = SSD_CHUNK
    nc = s // q
    fwd = lambda c: c
    bwd = lambda c: nc - 1 - c
    rows = lambda a, f: pl.BlockSpec((1, q, a.shape[2]), lambda i, c: (i, f(c), 0))
    cols = lambda a, f: pl.BlockSpec((1, a.shape[1], q), lambda i, c: (i, 0, f(c)))
    drspec = lambda f: cols(dt_t, f)
    full = lambda a: pl.BlockSpec(a.shape, lambda i, c: (0,) * a.ndim)
    hspec = pl.BlockSpec((1,) + h0.shape[1:], lambda i, c: (i, 0, 0, 0, 0))
    return pl.pallas_call(
        functools.partial(_ssd_kernel, nc=nc),
        grid=(b, nc),
        in_specs=[rows(xs, fwd), cols(bt, fwd), rows(cm, fwd), rows(xs, bwd), cols(bt, bwd), rows(cm, bwd),
                  drspec(fwd), drspec(bwd), hspec, full(alog_c), full(bias_c), full(dsk)],
        out_specs=[rows(xs, fwd), rows(xs, bwd), hspec],
        out_shape=[jax.ShapeDtypeStruct(xs.shape, F32), jax.ShapeDtypeStruct(xs.shape, F32),
                   jax.ShapeDtypeStruct(h0.shape, F32)],
        scratch_shapes=[pltpu.VMEM(h0.shape[1:], F32)],
        compiler_params=_params(("arbitrary", "arbitrary")),
        name="ssd",
    )(xs, bt, cm, xs, bt, cm, dt_t, dt_t, h0, alog_c, bias_c, dsk)


def _q_transposed(q):
    qt = jnp.transpose(q.astype(F32))
    zero = jnp.zeros((HEAD_DIM, q.shape[0]), BF16)
    out = []
    for h in range(Q_HEADS):
        blk = qt[h * HEAD_DIM:(h + 1) * HEAD_DIM].astype(BF16)
        out.append(jnp.concatenate([blk, zero] if h // HPG == 0 else [zero, blk], axis=0))
    return out


def _flash_tile(k_ref, vt_ref, qt_sc, m_sc, acc_sc):
    tq = qt_sc.shape[2]
    tk = k_ref.shape[1]
    ones = jnp.ones((ONES_ROWS, FLASH_SUB_K), BF16)
    chains = [(h, slice(jq * FLASH_SUB_Q, (jq + 1) * FLASH_SUB_Q))
              for h in range(Q_HEADS) for jq in range(tq // FLASH_SUB_Q)]
    m = [m_sc[h, :, cs] for h, cs in chains]
    acc = [acc_sc[h, :, cs] for h, cs in chains]
    units = [(jk, c) for jk in range(tk // FLASH_SUB_K) for c in range(len(chains))]

    def scores(u):
        jk, c = u
        h, cs = chains[c]
        return _dot(k_ref[0, jk * FLASH_SUB_K:(jk + 1) * FLASH_SUB_K, :], qt_sc[h, :, cs])

    def finish(u, s):
        jk, c = u
        g = chains[c][0] // HPG
        vte = jnp.concatenate(
            [vt_ref[0, g * HEAD_DIM:(g + 1) * HEAD_DIM, jk * FLASH_SUB_K:(jk + 1) * FLASH_SUB_K], ones], axis=0)
        m_new = jnp.maximum(m[c], jnp.max(s, axis=0, keepdims=True))
        p = jnp.exp2(s - m_new).astype(BF16)
        acc[c] = jnp.exp2(m[c] - m_new) * acc[c] + _dot(vte, p)
        m[c] = m_new

    pending = [scores(u) for u in units[:FLASH_DEPTH]]
    for i, u in enumerate(units):
        s_cur = pending.pop(0)
        if i + FLASH_DEPTH < len(units):
            pending.append(scores(units[i + FLASH_DEPTH]))
        finish(u, s_cur)
    for c, (h, cs) in enumerate(chains):
        m_sc[h, :, cs] = m[c]
        acc_sc[h, :, cs] = acc[c]


def _flash_kernel(sink_ref, q_ref, k_ref, vt_ref, *rest, nk, has_sink, has_extra):
    if has_extra:
        kx_ref, vtx_ref, o_ref, qt_sc, m_sc, acc_sc = rest
    else:
        o_ref, qt_sc, m_sc, acc_sc = rest
    ki = pl.program_id(2)
    tq = q_ref.shape[0]

    @pl.when(ki == 0)
    def _():
        qts = _q_transposed(q_ref[...])
        for h in range(Q_HEADS):
            qt_sc[h] = qts[h]
            if has_sink:
                m_sc[h] = jnp.full(m_sc.shape[1:], sink_ref[h] * LOG2E, F32)
                acc_sc[h] = jnp.concatenate([jnp.zeros((HEAD_DIM, tq), F32), jnp.ones((ONES_ROWS, tq), F32)], axis=0)
            else:
                m_sc[h] = jnp.full(m_sc.shape[1:], NEG_BIG, F32)
                acc_sc[h] = jnp.zeros(acc_sc.shape[1:], F32)
        if has_extra:
            _flash_tile(kx_ref, vtx_ref, qt_sc, m_sc, acc_sc)

    _flash_tile(k_ref, vt_ref, qt_sc, m_sc, acc_sc)

    @pl.when(ki == nk - 1)
    def _():
        outs = []
        for h in range(Q_HEADS):
            acc = acc_sc[h]
            outs.append(acc[:HEAD_DIM] / acc[HEAD_DIM:HEAD_DIM + 1])
        o_ref[...] = jnp.transpose(jnp.concatenate(outs, axis=0)).astype(o_ref.dtype)


def _pick(n, cands):
    for c in cands:
        if n % c == 0:
            return c
    return n


def _flash_call(q2, k, vt, sink, extra=None):
    b, sk, kw = k.shape
    sq = q2.shape[0] // b
    tq = _pick(sq, (512, 256))
    tk = _pick(sk, (2048, 1024, 512, 256))
    nq = sq // tq
    nk = sk // tk
    has_sink = sink is not None
    if sink is None:
        sink = jnp.zeros((Q_HEADS,), F32)
    in_specs = [pl.BlockSpec(memory_space=pltpu.SMEM),
                pl.BlockSpec((tq, q2.shape[1]), lambda i, qi, ki: (i * nq + qi, 0)),
                pl.BlockSpec((1, tk, kw), lambda i, qi, ki: (i, ki, 0)),
                pl.BlockSpec((1, kw, tk), lambda i, qi, ki: (i, 0, ki))]
    args = [sink, q2, k, vt]
    if extra is not None:
        lx = extra[0].shape[1]
        in_specs += [pl.BlockSpec((1, lx, kw), lambda i, qi, ki: (i, 0, 0)),
                     pl.BlockSpec((1, kw, lx), lambda i, qi, ki: (i, 0, 0))]
        args += list(extra)
    return pl.pallas_call(
        functools.partial(_flash_kernel, nk=nk, has_sink=has_sink, has_extra=extra is not None),
        grid=(b, nq, nk),
        in_specs=in_specs,
        out_specs=pl.BlockSpec((tq, q2.shape[1]), lambda i, qi, ki: (i * nq + qi, 0)),
        out_shape=jax.ShapeDtypeStruct(q2.shape, BF16),
        scratch_shapes=[pltpu.VMEM((Q_HEADS, kw, tq), BF16), pltpu.VMEM((Q_HEADS, 1, tq), F32),
                        pltpu.VMEM((Q_HEADS, HEAD_DIM + ONES_ROWS, tq), F32)],
        compiler_params=_params(("arbitrary", "arbitrary", "arbitrary")),
        name="flash_sink" if has_sink else "flash",
    )(*args)


def _win_kernel(sink_ref, q_ref, kp_ref, kc_ref, kn_ref, kx_ref, vtp_ref, vtc_ref, vtn_ref, vtx_ref,
                o_ref, *, nq):
    j = pl.program_id(1)
    tq = q_ref.shape[0]
    blk = kp_ref.shape[1]
    rc = lax.broadcasted_iota(jnp.int32, (tq, tq), 0) - lax.broadcasted_iota(jnp.int32, (tq, tq), 1)
    m_cur = (rc <= WINDOW) & (rc >= -WINDOW)
    rp = lax.broadcasted_iota(jnp.int32, (blk, tq), 0) - lax.broadcasted_iota(jnp.int32, (blk, tq), 1)
    m_prev = (rp - blk >= -WINDOW) & (j > 0)
    m_next = (rp + tq <= WINDOW) & (j < nq - 1)
    qts = _q_transposed(q_ref[...])
    raw = [[_dot(kr[0], qt) for kr in (kc_ref, kp_ref, kn_ref, kx_ref)] for qt in qts]
    outs = []
    for h in range(Q_HEADS):
        rows = slice((h // HPG) * HEAD_DIM, (h // HPG + 1) * HEAD_DIM)
        sink = sink_ref[h] * LOG2E
        s_c = jnp.where(m_cur, raw[h][0], NEG_BIG)
        s_p = jnp.where(m_prev, raw[h][1], NEG_BIG)
        s_n = jnp.where(m_next, raw[h][2], NEG_BIG)
        s_x = raw[h][3]
        cmax = lambda v: jnp.max(v, axis=0, keepdims=True)
        m = jnp.maximum(jnp.maximum(cmax(s_c), cmax(s_p)), jnp.maximum(cmax(s_n), cmax(s_x)))
        m = jnp.maximum(m, sink)
        p_c, p_p, p_n, p_x = (jnp.exp2(v - m) for v in (s_c, s_p, s_n, s_x))
        csum = lambda v: jnp.sum(v, axis=0, keepdims=True)
        l = csum(p_c) + csum(p_p) + csum(p_n) + csum(p_x) + jnp.exp2(sink - m)
        o = (_dot(vtc_ref[0, rows, :], p_c.astype(BF16)) + _dot(vtp_ref[0, rows, :], p_p.astype(BF16))
             + _dot(vtn_ref[0, rows, :], p_n.astype(BF16)) + _dot(vtx_ref[0, rows, :], p_x.astype(BF16)))
        outs.append(o / l)
    o_ref[...] = jnp.transpose(jnp.concatenate(outs, axis=0)).astype(o_ref.dtype)


def _win_call(q2, k, vt, kx, vtx, sink):
    b, s, kw = k.shape
    lc = kx.shape[1]
    blk = WINDOW
    tq = _pick(s, (512, 256, 128))
    nq = s // tq
    r = tq // blk
    nblk = s // blk
    prev = lambda t: jnp.maximum(t * r - 1, 0)
    nxt = lambda t: jnp.minimum(t * r + r, nblk - 1)
    return pl.pallas_call(
        functools.partial(_win_kernel, nq=nq),
        grid=(b, nq),
        in_specs=[pl.BlockSpec(memory_space=pltpu.SMEM),
                  pl.BlockSpec((tq, q2.shape[1]), lambda i, t: (i * nq + t, 0)),
                  pl.BlockSpec((1, blk, kw), lambda i, t: (i, prev(t), 0)),
                  pl.BlockSpec((1, tq, kw), lambda i, t: (i, t, 0)),
                  pl.BlockSpec((1, blk, kw), lambda i, t: (i, nxt(t), 0)),
                  pl.BlockSpec((1, lc, kw), lambda i, t: (i, 0, 0)),
                  pl.BlockSpec((1, kw, blk), lambda i, t: (i, 0, prev(t))),
                  pl.BlockSpec((1, kw, tq), lambda i, t: (i, 0, t)),
                  pl.BlockSpec((1, kw, blk), lambda i, t: (i, 0, nxt(t))),
                  pl.BlockSpec((1, kw, lc), lambda i, t: (i, 0, 0))],
        out_specs=pl.BlockSpec((tq, q2.shape[1]), lambda i, t: (i * nq + t, 0)),
        out_shape=jax.ShapeDtypeStruct(q2.shape, BF16),
        compiler_params=_params(("arbitrary", "arbitrary")),
        name="window_attn",
    )(sink, q2, k, k, k, kx, vt, vt, vt, vtx)


def _rep_rows(p, n):
    return jnp.concatenate([jnp.broadcast_to(p[s:s + 1, :], (n, p.shape[1])) for s in range(p.shape[0])], axis=0)


def _ctab(pr, pi, mr, mi):
    big_l, k = pr.shape[0], mr.shape[0]
    er, ei = _rep_rows(pr, k), _rep_rows(pi, k)
    tr, ti = jnp.tile(mr, (big_l, 1)), jnp.tile(mi, (big_l, 1))
    return er * tr - ei * ti, er * ti + ei * tr


def _cmul_rows(cr, ci, s):
    n = cr.shape[1]
    return jnp.concatenate([cr, cr], axis=1) * s + jnp.concatenate([-ci, ci], axis=1) * pltpu.roll(s, n, 1)


def _seg_scan(x, cr, ci, rowm, nper, reverse):
    rows = x.shape[0]
    s, sh = x, 1
    while sh < nper:
        if reverse:
            shifted, valid = pltpu.roll(s, rows - sh, 0), rowm < nper - sh
        else:
            shifted, valid = pltpu.roll(s, sh, 0), rowm >= sh
        s = s + jnp.where(valid, _cmul_rows(cr, ci, shifted), 0.0)
        cr, ci = cr * cr - ci * ci, 2.0 * cr * ci
        sh *= 2
    return s


def _s5_kernel(ul_ref, ux_ref, lre_ref, lim_ref, ls_ref, bre_ref, bim_ref, cre_ref, cim_ref,
               yl_ref, yx_ref, t_sc, *, nb, ncl, ncx):
    big_l = S5_CHUNK
    k = S5_GROUP_CH
    n = S5_STATE
    w = big_l * k
    tau = lax.broadcasted_iota(jnp.int32, (big_l, 1), 0).astype(F32)
    tabs = []
    for d in range(N_DIRS):
        lr = jnp.minimum(lre_ref[0, d], S5_MAX_RE)
        li = lim_ref[0, d]
        dl = jnp.exp(ls_ref[0, d])
        ar, th = lr * dl, li * dl

        def power(t, ar=ar, th=th):
            mag = jnp.exp(t * ar)
            return mag * jnp.cos(t * th), mag * jnp.sin(t * th)

        lbr, lbi = power(1.0)
        den = lr * lr + li * li
        zr = ((lbr - 1.0) * lr + lbi * li) / den
        zi = (lbi * lr - (lbr - 1.0) * li) / den
        br, bi = bre_ref[0, d], bim_ref[0, d]
        bbr, bbi = zr * br - zi * bi, zr * bi + zi * br
        cr, ci = cre_ref[0, d], cim_ref[0, d]
        tabs.append(dict(power=power, bbr=bbr, bbi=bbi, cr=cr, ci=ci, lam_l=power(float(big_l))))
    f, b = tabs
    e_f = _ctab(*f["power"](big_l - 1.0 - tau), f["bbr"], f["bbi"])
    e_b = _ctab(*b["power"](tau), b["bbr"], b["bbi"])
    g_f = _ctab(*f["power"](tau + 1.0), f["cr"], f["ci"])
    g_b = _ctab(*b["power"](big_l - tau), b["cr"], b["ci"])
    a_b = _ctab(*b["power"](big_l - 1.0 - tau), b["cr"], b["ci"])
    c_f = jnp.tile(f["cr"], (big_l, 1)), jnp.tile(f["ci"], (big_l, 1))
    bb2 = lambda t: jnp.concatenate([t["bbr"], t["bbi"]], axis=1)
    neg = lambda t: jnp.concatenate([t[0], -t[1]], axis=1)
    ka = _dot_nt(bb2(b), neg(a_b), precision=HIGHEST)
    kb = _dot_nt(bb2(f), neg(g_f), precision=HIGHEST)
    kc = _dot_nt(bb2(f), neg(c_f), precision=HIGHEST)
    lane = lax.broadcasted_iota(jnp.int32, (1, w), 1)
    ka = ka + jnp.where(lane >= w - k, kc, 0.0)
    kall = jnp.concatenate([ka, kb], axis=1)
    for s in range(big_l):
        off = (big_l - 1 - s) * k
        t_sc[s * k:(s + 1) * k, :] = kall[:, off:off + w].astype(BF16)
    ecat = jnp.concatenate([e_f[0], e_f[1], e_b[0], e_b[1]], axis=1).astype(BF16)
    gcat = jnp.concatenate([g_f[0], -g_f[1], g_b[0], -g_b[1]], axis=1).astype(BF16)
    tmat = t_sc[...]

    def rowmod(nper):
        return jnp.concatenate([lax.broadcasted_iota(jnp.int32, (nper, 1), 0)] * nb, axis=0)

    def by_batch(rows, nper):
        return jnp.concatenate([jnp.broadcast_to(r, (nper, r.shape[1])) for r in rows], axis=0)

    ux = ux_ref[0].astype(BF16)
    hx = _dot(ux, ecat)
    rmx = rowmod(ncx)
    rx = nb * ncx
    sxf = _seg_scan(hx[:, :2 * n], *f["lam_l"], rmx, ncx, False)
    sxb = _seg_scan(hx[:, 2 * n:], *b["lam_l"], rmx, ncx, True)
    hin_xf = jnp.where(rmx >= 1, pltpu.roll(sxf, 1, 0), 0.0)
    hin_xb = jnp.where(rmx < ncx - 1, pltpu.roll(sxb, rx - 1, 0), 0.0)
    hin_x = jnp.concatenate([hin_xf, hin_xb], axis=1).astype(BF16)
    yx_ref[0] = _dot(ux, tmat) + _dot_nt(hin_x, gcat)
    hc_f = [sxf[i * ncx + ncx - 1:i * ncx + ncx, :] for i in range(nb)]
    hc_b = [sxb[i * ncx:i * ncx + 1, :] for i in range(nb)]
    ul = ul_ref[0].astype(BF16)
    hl = _dot(ul, ecat)
    rml = rowmod(ncl)
    rl = nb * ncl
    xf = jnp.where(rml == 0, by_batch(hc_f, ncl), pltpu.roll(hl[:, :2 * n], 1, 0))
    xb = jnp.where(rml == ncl - 1, by_batch(hc_b, ncl), pltpu.roll(hl[:, 2 * n:], rl - 1, 0))
    hin_f = _seg_scan(xf, *f["lam_l"], rml, ncl, False)
    hin_b = _seg_scan(xb, *b["lam_l"], rml, ncl, True)
    hin = jnp.concatenate([hin_f, hin_b], axis=1).astype(BF16)
    yl_ref[0] = _dot(ul, tmat) + _dot_nt(hin, gcat)


def _s5_call(ul, ux, lre, lim, ls, bre, bim, cre, cim, *, nb):
    g, rl, w = ul.shape
    rx = ux.shape[1]
    grp = lambda a: pl.BlockSpec((1,) + a.shape[1:], lambda i: (i,) + (0,) * (a.ndim - 1))
    return pl.pallas_call(
        functools.partial(_s5_kernel, nb=nb, ncl=rl // nb, ncx=rx // nb),
        grid=(g,),
        in_specs=[grp(a) for a in (ul, ux, lre, lim, ls, bre, bim, cre, cim)],
        out_specs=[grp(ul), grp(ux)],
        out_shape=[jax.ShapeDtypeStruct(ul.shape, F32), jax.ShapeDtypeStruct(ux.shape, F32)],
        scratch_shapes=[pltpu.VMEM((w, w), BF16)],
        compiler_params=_params(("arbitrary",)),
        name="s5",
    )(ul, ux, lre, lim, ls, bre, bim, cre, cim)


def _outproj_kernel(x_ref, gt_ref, yf_ref, yb_ref, z_ref, ga_ref, wa_ref, s5_ref, u_ref, ng_ref,
                    d5_ref, wglu_ref, bglu_ref, wout_ref, o_ref, y5_sc):
    ys = (yf_ref[...] + yb_ref[...]) * _silu(z_ref[...])
    ys = ys * lax.rsqrt(jnp.mean(ys * ys, axis=-1, keepdims=True) + NORM_EPS) * ng_ref[...]
    n_rest = wout_ref.shape[0] - D_S5
    part = _dot(jnp.concatenate([ys.astype(BF16), ga_ref[...], wa_ref[...]], axis=1), wout_ref[:n_rest, :])
    gpl = LANES // S5_GROUP_CH
    nck = y5_sc.shape[1] // S5_CHUNK
    for s in range(S5_CHUNK):
        for half in range(D_S5 // LANES):
            y5_sc[half, pl.ds(s, nck, stride=S5_CHUNK), :] = jnp.concatenate(
                [s5_ref[half * gpl + j, :, s * S5_GROUP_CH:(s + 1) * S5_GROUP_CH] for j in range(gpl)], axis=1)
    y5 = jnp.concatenate([y5_sc[half] for half in range(D_S5 // LANES)], axis=1)
    t = jax.nn.gelu(y5 + d5_ref[...] * u_ref[...])
    t = _dot(t.astype(BF16), wglu_ref[...]) + bglu_ref[...]
    s5o = t[:, :D_S5] * jax.nn.sigmoid(t[:, D_S5:])
    o_ref[...] = x_ref[...] + gt_ref[0] * (part + _dot(s5o.astype(BF16), wout_ref[n_rest:, :]))


def _outproj_call(x2, gate, yf, yb, z, ga, wa, s5, u, ng, d5, wglu, bglu, wout, *, tm, rows_per_seg):
    r, d = x2.shape
    tps = rows_per_seg // tm
    row = lambda a: pl.BlockSpec((tm, a.shape[1]), lambda i: (i, 0))
    seg = pl.BlockSpec((1, 1, d), lambda i: (i // tps, 0, 0))
    full = lambda a: pl.BlockSpec(a.shape, lambda i: (0,) * a.ndim)
    s5_spec = pl.BlockSpec((s5.shape[0], tm // S5_CHUNK, s5.shape[2]), lambda i: (0, i, 0))
    return pl.pallas_call(
        _outproj_kernel,
        grid=(r // tm,),
        in_specs=[row(x2), seg] + [row(a) for a in (yf, yb, z, ga, wa)] + [s5_spec, row(u)]
                 + [full(a) for a in (ng, d5, wglu, bglu, wout)],
        out_specs=row(x2),
        out_shape=jax.ShapeDtypeStruct(x2.shape, F32),
        scratch_shapes=[pltpu.VMEM((D_S5 // LANES, tm, LANES), F32)],
        compiler_params=_params(("arbitrary",)),
        name="outproj",
    )(x2, gate, yf, yb, z, ga, wa, s5, u, ng, d5, wglu, bglu, wout)


def _ffn_kernel(x_ref, xp_ref, xn_ref, sh_ref, sc_ref, gt_ref, g_ref, wgu_ref, cw_ref, cb_ref,
                wd_ref, gfin_ref, o_ref, *, tiles_per_seq, d_ff, final_norm):
    i = pl.program_id(0)
    tm = x_ref.shape[0]
    x = x_ref[...]
    g, sh, sc = g_ref[...], sh_ref[0], sc_ref[0]
    h = _rms_mod(x, g, sh, sc)
    he = jnp.concatenate([_rms_mod(xp_ref[...], g, sh, sc), h, _rms_mod(xn_ref[...], g, sh, sc)], axis=0)
    ge = _dot(he.astype(BF16), wgu_ref[:, :d_ff])
    upj = _dot(h.astype(BF16), wgu_ref[:, d_ff:])
    t = i % tiles_per_seq
    row = lax.broadcasted_iota(jnp.int32, (tm, 1), 0)
    up = jnp.where((row == 0) & (t == 0), 0.0, ge[HALO - 1:HALO - 1 + tm])
    dn = jnp.where((row == tm - 1) & (t == tiles_per_seq - 1), 0.0, ge[HALO + 1:HALO + 1 + tm])
    cw = cw_ref[...]
    conv = cw[0:1] * up + cw[1:2] * ge[HALO:HALO + tm] + cw[2:3] * dn + cb_ref[...]
    act = (_silu(conv) * upj).astype(BF16)
    y = x + gt_ref[0] * _dot(act, wd_ref[...])
    if final_norm:
        y = y * lax.rsqrt(jnp.mean(y * y, axis=-1, keepdims=True) + NORM_EPS) * gfin_ref[...]
    o_ref[...] = y


def _ffn_call(x2, shift, scale, gate, g, wgu, cw, cb, wd, gfin, *, tm, rows_per_seg, seq, final_norm):
    r, d = x2.shape
    d_ff = wd.shape[0]
    tps = rows_per_seg // tm
    hb = tm // HALO
    last = r // HALO - 1
    seg = pl.BlockSpec((1, 1, d), lambda i: (i // tps, 0, 0))
    full = lambda a: pl.BlockSpec(a.shape, lambda i: (0,) * a.ndim)
    once = lambda a: pl.BlockSpec(a.shape, lambda i: (0,) * a.ndim, pipeline_mode=pl.Buffered(1))
    return pl.pallas_call(
        functools.partial(_ffn_kernel, tiles_per_seq=seq // tm, d_ff=d_ff, final_norm=final_norm),
        grid=(r // tm,),
        in_specs=[pl.BlockSpec((tm, d), lambda i: (i, 0)),
                  pl.BlockSpec((HALO, d), lambda i: (jnp.maximum(i * hb - 1, 0), 0)),
                  pl.BlockSpec((HALO, d), lambda i: (jnp.minimum(i * hb + hb, last), 0)),
                  seg, seg, seg, full(g), once(wgu), full(cw), full(cb), once(wd), full(gfin)],
        out_specs=pl.BlockSpec((tm, d), lambda i: (i, 0)),
        out_shape=jax.ShapeDtypeStruct(x2.shape, F32),
        compiler_params=_params(("arbitrary",)),
        name="conv_ffn",
    )(x2, x2, x2, shift, scale, gate, g, wgu, cw, cb, wd, gfin)


def _rope_tables(seq):
    rows = seq // GRID_W
    r = jnp.broadcast_to(jnp.arange(rows, dtype=F32)[:, None], (rows, GRID_W)).reshape(-1)
    c = jnp.broadcast_to(jnp.arange(GRID_W, dtype=F32)[None, :], (rows, GRID_W)).reshape(-1)
    n_freq = HEAD_DIM // 4
    inv = ROPE_THETA ** (-jnp.arange(n_freq, dtype=F32) / n_freq)
    ang = jnp.concatenate([r[:, None] * inv, c[:, None] * inv], axis=-1)
    cos, sin, zero = jnp.cos(ang), jnp.sin(ang), jnp.zeros_like(ang)
    reps = LANES // HEAD_DIM
    return (jnp.tile(jnp.concatenate([cos, cos], axis=-1), (1, reps)),
            jnp.tile(jnp.concatenate([-sin, zero], axis=-1), (1, reps)),
            jnp.tile(jnp.concatenate([zero, sin], axis=-1), (1, reps)))


def _pad_w_in(w):
    offs = [0]
    for s in IN_SIZES:
        offs.append(offs[-1] + s)
    parts = [w[:, offs[j]:offs[j + 1]] for j in range(len(IN_SIZES))]
    parts[2] = jnp.pad(parts[2], ((0, 0), (0, LANES - IN_SIZES[2])))
    return jnp.concatenate(parts, axis=1).astype(BF16)


def kernel(x, c, ctx, c_ctx, w_mod, b_mod, g_mix, w_in, ssd_conv_w, ssd_conv_b, ssd_a_log, ssd_dt_bias, ssd_d, ssd_norm_g, ga_q_norm, ga_k_norm, wa_sink, s5_lambda_re, s5_lambda_im, s5_log_step, s5_b_re, s5_b_im, s5_c_re, s5_c_im, s5_d, s5_w_glu, s5_b_glu, w_out, g_ffn, w_gate, w_up, ffn_conv_w, ffn_conv_b, w_down, g_final):
    b, s, d = x.shape
    lc = ctx.shape[1]
    depth = w_mod.shape[0]
    hpg = 2
    tm = 512

    cc = jnp.zeros((8, d), F32).at[:b].set(c).at[b].set(c_ctx)
    mod = _mod_call(cc, w_mod, b_mod).reshape(depth, 8, 6, d)

    tabs_l = _rope_tables(s)
    one, zero = jnp.ones((lc, LANES), F32), jnp.zeros((lc, LANES), F32)
    tabs_c = (one, zero, zero)
    bd = jnp.kron(jnp.eye(LANES // HEAD_DIM, dtype=F32),
                  jnp.full((HEAD_DIM, HEAD_DIM), 1.0 / HEAD_DIM, F32)).astype(BF16)
    gfin = g_final.reshape(1, d)

    x2 = x.reshape(b * s, d)
    xc2 = ctx.reshape(b * lc, d)
    for i in range(depth):
        need_ctx = i < depth - 1
        ml = lambda j: mod[i, :b, j].reshape(b, 1, d)
        mc = lambda j: mod[i, b, j].reshape(1, 1, d)
        g_i = g_mix[i].reshape(1, d)
        w_pad = _pad_w_in(w_in[i])
        qn = jnp.tile(ga_q_norm[i], LANES // HEAD_DIM).reshape(1, LANES)
        kn = jnp.tile(ga_k_norm[i], LANES // HEAD_DIM).reshape(1, LANES)
        conv_w = ssd_conv_w[i]
        conv_b = ssd_conv_b[i].reshape(1, -1)
        pl_ = _inproj_call(x2, ml(0), ml(1), g_i, w_pad, tabs_l, qn, kn, bd, conv_w, conv_b,
                           tm=tm, rows_per_seg=s, seq=s)
        pc_ = _inproj_call(xc2, mc(0), mc(1), g_i, w_pad, tabs_c, qn, kn, bd, conv_w, conv_b,
                           tm=lc, rows_per_seg=b * lc, seq=lc)
        zl, xsl, btl, cml, dtl, gql, gkl, gvl, wql, wkl, wvl, ul, u5l = pl_
        zc, xsc, btc, cmc, dtc, gqc, gkc, gvc, wqc, wkc, wvc, uc, u5c = pc_

        alog_c = ssd_a_log[i].reshape(-1, 1)
        bias_c = ssd_dt_bias[i].reshape(-1, 1)
        dsk = jnp.repeat(ssd_d[i], HEAD_DIM).reshape(1, D_SSD)
        n_dt = N_DIRS * SSD_HEADS

        def ssd(xs2, bt, cm2, dt2, h0, n):
            dt_t = dt2.reshape(b, n, LANES)[:, :, :n_dt].transpose(0, 2, 1)
            return _ssd_call(xs2.reshape(b, n, -1), bt, cm2.reshape(b, n, -1), dt_t, h0, alog_c, bias_c, dsk)

        h0 = jnp.zeros((b, N_DIRS, SSD_GROUPS, SSD_STATE, hpg * HEAD_DIM), F32)
        yfc, ybc, hc = ssd(xsc, btc, cmc, dtc, h0, lc)
        yfl, ybl, _ = ssd(xsl, btl, cml, dtl, hc, s)

        kx_ga = gkc.reshape(b, lc, -1)
        y_ga = _flash_call(gql, gkl.reshape(b, s, -1), gvl, None, extra=(kx_ga, gvc))

        kx_wa = wkc.reshape(b, lc, -1)
        sink = wa_sink[i].astype(F32)
        y_wa = _win_call(wql, wkl.reshape(b, s, -1), wvl, kx_wa, wvc, sink)

        gm = lambda a: jnp.moveaxis(a, 1, 0)
        lre = gm(s5_lambda_re[i])[:, :, None, :]
        lim = gm(s5_lambda_im[i])[:, :, None, :]
        ls = gm(s5_log_step[i])[:, :, None, None]
        bre = gm(s5_b_re[i]).transpose(0, 1, 3, 2)
        bim = gm(s5_b_im[i]).transpose(0, 1, 3, 2)
        cre, cim = gm(s5_c_re[i]), gm(s5_c_im[i])
        y5l, y5c = _s5_call(u5l, u5c, lre, lim, ls, bre, bim, cre, cim, nb=b)

        ng = ssd_norm_g[i].reshape(1, D_SSD)
        d5 = s5_d[i].reshape(1, D_S5)
        wglu = s5_w_glu[i].astype(BF16)
        bglu = s5_b_glu[i].reshape(1, -1)
        wout = w_out[i].astype(BF16)
        x2 = _outproj_call(x2, ml(2), yfl.reshape(b * s, -1), ybl.reshape(b * s, -1), zl, y_ga, y_wa,
                           y5l, ul, ng, d5, wglu, bglu, wout, tm=tm, rows_per_seg=s)

        gf_i = g_ffn[i].reshape(1, d)
        wgu = jnp.concatenate([w_gate[i], w_up[i]], axis=1).astype(BF16)
        cw = ffn_conv_w[i]
        cb = ffn_conv_b[i].reshape(1, -1)
        wd = w_down[i].astype(BF16)
        x2 = _ffn_call(x2, ml(3), ml(4), ml(5), gf_i, wgu, cw, cb, wd, gfin, tm=512, rows_per_seg=s,
                       seq=s, final_norm=not need_ctx)

        if need_ctx:
            yc_ga = _flash_call(gqc, kx_ga, gvc, None)
            yc_wa = _flash_call(wqc, kx_wa, wvc, sink)
            xc2 = _outproj_call(xc2, mc(2), yfc.reshape(b * lc, -1), ybc.reshape(b * lc, -1), zc, yc_ga,
                                yc_wa, y5c, uc, ng, d5, wglu, bglu, wout, tm=lc, rows_per_seg=b * lc)
            xc2 = _ffn_call(xc2, mc(3), mc(4), mc(5), gf_i, wgu, cw, cb, wd, gfin, tm=lc,
                            rows_per_seg=b * lc, seq=lc, final_norm=False)
    return x2.reshape(b, s, d)
```

```python
import functools

import jax
import jax.numpy as jnp
from jax import lax
from jax.experimental import pallas as pl
from jax.experimental.pallas import tpu as pltpu

F32 = jnp.float32
BF16 = jnp.bfloat16
HIGHEST = lax.Precision.HIGHEST

HEAD_DIM = 64
GRID_W = 64
ROPE_THETA = 10000.0
NORM_EPS = 1e-6
WINDOW = 128
N_DIRS = 2
SSD_HEADS = 4
SSD_GROUPS = 2
SSD_STATE = 128
SSD_CHUNK = 128
SSD_CHUNKS_PER_STEP = 4
D_SSD = SSD_HEADS * HEAD_DIM
SSD_XBC = D_SSD + 2 * SSD_GROUPS * SSD_STATE
S5_GROUPS = 16
S5_GROUP_CH = 16
S5_STATE = 64
S5_MAX_RE = -1e-4
S5_CHUNK = 32
D_S5 = S5_GROUPS * S5_GROUP_CH
LANES = 128
HALO = 8
NEG_BIG = -1e30
LOG2E = 1.4426950408889634
Q_HEADS = 4
KV_HEADS = 2
HPG = Q_HEADS // KV_HEADS
ONES_ROWS = 16
FLASH_SUB_K = 256
FLASH_SUB_Q = 256
FLASH_DEPTH = 6
VMEM_LIMIT = 52 * 1024 * 1024

IN_SIZES = (D_SSD, SSD_XBC, N_DIRS * SSD_HEADS, 256, 128, 128, 256, 128, 128, D_S5)
P_Z, P_XBC, P_DT, P_GQ, P_GK, P_GV, P_WQ, P_WK, P_WV, P_U, P_END = (
    0, 256, 1024, 1152, 1408, 1536, 1664, 1920, 2048, 2176, 2432)


def _params(sem=None, flags=None):
    return pltpu.CompilerParams(dimension_semantics=sem, vmem_limit_bytes=VMEM_LIMIT, flags=flags)


def _silu(v):
    return v * jax.nn.sigmoid(v)


def _softplus(v):
    return jnp.maximum(v, 0.0) + jnp.log1p(jnp.exp(-jnp.abs(v)))


def _dot(a, b):
    return jnp.dot(a, b, preferred_element_type=F32)


def _dot_nt(a, b, precision=None):
    return lax.dot_general(a, b, (((1,), (1,)), ((), ())), preferred_element_type=F32,
                           precision=precision)


def _dot_tn(a, b):
    return lax.dot_general(a, b, (((0,), (0,)), ((), ())), preferred_element_type=F32)


def _mod_kernel(cc_ref, w_ref, b_ref, o_ref):
    s = _silu(cc_ref[...])
    o_ref[0] = jnp.dot(s, w_ref[0], preferred_element_type=F32, precision=HIGHEST) + b_ref[0]


def _mod_call(cc, w_mod, b_mod):
    n_layers, d, n = w_mod.shape
    tn = 1536
    return pl.pallas_call(
        _mod_kernel,
        grid=(n_layers, n // tn),
        in_specs=[pl.BlockSpec((8, d), lambda l, j: (0, 0)),
                  pl.BlockSpec((1, d, tn), lambda l, j: (l, 0, j)),
                  pl.BlockSpec((1, 1, tn), lambda l, j: (l, 0, j))],
        out_specs=pl.BlockSpec((1, 8, tn), lambda l, j: (l, 0, j)),
        out_shape=jax.ShapeDtypeStruct((n_layers, 8, n), F32),
        compiler_params=_params(("arbitrary", "arbitrary")),
        name="adaln_mod",
    )(cc, w_mod, b_mod.reshape(n_layers, 1, n))


def _rms_mod(x, g, shift, scale):
    y = x * lax.rsqrt(jnp.mean(x * x, axis=-1, keepdims=True) + NORM_EPS) * g
    return y * (1.0 + scale) + shift


def _rope(t, cos, sna, snb):
    return t * cos + pltpu.roll(t, 96, 1) * sna + pltpu.roll(t, 32, 1) * snb


def _head_rms(t, gain, bd):
    t2 = t * t
    hi = t2.astype(BF16)
    lo = (t2 - hi.astype(F32)).astype(BF16)
    ms = _dot(hi, bd) + _dot(lo, bd)
    return t * lax.rsqrt(ms + NORM_EPS) * gain


def _conv3_silu(v, prev, nxt, cw, cb):
    n = v.shape[0]
    row = lax.broadcasted_iota(jnp.int32, (n, 1), 0)
    up = jnp.where(row == 0, prev, pltpu.roll(v, 1, 0))
    dn = jnp.where(row == n - 1, nxt, pltpu.roll(v, n - 1, 0))
    return _silu(cw[0:1] * up + cw[1:2] * v + cw[2:3] * dn + cb)


def _inproj_kernel(x_ref, xp_ref, xn_ref, sh_ref, sc_ref, g_ref, w_ref, cos_ref, sna_ref, snb_ref,
                   qn_ref, kn_ref, bd_ref, cw_ref, cb_ref, z_ref, xs_ref, bt_ref, cm_ref, dt_ref, gq_ref, gk_ref,
                   gv_ref, wq_ref, wk_ref, wv_ref, u_ref, u5_ref, u_sc, *, tiles_per_seq):
    g, sh, sc = g_ref[...], sh_ref[0], sc_ref[0]
    hb = _rms_mod(x_ref[...], g, sh, sc).astype(BF16)
    halo = jnp.concatenate([xp_ref[...], xn_ref[...]], axis=0)
    u = _dot(hb, w_ref[:, P_U:P_END])
    p = _dot(hb, w_ref[:, P_Z:P_GQ])
    ph = _dot(_rms_mod(halo, g, sh, sc).astype(BF16), w_ref[:, P_XBC:P_DT])
    pg = _dot(hb, w_ref[:, P_GQ:P_WQ])
    pw = _dot(hb, w_ref[:, P_WQ:P_U])
    u_ref[...] = u
    gpl = LANES // S5_GROUP_CH
    for half in range(D_S5 // LANES):
        u_sc[half] = u[:, half * LANES:(half + 1) * LANES]
    for s in range(S5_CHUNK):
        for half in range(D_S5 // LANES):
            rows = u_sc[half, pl.ds(s, u_sc.shape[1] // S5_CHUNK, stride=S5_CHUNK), :]
            for j in range(gpl):
                u5_ref[half * gpl + j, :, s * S5_GROUP_CH:(s + 1) * S5_GROUP_CH] = (
                    rows[:, j * S5_GROUP_CH:(j + 1) * S5_GROUP_CH])
    cos, sna, snb = cos_ref[...], sna_ref[...], snb_ref[...]
    bd = bd_ref[...]
    q_scale = HEAD_DIM ** -0.5 * LOG2E
    z_ref[...] = p[:, P_Z:P_XBC]
    t = pl.program_id(0) % tiles_per_seq
    prev = jnp.where(t > 0, ph[HALO - 1:HALO, :], 0.0)
    nxt = jnp.where(t < tiles_per_seq - 1, ph[HALO:HALO + 1, :], 0.0)
    act = _conv3_silu(p[:, P_XBC:P_DT], prev, nxt, cw_ref[...], cb_ref[...])
    xs_ref[...] = act[:, :D_SSD]
    n_bc = SSD_GROUPS * SSD_STATE
    bt_ref[0] = jnp.transpose(act[:, D_SSD:D_SSD + n_bc]).astype(BF16)
    cm_ref[...] = act[:, D_SSD + n_bc:].astype(BF16)
    dt_ref[...] = p[:, P_DT:P_GQ]
    for j in range(2):
        q = _rope(_head_rms(pg[:, j * LANES:(j + 1) * LANES], qn_ref[...], bd), cos, sna, snb) * q_scale
        gq_ref[:, j * LANES:(j + 1) * LANES] = q.astype(BF16)
    gk_ref[...] = _rope(_head_rms(pg[:, P_GK - P_GQ:P_GV - P_GQ], kn_ref[...], bd), cos, sna, snb).astype(BF16)
    gv_ref[0] = jnp.transpose(pg[:, P_GV - P_GQ:]).astype(BF16)
    for j in range(2):
        q = _rope(pw[:, j * LANES:(j + 1) * LANES], cos, sna, snb) * q_scale
        wq_ref[:, j * LANES:(j + 1) * LANES] = q.astype(BF16)
    wk_ref[...] = _rope(pw[:, P_WK - P_WQ:P_WV - P_WQ], cos, sna, snb).astype(BF16)
    wv_ref[0] = jnp.transpose(pw[:, P_WV - P_WQ:]).astype(BF16)


def _inproj_call(x2, shift, scale, g, w_pad, tabs, qn, kn, bd, cw, cb, *, tm, rows_per_seg, seq):
    r, d = x2.shape
    tps = rows_per_seg // tm
    tpq = seq // tm
    hb = tm // HALO
    last = r // HALO - 1
    row = lambda w: pl.BlockSpec((tm, w), lambda i: (i, 0))
    seg = pl.BlockSpec((1, 1, d), lambda i: (i // tps, 0, 0))
    full = lambda a: pl.BlockSpec(a.shape, lambda i: (0,) * a.ndim)
    tab = pl.BlockSpec((tm, LANES), lambda i: (i % tpq, 0))
    n_bc = SSD_GROUPS * SSD_STATE
    widths = (256, D_SSD, -n_bc, n_bc, LANES, 256, 128, -LANES, 256, 128, -LANES, D_S5)
    dtypes = (F32, F32, BF16, BF16, F32, BF16, BF16, BF16, BF16, BF16, BF16, F32)
    spec = lambda w: row(w) if w > 0 else pl.BlockSpec((1, -w, tm), lambda i: (i // tpq, 0, i % tpq))
    shape = lambda w: (r, w) if w > 0 else (r // seq, -w, seq)
    w5 = S5_CHUNK * S5_GROUP_CH
    u5_spec = pl.BlockSpec((S5_GROUPS, tm // S5_CHUNK, w5), lambda i: (0, i, 0))
    return pl.pallas_call(
        functools.partial(_inproj_kernel, tiles_per_seq=tpq),
        grid=(r // tm,),
        in_specs=[row(d),
                  pl.BlockSpec((HALO, d), lambda i: (jnp.maximum(i * hb - 1, 0), 0)),
                  pl.BlockSpec((HALO, d), lambda i: (jnp.minimum(i * hb + hb, last), 0)),
                  seg, seg, full(g), full(w_pad), tab, tab, tab, full(qn), full(kn), full(bd),
                  full(cw), full(cb)],
        out_specs=[spec(w) for w in widths] + [u5_spec],
        out_shape=[jax.ShapeDtypeStruct(shape(w), t) for w, t in zip(widths, dtypes)]
                  + [jax.ShapeDtypeStruct((S5_GROUPS, r // S5_CHUNK, w5), F32)],
        scratch_shapes=[pltpu.VMEM((D_S5 // LANES, tm, LANES), F32)],
        compiler_params=_params(("arbitrary",)),
        name="inproj",
    )(x2, x2, x2, shift, scale, g, w_pad, *tabs, qn, kn, bd, cw, cb)


def _ssd_kernel(xsf_ref, btf_ref, cmf_ref, xsb_ref, btb_ref, cmb_ref, dtrf_ref, dtrb_ref, h0_ref, alc_ref,
                bic_ref, dsk_ref, yf_ref, yb_ref, hout_ref, h_sc, *, nc):
    c = pl.program_id(1)

    @pl.when(c == 0)
    def _():
        h_sc[...] = h0_ref[0]

    q = SSD_CHUNK
    nsub = xsf_ref.shape[1] // q
    hpg = SSD_HEADS // SSD_GROUPS
    xs_refs, dtr_refs, y_refs = (xsf_ref, xsb_ref), (dtrf_ref, dtrb_ref), (yf_ref, yb_ref)
    bt_refs, cm_refs = (btf_ref, btb_ref), (cmf_ref, cmb_ref)
    order = (tuple(range(nsub)), tuple(reversed(range(nsub))))
    tok = lambda j: slice(j * q, (j + 1) * q)
    bmat_t = lambda d, j, g: bt_refs[d][0, g * SSD_STATE:(g + 1) * SSD_STATE, tok(j)]
    cmat = lambda d, j, g: cm_refs[d][0, tok(j), g * SSD_STATE:(g + 1) * SSD_STATE]
    ri = lax.broadcasted_iota(jnp.int32, (q, q), 0)
    ci = lax.broadcasted_iota(jnp.int32, (q, q), 1)
    mask = (ri >= ci, ri <= ci)
    lane = lax.broadcasted_iota(jnp.int32, (1, LANES), 1)
    first = lane < HEAD_DIM
    dgs = [(d, g) for d in range(N_DIRS) for g in range(SSD_GROUPS)]
    items = [(d, j, g) for j in range(nsub) for d, g in dgs]

    h = {dg: h_sc[dg[0], dg[1]] for dg in dgs}
    yoff0 = {(d, g): _dot(cmat(d, order[d][0], g), h[d, g].astype(BF16)) for d, g in dgs}
    cb = {(d, j, g): _dot(cmat(d, j, g), bmat_t(d, j, g)) for d, j, g in items}

    dt_r, acs_c, acs_r, dt_c, tot = {}, {}, {}, {}, {}
    for j in range(nsub):
        for d in range(N_DIRS):
            dt_r[d, j] = _softplus(dtr_refs[d][0, :, tok(j)] + bic_ref[...])
    for j in range(nsub):
        for d in range(N_DIRS):
            dta_r = dt_r[d, j] * (-jnp.exp(alc_ref[...]))
            maskf = mask[d].astype(F32)
            pad = jnp.zeros((LANES - dta_r.shape[0], q), F32)
            acs_c[d, j] = _dot_nt(maskf, jnp.concatenate([dta_r, pad], axis=0), precision=HIGHEST)
            acs_r[d, j] = _dot_nt(dta_r, maskf, precision=HIGHEST)
            dt_c[d, j] = jnp.transpose(jnp.concatenate([dt_r[d, j], pad], axis=0))
            tot[d, j] = acs_c[d, j][q - 1:q] if d == 0 else acs_c[d, j][0:1]

    ydiag, acs_g, tot_g, st = {}, {}, {}, {}
    for d, j, g in items:
        e0 = d * SSD_HEADS + g * hpg
        pick = lambda v: jnp.where(first, v[:, e0:e0 + 1], v[:, e0 + 1:e0 + 2])
        acs_g[d, j, g], tot_g[d, j, g] = pick(acs_c[d, j]), pick(tot[d, j])
        xs_g = xs_refs[d][0, tok(j), g * LANES:(g + 1) * LANES]
        xdt = xs_g * pick(dt_c[d, j])
        xdt_b = xdt.astype(BF16)
        yd = []
        for a in range(hpg):
            e = e0 + a
            dec = jnp.where(mask[d], jnp.exp(acs_c[d, j][:, e:e + 1] - acs_r[d, j][e:e + 1, :]), 0.0)
            yd.append(_dot((cb[d, j, g] * dec).astype(BF16), xdt_b))
        y = jnp.where(first, yd[0], yd[1])
        if d == 0:
            y = y + dsk_ref[:, g * LANES:(g + 1) * LANES] * xs_g
        ydiag[d, j, g] = y
        w = (xdt * jnp.exp(tot_g[d, j, g] - acs_g[d, j, g])).astype(BF16)
        st[d, j, g] = _dot(bmat_t(d, j, g), w)

    for idx in range(nsub):
        for d, g in dgs:
            j = order[d][idx]
            yoff = yoff0[d, g] if idx == 0 else _dot(cmat(d, j, g), h[d, g].astype(BF16))
            y_refs[d][0, tok(j), g * LANES:(g + 1) * LANES] = ydiag[d, j, g] + yoff * jnp.exp(acs_g[d, j, g])
            h[d, g] = h[d, g] * jnp.exp(tot_g[d, j, g]) + st[d, j, g]
    for d, g in dgs:
        h_sc[d, g] = h[d, g]

    @pl.when(c == nc - 1)
    def _():
        hout_ref[0] = h_sc[...]


def _ssd_call(xs, bt, cm, dt_t, h0, alog_c, bias_c, dsk):
    b, s, _ = xs.shape
    q = SSD_CHUNK * min(SSD_CHUNKS_PER_STEP, s // SSD_CHUNK)
    nc = s // q
    fwd = lambda c: c
    bwd = lambda c: nc - 1 - c
    rows = lambda a, f: pl.BlockSpec((1, q, a.shape[2]), lambda i, c: (i, f(c), 0))
    cols = lambda a, f: pl.BlockSpec((1, a.shape[1], q), lambda i, c: (i, 0, f(c)))
    drspec = lambda f: cols(dt_t, f)
    full = lambda a: pl.BlockSpec(a.shape, lambda i, c: (0,) * a.ndim)
    hspec = pl.BlockSpec((1,) + h0.shape[1:], lambda i, c: (i, 0, 0, 0, 0))
    return pl.pallas_call(
        functools.partial(_ssd_kernel, nc=nc),
        grid=(b, nc),
        in_specs=[rows(xs, fwd), cols(bt, fwd), rows(cm, fwd), rows(xs, bwd), cols(bt, bwd), rows(cm, bwd),
                  drspec(fwd), drspec(bwd), hspec, full(alog_c), full(bias_c), full(dsk)],
        out_specs=[rows(xs, fwd), rows(xs, bwd), hspec],
        out_shape=[jax.ShapeDtypeStruct(xs.shape, F32), jax.ShapeDtypeStruct(xs.shape, F32),
                   jax.ShapeDtypeStruct(h0.shape, F32)],
        scratch_shapes=[pltpu.VMEM(h0.shape[1:], F32)],
        compiler_params=_params(("arbitrary", "arbitrary")),
        name="ssd",
    )(xs, bt, cm, xs, bt, cm, dt_t, dt_t, h0, alog_c, bias_c, dsk)


def _q_transposed(q):
    qt = jnp.transpose(q.astype(F32))
    zero = jnp.zeros((HEAD_DIM, q.shape[0]), BF16)
    out = []
    for h in range(Q_HEADS):
        blk = qt[h * HEAD_DIM:(h + 1) * HEAD_DIM].astype(BF16)
        out.append(jnp.concatenate([blk, zero] if h // HPG == 0 else [zero, blk], axis=0))
    return out


def _flash_tile(k_ref, vt_ref, qt_sc, m_sc, acc_sc):
    tq = qt_sc.shape[2]
    tk = k_ref.shape[1]
    ones = jnp.ones((ONES_ROWS, FLASH_SUB_K), BF16)
    chains = [(h, slice(jq * FLASH_SUB_Q, (jq + 1) * FLASH_SUB_Q))
              for h in range(Q_HEADS) for jq in range(tq // FLASH_SUB_Q)]
    m = [m_sc[h, :, cs] for h, cs in chains]
    acc = [acc_sc[h, :, cs] for h, cs in chains]
    units = [(jk, c) for jk in range(tk // FLASH_SUB_K) for c in range(len(chains))]

    def scores(u):
        jk, c = u
        h, cs = chains[c]
        return _dot(k_ref[0, jk * FLASH_SUB_K:(jk + 1) * FLASH_SUB_K, :], qt_sc[h, :, cs])

    def finish(u, s):
        jk, c = u
        g = chains[c][0] // HPG
        vte = jnp.concatenate(
            [vt_ref[0, g * HEAD_DIM:(g + 1) * HEAD_DIM, jk * FLASH_SUB_K:(jk + 1) * FLASH_SUB_K], ones], axis=0)
        m_new = jnp.maximum(m[c], jnp.max(s, axis=0, keepdims=True))
        p = jnp.exp2(s - m_new).astype(BF16)
        acc[c] = jnp.exp2(m[c] - m_new) * acc[c] + _dot(vte, p)
        m[c] = m_new

    pending = [scores(u) for u in units[:FLASH_DEPTH]]
    for i, u in enumerate(units):
        s_cur = pending.pop(0)
        if i + FLASH_DEPTH < len(units):
            pending.append(scores(units[i + FLASH_DEPTH]))
        finish(u, s_cur)
    for c, (h, cs) in enumerate(chains):
        m_sc[h, :, cs] = m[c]
        acc_sc[h, :, cs] = acc[c]


def _flash_kernel(sink_ref, q_ref, k_ref, vt_ref, *rest, nk, has_sink, has_extra):
    if has_extra:
        kx_ref, vtx_ref, o_ref, qt_sc, m_sc, acc_sc = rest
    else:
        o_ref, qt_sc, m_sc, acc_sc = rest
    ki = pl.program_id(2)
    tq = q_ref.shape[0]

    @pl.when(ki == 0)
    def _():
        qts = _q_transposed(q_ref[...])
        for h in range(Q_HEADS):
            qt_sc[h] = qts[h]
            if has_sink:
                m_sc[h] = jnp.full(m_sc.shape[1:], sink_ref[h] * LOG2E, F32)
                acc_sc[h] = jnp.concatenate([jnp.zeros((HEAD_DIM, tq), F32), jnp.ones((ONES_ROWS, tq), F32)], axis=0)
            else:
                m_sc[h] = jnp.full(m_sc.shape[1:], NEG_BIG, F32)
                acc_sc[h] = jnp.zeros(acc_sc.shape[1:], F32)
        if has_extra:
            _flash_tile(kx_ref, vtx_ref, qt_sc, m_sc, acc_sc)

    _flash_tile(k_ref, vt_ref, qt_sc, m_sc, acc_sc)

    @pl.when(ki == nk - 1)
    def _():
        outs = []
        for h in range(Q_HEADS):
            acc = acc_sc[h]
            outs.append(acc[:HEAD_DIM] / acc[HEAD_DIM:HEAD_DIM + 1])
        o_ref[...] = jnp.transpose(jnp.concatenate(outs, axis=0)).astype(o_ref.dtype)


def _pick(n, cands):
    for c in cands:
        if n % c == 0:
            return c
    return n


def _flash_call(q2, k, vt, sink, extra=None):
    b, sk, kw = k.shape
    sq = q2.shape[0] // b
    tq = _pick(sq, (512, 256))
    tk = _pick(sk, (4096, 1024, 512, 256))
    nq = sq // tq
    nk = sk // tk
    has_sink = sink is not None
    if sink is None:
        sink = jnp.zeros((Q_HEADS,), F32)
    in_specs = [pl.BlockSpec(memory_space=pltpu.SMEM),
                pl.BlockSpec((tq, q2.shape[1]), lambda i, qi, ki: (i * nq + qi, 0)),
                pl.BlockSpec((1, tk, kw), lambda i, qi, ki: (i, ki, 0)),
                pl.BlockSpec((1, kw, tk), lambda i, qi, ki: (i, 0, ki))]
    args = [sink, q2, k, vt]
    if extra is not None:
        lx = extra[0].shape[1]
        in_specs += [pl.BlockSpec((1, lx, kw), lambda i, qi, ki: (i, 0, 0)),
                     pl.BlockSpec((1, kw, lx), lambda i, qi, ki: (i, 0, 0))]
        args += list(extra)
    return pl.pallas_call(
        functools.partial(_flash_kernel, nk=nk, has_sink=has_sink, has_extra=extra is not None),
        grid=(b, nq, nk),
        in_specs=in_specs,
        out_specs=pl.BlockSpec((tq, q2.shape[1]), lambda i, qi, ki: (i * nq + qi, 0)),
        out_shape=jax.ShapeDtypeStruct(q2.shape, BF16),
        scratch_shapes=[pltpu.VMEM((Q_HEADS, kw, tq), BF16), pltpu.VMEM((Q_HEADS, 1, tq), F32),
                        pltpu.VMEM((Q_HEADS, HEAD_DIM + ONES_ROWS, tq), F32)],
        compiler_params=_params(("arbitrary", "arbitrary", "arbitrary")),
        name="flash_sink" if has_sink else "flash",
    )(*args)


def _win_kernel(sink_ref, q_ref, kp_ref, kc_ref, kn_ref, kx_ref, vtp_ref, vtc_ref, vtn_ref, vtx_ref,
                o_ref, *, nq):
    j = pl.program_id(1)
    tq = q_ref.shape[0]
    blk = kp_ref.shape[1]
    rc = lax.broadcasted_iota(jnp.int32, (tq, tq), 0) - lax.broadcasted_iota(jnp.int32, (tq, tq), 1)
    m_cur = (rc <= WINDOW) & (rc >= -WINDOW)
    rp = lax.broadcasted_iota(jnp.int32, (blk, tq), 0) - lax.broadcasted_iota(jnp.int32, (blk, tq), 1)
    m_prev = (rp - blk >= -WINDOW) & (j > 0)
    m_next = (rp + tq <= WINDOW) & (j < nq - 1)
    qts = _q_transposed(q_ref[...])
    raw = [[_dot(kr[0], qt) for kr in (kc_ref, kp_ref, kn_ref, kx_ref)] for qt in qts]
    outs = []
    for h in range(Q_HEADS):
        rows = slice((h // HPG) * HEAD_DIM, (h // HPG + 1) * HEAD_DIM)
        sink = sink_ref[h] * LOG2E
        s_c = jnp.where(m_cur, raw[h][0], NEG_BIG)
        s_p = jnp.where(m_prev, raw[h][1], NEG_BIG)
        s_n = jnp.where(m_next, raw[h][2], NEG_BIG)
        s_x = raw[h][3]
        cmax = lambda v: jnp.max(v, axis=0, keepdims=True)
        m = jnp.maximum(jnp.maximum(cmax(s_c), cmax(s_p)), jnp.maximum(cmax(s_n), cmax(s_x)))
        m = jnp.maximum(m, sink)
        p_c, p_p, p_n, p_x = (jnp.exp2(v - m) for v in (s_c, s_p, s_n, s_x))
        csum = lambda v: jnp.sum(v, axis=0, keepdims=True)
        l = csum(p_c) + csum(p_p) + csum(p_n) + csum(p_x) + jnp.exp2(sink - m)
        o = (_dot(vtc_ref[0, rows, :], p_c.astype(BF16)) + _dot(vtp_ref[0, rows, :], p_p.astype(BF16))
             + _dot(vtn_ref[0, rows, :], p_n.astype(BF16)) + _dot(vtx_ref[0, rows, :], p_x.astype(BF16)))
        outs.append(o / l)
    o_ref[...] = jnp.transpose(jnp.concatenate(outs, axis=0)).astype(o_ref.dtype)


def _win_call(q2, k, vt, kx, vtx, sink):
    b, s, kw = k.shape
    lc = kx.shape[1]
    blk = WINDOW
    tq = _pick(s, (512, 256, 128))
    nq = s // tq
    r = tq // blk
    nblk = s // blk
    prev = lambda t: jnp.maximum(t * r - 1, 0)
    nxt = lambda t: jnp.minimum(t * r + r, nblk - 1)
    return pl.pallas_call(
        functools.partial(_win_kernel, nq=nq),
        grid=(b, nq),
        in_specs=[pl.BlockSpec(memory_space=pltpu.SMEM),
                  pl.BlockSpec((tq, q2.shape[1]), lambda i, t: (i * nq + t, 0)),
                  pl.BlockSpec((1, blk, kw), lambda i, t: (i, prev(t), 0)),
                  pl.BlockSpec((1, tq, kw), lambda i, t: (i, t, 0)),
                  pl.BlockSpec((1, blk, kw), lambda i, t: (i, nxt(t), 0)),
                  pl.BlockSpec((1, lc, kw), lambda i, t: (i, 0, 0)),
                  pl.BlockSpec((1, kw, blk), lambda i, t: (i, 0, prev(t))),
                  pl.BlockSpec((1, kw, tq), lambda i, t: (i, 0, t)),
                  pl.BlockSpec((1, kw, blk), lambda i, t: (i, 0, nxt(t))),
                  pl.BlockSpec((1, kw, lc), lambda i, t: (i, 0, 0))],
        out_specs=pl.BlockSpec((tq, q2.shape[1]), lambda i, t: (i * nq + t, 0)),
        out_shape=jax.ShapeDtypeStruct(q2.shape, BF16),
        compiler_params=_params(("arbitrary", "arbitrary")),
        name="window_attn",
    )(sink, q2, k, k, k, kx, vt, vt, vt, vtx)


def _rep_rows(p, n):
    return jnp.concatenate([jnp.broadcast_to(p[s:s + 1, :], (n, p.shape[1])) for s in range(p.shape[0])], axis=0)


def _ctab(pr, pi, mr, mi):
    big_l, k = pr.shape[0], mr.shape[0]
    er, ei = _rep_rows(pr, k), _rep_rows(pi, k)
    tr, ti = jnp.tile(mr, (big_l, 1)), jnp.tile(mi, (big_l, 1))
    return er * tr - ei * ti, er * ti + ei * tr


def _cmul_rows(cr, ci, s):
    n = cr.shape[1]
    return jnp.concatenate([cr, cr], axis=1) * s + jnp.concatenate([-ci, ci], axis=1) * pltpu.roll(s, n, 1)


def _seg_scan(x, cr, ci, rowm, nper, reverse):
    rows = x.shape[0]
    s, sh = x, 1
    while sh < nper:
        if reverse:
            shifted, valid = pltpu.roll(s, rows - sh, 0), rowm < nper - sh
        else:
            shifted, valid = pltpu.roll(s, sh, 0), rowm >= sh
        s = s + jnp.where(valid, _cmul_rows(cr, ci, shifted), 0.0)
        cr, ci = cr * cr - ci * ci, 2.0 * cr * ci
        sh *= 2
    return s


def _s5_kernel(ul_ref, ux_ref, lre_ref, lim_ref, ls_ref, bre_ref, bim_ref, cre_ref, cim_ref,
               yl_ref, yx_ref, t_sc, *, nb, ncl, ncx):
    big_l = S5_CHUNK
    k = S5_GROUP_CH
    n = S5_STATE
    w = big_l * k
    tau = lax.broadcasted_iota(jnp.int32, (big_l, 1), 0).astype(F32)
    tabs = []
    for d in range(N_DIRS):
        lr = jnp.minimum(lre_ref[0, d], S5_MAX_RE)
        li = lim_ref[0, d]
        dl = jnp.exp(ls_ref[0, d])
        ar, th = lr * dl, li * dl

        def power(t, ar=ar, th=th):
            mag = jnp.exp(t * ar)
            return mag * jnp.cos(t * th), mag * jnp.sin(t * th)

        lbr, lbi = power(1.0)
        den = lr * lr + li * li
        zr = ((lbr - 1.0) * lr + lbi * li) / den
        zi = (lbi * lr - (lbr - 1.0) * li) / den
        br, bi = bre_ref[0, d], bim_ref[0, d]
        bbr, bbi = zr * br - zi * bi, zr * bi + zi * br
        cr, ci = cre_ref[0, d], cim_ref[0, d]
        tabs.append(dict(power=power, bbr=bbr, bbi=bbi, cr=cr, ci=ci, lam_l=power(float(big_l))))
    f, b = tabs
    e_f = _ctab(*f["power"](big_l - 1.0 - tau), f["bbr"], f["bbi"])
    e_b = _ctab(*b["power"](tau), b["bbr"], b["bbi"])
    g_f = _ctab(*f["power"](tau + 1.0), f["cr"], f["ci"])
    g_b = _ctab(*b["power"](big_l - tau), b["cr"], b["ci"])
    a_b = _ctab(*b["power"](big_l - 1.0 - tau), b["cr"], b["ci"])
    c_f = jnp.tile(f["cr"], (big_l, 1)), jnp.tile(f["ci"], (big_l, 1))
    bb2 = lambda t: jnp.concatenate([t["bbr"], t["bbi"]], axis=1)
    neg = lambda t: jnp.concatenate([t[0], -t[1]], axis=1)
    ka = _dot_nt(bb2(b), neg(a_b), precision=HIGHEST)
    kb = _dot_nt(bb2(f), neg(g_f), precision=HIGHEST)
    kc = _dot_nt(bb2(f), neg(c_f), precision=HIGHEST)
    lane = lax.broadcasted_iota(jnp.int32, (1, w), 1)
    ka = ka + jnp.where(lane >= w - k, kc, 0.0)
    kall = jnp.concatenate([ka, kb], axis=1)
    for s in range(big_l):
        off = (big_l - 1 - s) * k
        t_sc[s * k:(s + 1) * k, :] = kall[:, off:off + w].astype(BF16)
    ecat = jnp.concatenate([e_f[0], e_f[1], e_b[0], e_b[1]], axis=1).astype(BF16)
    gcat = jnp.concatenate([g_f[0], -g_f[1], g_b[0], -g_b[1]], axis=1).astype(BF16)
    tmat = t_sc[...]

    def rowmod(nper):
        return jnp.concatenate([lax.broadcasted_iota(jnp.int32, (nper, 1), 0)] * nb, axis=0)

    def by_batch(rows, nper):
        return jnp.concatenate([jnp.broadcast_to(r, (nper, r.shape[1])) for r in rows], axis=0)

    ux = ux_ref[0].astype(BF16)
    hx = _dot(ux, ecat)
    rmx = rowmod(ncx)
    rx = nb * ncx
    sxf = _seg_scan(hx[:, :2 * n], *f["lam_l"], rmx, ncx, False)
    sxb = _seg_scan(hx[:, 2 * n:], *b["lam_l"], rmx, ncx, True)
    hin_xf = jnp.where(rmx >= 1, pltpu.roll(sxf, 1, 0), 0.0)
    hin_xb = jnp.where(rmx < ncx - 1, pltpu.roll(sxb, rx - 1, 0), 0.0)
    hin_x = jnp.concatenate([hin_xf, hin_xb], axis=1).astype(BF16)
    yx_ref[0] = _dot(ux, tmat) + _dot_nt(hin_x, gcat)
    hc_f = [sxf[i * ncx + ncx - 1:i * ncx + ncx, :] for i in range(nb)]
    hc_b = [sxb[i * ncx:i * ncx + 1, :] for i in range(nb)]
    ul = ul_ref[0].astype(BF16)
    hl = _dot(ul, ecat)
    rml = rowmod(ncl)
    rl = nb * ncl
    xf = jnp.where(rml == 0, by_batch(hc_f, ncl), pltpu.roll(hl[:, :2 * n], 1, 0))
    xb = jnp.where(rml == ncl - 1, by_batch(hc_b, ncl), pltpu.roll(hl[:, 2 * n:], rl - 1, 0))
    hin_f = _seg_scan(xf, *f["lam_l"], rml, ncl, False)
    hin_b = _seg_scan(xb, *b["lam_l"], rml, ncl, True)
    hin = jnp.concatenate([hin_f, hin_b], axis=1).astype(BF16)
    yl_ref[0] = _dot(ul, tmat) + _dot_nt(hin, gcat)


def _s5_call(ul, ux, lre, lim, ls, bre, bim, cre, cim, *, nb):
    g, rl, w = ul.shape
    rx = ux.shape[1]
    grp = lambda a: pl.BlockSpec((1,) + a.shape[1:], lambda i: (i,) + (0,) * (a.ndim - 1))
    return pl.pallas_call(
        functools.partial(_s5_kernel, nb=nb, ncl=rl // nb, ncx=rx // nb),
        grid=(g,),
        in_specs=[grp(a) for a in (ul, ux, lre, lim, ls, bre, bim, cre, cim)],
        out_specs=[grp(ul), grp(ux)],
        out_shape=[jax.ShapeDtypeStruct(ul.shape, F32), jax.ShapeDtypeStruct(ux.shape, F32)],
        scratch_shapes=[pltpu.VMEM((w, w), BF16)],
        compiler_params=_params(("arbitrary",)),
        name="s5",
    )(ul, ux, lre, lim, ls, bre, bim, cre, cim)


def _outproj_kernel(x_ref, gt_ref, yf_ref, yb_ref, z_ref, ga_ref, wa_ref, s5_ref, u_ref, ng_ref,
                    d5_ref, wglu_ref, bglu_ref, wout_ref, o_ref, y5_sc):
    ys = (yf_ref[...] + yb_ref[...]) * _silu(z_ref[...])
    ys = ys * lax.rsqrt(jnp.mean(ys * ys, axis=-1, keepdims=True) + NORM_EPS) * ng_ref[...]
    n_rest = wout_ref.shape[0] - D_S5
    part = _dot(jnp.concatenate([ys.astype(BF16), ga_ref[...], wa_ref[...]], axis=1), wout_ref[:n_rest, :])
    gpl = LANES // S5_GROUP_CH
    nck = y5_sc.shape[1] // S5_CHUNK
    for s in range(S5_CHUNK):
        for half in range(D_S5 // LANES):
            y5_sc[half, pl.ds(s, nck, stride=S5_CHUNK), :] = jnp.concatenate(
                [s5_ref[half * gpl + j, :, s * S5_GROUP_CH:(s + 1) * S5_GROUP_CH] for j in range(gpl)], axis=1)
    y5 = jnp.concatenate([y5_sc[half] for half in range(D_S5 // LANES)], axis=1)
    t = jax.nn.gelu(y5 + d5_ref[...] * u_ref[...])
    t = _dot(t.astype(BF16), wglu_ref[...]) + bglu_ref[...]
    s5o = t[:, :D_S5] * jax.nn.sigmoid(t[:, D_S5:])
    o_ref[...] = x_ref[...] + gt_ref[0] * (part + _dot(s5o.astype(BF16), wout_ref[n_rest:, :]))


def _outproj_call(x2, gate, yf, yb, z, ga, wa, s5, u, ng, d5, wglu, bglu, wout, *, tm, rows_per_seg):
    r, d = x2.shape
    tps = rows_per_seg // tm
    row = lambda a: pl.BlockSpec((tm, a.shape[1]), lambda i: (i, 0))
    seg = pl.BlockSpec((1, 1, d), lambda i: (i // tps, 0, 0))
    full = lambda a: pl.BlockSpec(a.shape, lambda i: (0,) * a.ndim)
    s5_spec = pl.BlockSpec((s5.shape[0], tm // S5_CHUNK, s5.shape[2]), lambda i: (0, i, 0))
    return pl.pallas_call(
        _outproj_kernel,
        grid=(r // tm,),
        in_specs=[row(x2), seg] + [row(a) for a in (yf, yb, z, ga, wa)] + [s5_spec, row(u)]
                 + [full(a) for a in (ng, d5, wglu, bglu, wout)],
        out_specs=row(x2),
        out_shape=jax.ShapeDtypeStruct(x2.shape, F32),
        scratch_shapes=[pltpu.VMEM((D_S5 // LANES, tm, LANES), F32)],
        compiler_params=_params(("arbitrary",)),
        name="outproj",
    )(x2, gate, yf, yb, z, ga, wa, s5, u, ng, d5, wglu, bglu, wout)


def _ffn_kernel(x_ref, xp_ref, xn_ref, sh_ref, sc_ref, gt_ref, g_ref, wgu_ref, cw_ref, cb_ref,
                wd_ref, gfin_ref, o_ref, *, tiles_per_seq, d_ff, final_norm):
    i = pl.program_id(0)
    tm = x_ref.shape[0]
    x = x_ref[...]
    g, sh, sc = g_ref[...], sh_ref[0], sc_ref[0]
    h = _rms_mod(x, g, sh, sc)
    he = jnp.concatenate([_rms_mod(xp_ref[...], g, sh, sc), h, _rms_mod(xn_ref[...], g, sh, sc)], axis=0)
    ge = _dot(he.astype(BF16), wgu_ref[:, :d_ff])
    upj = _dot(h.astype(BF16), wgu_ref[:, d_ff:])
    t = i % tiles_per_seq
    row = lax.broadcasted_iota(jnp.int32, (tm, 1), 0)
    up = jnp.where((row == 0) & (t == 0), 0.0, ge[HALO - 1:HALO - 1 + tm])
    dn = jnp.where((row == tm - 1) & (t == tiles_per_seq - 1), 0.0, ge[HALO + 1:HALO + 1 + tm])
    cw = cw_ref[...]
    conv = cw[0:1] * up + cw[1:2] * ge[HALO:HALO + tm] + cw[2:3] * dn + cb_ref[...]
    act = (_silu(conv) * upj).astype(BF16)
    y = x + gt_ref[0] * _dot(act, wd_ref[...])
    if final_norm:
        y = y * lax.rsqrt(jnp.mean(y * y, axis=-1, keepdims=True) + NORM_EPS) * gfin_ref[...]
    o_ref[...] = y


def _ffn_call(x2, shift, scale, gate, g, wgu, cw, cb, wd, gfin, *, tm, rows_per_seg, seq, final_norm):
    r, d = x2.shape
    d_ff = wd.shape[0]
    tps = rows_per_seg // tm
    hb = tm // HALO
    last = r // HALO - 1
    seg = pl.BlockSpec((1, 1, d), lambda i: (i // tps, 0, 0))
    full = lambda a: pl.BlockSpec(a.shape, lambda i: (0,) * a.ndim)
    once = lambda a: pl.BlockSpec(a.shape, lambda i: (0,) * a.ndim, pipeline_mode=pl.Buffered(1))
    return pl.pallas_call(
        functools.partial(_ffn_kernel, tiles_per_seq=seq // tm, d_ff=d_ff, final_norm=final_norm),
        grid=(r // tm,),
        in_specs=[pl.BlockSpec((tm, d), lambda i: (i, 0)),
                  pl.BlockSpec((HALO, d), lambda i: (jnp.maximum(i * hb - 1, 0), 0)),
                  pl.BlockSpec((HALO, d), lambda i: (jnp.minimum(i * hb + hb, last), 0)),
                  seg, seg, seg, full(g), once(wgu), full(cw), full(cb), once(wd), full(gfin)],
        out_specs=pl.BlockSpec((tm, d), lambda i: (i, 0)),
        out_shape=jax.ShapeDtypeStruct(x2.shape, F32),
        compiler_params=_params(("arbitrary",)),
        name="conv_ffn",
    )(x2, x2, x2, shift, scale, gate, g, wgu, cw, cb, wd, gfin)


def _rope_tables(seq):
    rows = seq // GRID_W
    r = jnp.broadcast_to(jnp.arange(rows, dtype=F32)[:, None], (rows, GRID_W)).reshape(-1)
    c = jnp.broadcast_to(jnp.arange(GRID_W, dtype=F32)[None, :], (rows, GRID_W)).reshape(-1)
    n_freq = HEAD_DIM // 4
    inv = ROPE_THETA ** (-jnp.arange(n_freq, dtype=F32) / n_freq)
    ang = jnp.concatenate([r[:, None] * inv, c[:, None] * inv], axis=-1)
    cos, sin, zero = jnp.cos(ang), jnp.sin(ang), jnp.zeros_like(ang)
    reps = LANES // HEAD_DIM
    return (jnp.tile(jnp.concatenate([cos, cos], axis=-1), (1, reps)),
            jnp.tile(jnp.concatenate([-sin, zero], axis=-1), (1, reps)),
            jnp.tile(jnp.concatenate([zero, sin], axis=-1), (1, reps)))


def _pad_w_in(w):
    offs = [0]
    for s in IN_SIZES:
        offs.append(offs[-1] + s)
    parts = [w[:, offs[j]:offs[j + 1]] for j in range(len(IN_SIZES))]
    parts[2] = jnp.pad(parts[2], ((0, 0), (0, LANES - IN_SIZES[2])))
    return jnp.concatenate(parts, axis=1).astype(BF16)


def kernel(x, c, ctx, c_ctx, w_mod, b_mod, g_mix, w_in, ssd_conv_w, ssd_conv_b, ssd_a_log, ssd_dt_bias, ssd_d, ssd_norm_g, ga_q_norm, ga_k_norm, wa_sink, s5_lambda_re, s5_lambda_im, s5_log_step, s5_b_re, s5_b_im, s5_c_re, s5_c_im, s5_d, s5_w_glu, s5_b_glu, w_out, g_ffn, w_gate, w_up, ffn_conv_w, ffn_conv_b, w_down, g_final):
    b, s, d = x.shape
    lc = ctx.shape[1]
    depth = w_mod.shape[0]
    hpg = 2
    tm = 512

    cc = jnp.zeros((8, d), F32).at[:b].set(c).at[b].set(c_ctx)
    mod = _mod_call(cc, w_mod, b_mod).reshape(depth, 8, 6, d)

    tabs_l = _rope_tables(s)
    one, zero = jnp.ones((lc, LANES), F32), jnp.zeros((lc, LANES), F32)
    tabs_c = (one, zero, zero)
    bd = jnp.kron(jnp.eye(LANES // HEAD_DIM, dtype=F32),
                  jnp.full((HEAD_DIM, HEAD_DIM), 1.0 / HEAD_DIM, F32)).astype(BF16)
    gfin = g_final.reshape(1, d)

    x2 = x.reshape(b * s, d)
    xc2 = ctx.reshape(b * lc, d)
    for i in range(depth):
        need_ctx = i < depth - 1
        ml = lambda j: mod[i, :b, j].reshape(b, 1, d)
        mc = lambda j: mod[i, b, j].reshape(1, 1, d)
        g_i = g_mix[i].reshape(1, d)
        w_pad = _pad_w_in(w_in[i])
        qn = jnp.tile(ga_q_norm[i], LANES // HEAD_DIM).reshape(1, LANES)
        kn = jnp.tile(ga_k_norm[i], LANES // HEAD_DIM).reshape(1, LANES)
        conv_w = ssd_conv_w[i]
        conv_b = ssd_conv_b[i].reshape(1, -1)
        pl_ = _inproj_call(x2, ml(0), ml(1), g_i, w_pad, tabs_l, qn, kn, bd, conv_w, conv_b,
                           tm=tm, rows_per_seg=s, seq=s)
        pc_ = _inproj_call(xc2, mc(0), mc(1), g_i, w_pad, tabs_c, qn, kn, bd, conv_w, conv_b,
                           tm=lc, rows_per_seg=b * lc, seq=lc)
        zl, xsl, btl, cml, dtl, gql, gkl, gvl, wql, wkl, wvl, ul, u5l = pl_
        zc, xsc, btc, cmc, dtc, gqc, gkc, gvc, wqc, wkc, wvc, uc, u5c = pc_

        alog_c = ssd_a_log[i].reshape(-1, 1)
        bias_c = ssd_dt_bias[i].reshape(-1, 1)
        dsk = jnp.repeat(ssd_d[i], HEAD_DIM).reshape(1, D_SSD)
        n_dt = N_DIRS * SSD_HEADS

        def ssd(xs2, bt, cm2, dt2, h0, n):
            dt_t = dt2.reshape(b, n, LANES)[:, :, :n_dt].transpose(0, 2, 1)
            return _ssd_call(xs2.reshape(b, n, -1), bt, cm2.reshape(b, n, -1), dt_t, h0, alog_c, bias_c, dsk)

        h0 = jnp.zeros((b, N_DIRS, SSD_GROUPS, SSD_STATE, hpg * HEAD_DIM), F32)
        yfc, ybc, hc = ssd(xsc, btc, cmc, dtc, h0, lc)
        yfl, ybl, _ = ssd(xsl, btl, cml, dtl, hc, s)

        kx_ga = gkc.reshape(b, lc, -1)
        y_ga = _flash_call(gql, gkl.reshape(b, s, -1), gvl, None, extra=(kx_ga, gvc))

        kx_wa = wkc.reshape(b, lc, -1)
        sink = wa_sink[i].astype(F32)
        y_wa = _win_call(wql, wkl.reshape(b, s, -1), wvl, kx_wa, wvc, sink)

        gm = lambda a: jnp.moveaxis(a, 1, 0)
        lre = gm(s5_lambda_re[i])[:, :, None, :]
        lim = gm(s5_lambda_im[i])[:, :, None, :]
        ls = gm(s5_log_step[i])[:, :, None, None]
        bre = gm(s5_b_re[i]).transpose(0, 1, 3, 2)
        bim = gm(s5_b_im[i]).transpose(0, 1, 3, 2)
        cre, cim = gm(s5_c_re[i]), gm(s5_c_im[i])
        y5l, y5c = _s5_call(u5l, u5c, lre, lim, ls, bre, bim, cre, cim, nb=b)

        ng = ssd_norm_g[i].reshape(1, D_SSD)
        d5 = s5_d[i].reshape(1, D_S5)
        wglu = s5_w_glu[i].astype(BF16)
        bglu = s5_b_glu[i].reshape(1, -1)
        wout = w_out[i].astype(BF16)
        x2 = _outproj_call(x2, ml(2), yfl.reshape(b * s, -1), ybl.reshape(b * s, -1), zl, y_ga, y_wa,
                           y5l, ul, ng, d5, wglu, bglu, wout, tm=tm, rows_per_seg=s)

        gf_i = g_ffn[i].reshape(1, d)
        wgu = jnp.concatenate([w_gate[i], w_up[i]], axis=1).astype(BF16)
        cw = ffn_conv_w[i]
        cb = ffn_conv_b[i].reshape(1, -1)
        wd = w_down[i].astype(BF16)
        x2 = _ffn_call(x2, ml(3), ml(4), ml(5), gf_i, wgu, cw, cb, wd, gfin, tm=512, rows_per_seg=s,
                       seq=s, final_norm=not need_ctx)

        if need_ctx:
            yc_ga = _flash_call(gqc, kx_ga, gvc, None)
            yc_wa = _flash_call(wqc, kx_wa, wvc, sink)
            xc2 = _outproj_call(xc2, mc(2), yfc.reshape(b * lc, -1), ybc.reshape(b * lc, -1), zc, yc_ga,
                                yc_wa, y5c, uc, ng, d5, wglu, bglu, wout, tm=lc, rows_per_seg=b * lc)
            xc2 = _ffn_call(xc2, mc(3), mc(4), mc(5), gf_i, wgu, cw, cb, wd, gfin, tm=lc,
                            rows_per_seg=b * lc, seq=lc, final_norm=False)
    return x2.reshape(b, s, d)
```

```python
import functools

import jax
import jax.numpy as jnp
from jax import lax
from jax.experimental import pallas as pl
from jax.experimental.pallas import tpu as pltpu

F32 = jnp.float32
BF16 = jnp.bfloat16
HIGHEST = lax.Precision.HIGHEST

HEAD_DIM = 64
GRID_W = 64
ROPE_THETA = 10000.0
NORM_EPS = 1e-6
WINDOW = 128
N_DIRS = 2
SSD_HEADS = 4
SSD_GROUPS = 2
SSD_STATE = 128
SSD_CHUNK = 128
SSD_CHUNKS_PER_STEP = 4
D_SSD = SSD_HEADS * HEAD_DIM
SSD_XBC = D_SSD + 2 * SSD_GROUPS * SSD_STATE
S5_GROUPS = 16
S5_GROUP_CH = 16
S5_STATE = 64
S5_MAX_RE = -1e-4
S5_CHUNK = 32
D_S5 = S5_GROUPS * S5_GROUP_CH
LANES = 128
HALO = 8
NEG_BIG = -1e30
LOG2E = 1.4426950408889634
Q_HEADS = 4
KV_HEADS = 2
HPG = Q_HEADS // KV_HEADS
ONES_ROWS = 16
FLASH_SUB_K = 256
FLASH_SUB_Q = 256
FLASH_DEPTH = 6
VMEM_LIMIT = 52 * 1024 * 1024

IN_SIZES = (D_SSD, SSD_XBC, N_DIRS * SSD_HEADS, 256, 128, 128, 256, 128, 128, D_S5)
P_Z, P_XBC, P_DT, P_GQ, P_GK, P_GV, P_WQ, P_WK, P_WV, P_U, P_END = (
    0, 256, 1024, 1152, 1408, 1536, 1664, 1920, 2048, 2176, 2432)


def _params(sem=None, flags=None):
    return pltpu.CompilerParams(dimension_semantics=sem, vmem_limit_bytes=VMEM_LIMIT, flags=flags)


def _silu(v):
    return v * jax.nn.sigmoid(v)


def _softplus(v):
    return jnp.maximum(v, 0.0) + jnp.log1p(jnp.exp(-jnp.abs(v)))


def _dot(a, b):
    return jnp.dot(a, b, preferred_element_type=F32)


def _dot_nt(a, b, precision=None):
    return lax.dot_general(a, b, (((1,), (1,)), ((), ())), preferred_element_type=F32,
                           precision=precision)


def _dot_tn(a, b):
    return lax.dot_general(a, b, (((0,), (0,)), ((), ())), preferred_element_type=F32)


def _mod_kernel(cc_ref, w_ref, b_ref, o_ref):
    s = _silu(cc_ref[...])
    o_ref[0] = jnp.dot(s, w_ref[0], preferred_element_type=F32, precision=HIGHEST) + b_ref[0]


def _mod_call(cc, w_mod, b_mod):
    n_layers, d, n = w_mod.shape
    tn = 1536
    return pl.pallas_call(
        _mod_kernel,
        grid=(n_layers, n // tn),
        in_specs=[pl.BlockSpec((8, d), lambda l, j: (0, 0)),
                  pl.BlockSpec((1, d, tn), lambda l, j: (l, 0, j)),
                  pl.BlockSpec((1, 1, tn), lambda l, j: (l, 0, j))],
        out_specs=pl.BlockSpec((1, 8, tn), lambda l, j: (l, 0, j)),
        out_shape=jax.ShapeDtypeStruct((n_layers, 8, n), F32),
        compiler_params=_params(("arbitrary", "arbitrary")),
        name="adaln_mod",
    )(cc, w_mod, b_mod.reshape(n_layers, 1, n))


def _rms_mod(x, g, shift, scale):
    y = x * lax.rsqrt(jnp.mean(x * x, axis=-1, keepdims=True) + NORM_EPS) * g
    return y * (1.0 + scale) + shift


def _rope(t, cos, sna, snb):
    return t * cos + pltpu.roll(t, 96, 1) * sna + pltpu.roll(t, 32, 1) * snb


def _head_rms(t, gain, bd):
    t2 = t * t
    hi = t2.astype(BF16)
    lo = (t2 - hi.astype(F32)).astype(BF16)
    ms = _dot(hi, bd) + _dot(lo, bd)
    return t * lax.rsqrt(ms + NORM_EPS) * gain


def _conv3_silu(v, prev, nxt, cw, cb):
    n = v.shape[0]
    row = lax.broadcasted_iota(jnp.int32, (n, 1), 0)
    up = jnp.where(row == 0, prev, pltpu.roll(v, 1, 0))
    dn = jnp.where(row == n - 1, nxt, pltpu.roll(v, n - 1, 0))
    return _silu(cw[0:1] * up + cw[1:2] * v + cw[2:3] * dn + cb)


def _inproj_kernel(x_ref, xp_ref, xn_ref, sh_ref, sc_ref, g_ref, w_ref, cos_ref, sna_ref, snb_ref,
                   qn_ref, kn_ref, bd_ref, cw_ref, cb_ref, z_ref, xs_ref, bt_ref, cm_ref, dt_ref, gq_ref, gk_ref,
                   gv_ref, wq_ref, wk_ref, wv_ref, u_ref, u5_ref, u_sc, *, tiles_per_seq):
    g, sh, sc = g_ref[...], sh_ref[0], sc_ref[0]
    hb = _rms_mod(x_ref[...], g, sh, sc).astype(BF16)
    halo = jnp.concatenate([xp_ref[...], xn_ref[...]], axis=0)
    u = _dot(hb, w_ref[:, P_U:P_END])
    p = _dot(hb, w_ref[:, P_Z:P_GQ])
    ph = _dot(_rms_mod(halo, g, sh, sc).astype(BF16), w_ref[:, P_XBC:P_DT])
    pg = _dot(hb, w_ref[:, P_GQ:P_WQ])
    pw = _dot(hb, w_ref[:, P_WQ:P_U])
    u_ref[...] = u
    gpl = LANES // S5_GROUP_CH
    for half in range(D_S5 // LANES):
        u_sc[half] = u[:, half * LANES:(half + 1) * LANES]
    for s in range(S5_CHUNK):
        for half in range(D_S5 // LANES):
            rows = u_sc[half, pl.ds(s, u_sc.shape[1] // S5_CHUNK, stride=S5_CHUNK), :]
            for j in range(gpl):
                u5_ref[half * gpl + j, :, s * S5_GROUP_CH:(s + 1) * S5_GROUP_CH] = (
                    rows[:, j * S5_GROUP_CH:(j + 1) * S5_GROUP_CH])
    cos, sna, snb = cos_ref[...], sna_ref[...], snb_ref[...]
    bd = bd_ref[...]
    q_scale = HEAD_DIM ** -0.5 * LOG2E
    z_ref[...] = p[:, P_Z:P_XBC].astype(z_ref.dtype)
    t = pl.program_id(0) % tiles_per_seq
    prev = jnp.where(t > 0, ph[HALO - 1:HALO, :], 0.0)
    nxt = jnp.where(t < tiles_per_seq - 1, ph[HALO:HALO + 1, :], 0.0)
    act = _conv3_silu(p[:, P_XBC:P_DT], prev, nxt, cw_ref[...], cb_ref[...])
    xs_ref[...] = act[:, :D_SSD]
    n_bc = SSD_GROUPS * SSD_STATE
    bt_ref[0] = jnp.transpose(act[:, D_SSD:D_SSD + n_bc]).astype(BF16)
    cm_ref[...] = act[:, D_SSD + n_bc:].astype(BF16)
    dt_ref[0] = jnp.transpose(p[:, P_DT:P_GQ])[:dt_ref.shape[1]]
    for j in range(2):
        q = _rope(_head_rms(pg[:, j * LANES:(j + 1) * LANES], qn_ref[...], bd), cos, sna, snb) * q_scale
        gq_ref[:, j * LANES:(j + 1) * LANES] = q.astype(BF16)
    gk_ref[...] = _rope(_head_rms(pg[:, P_GK - P_GQ:P_GV - P_GQ], kn_ref[...], bd), cos, sna, snb).astype(BF16)
    gv_ref[0] = jnp.transpose(pg[:, P_GV - P_GQ:]).astype(BF16)
    for j in range(2):
        q = _rope(pw[:, j * LANES:(j + 1) * LANES], cos, sna, snb) * q_scale
        wq_ref[:, j * LANES:(j + 1) * LANES] = q.astype(BF16)
    wk_ref[...] = _rope(pw[:, P_WK - P_WQ:P_WV - P_WQ], cos, sna, snb).astype(BF16)
    wv_ref[0] = jnp.transpose(pw[:, P_WV - P_WQ:]).astype(BF16)


def _inproj_call(x2, shift, scale, g, w_pad, tabs, qn, kn, bd, cw, cb, *, tm, rows_per_seg, seq):
    r, d = x2.shape
    tps = rows_per_seg // tm
    tpq = seq // tm
    hb = tm // HALO
    last = r // HALO - 1
    row = lambda w: pl.BlockSpec((tm, w), lambda i: (i, 0))
    seg = pl.BlockSpec((1, 1, d), lambda i: (i // tps, 0, 0))
    full = lambda a: pl.BlockSpec(a.shape, lambda i: (0,) * a.ndim)
    tab = pl.BlockSpec((tm, LANES), lambda i: (i % tpq, 0))
    n_bc = SSD_GROUPS * SSD_STATE
    widths = (256, D_SSD, -n_bc, n_bc, -IN_SIZES[2], 256, 128, -LANES, 256, 128, -LANES, D_S5)
    dtypes = (BF16, F32, BF16, BF16, F32, BF16, BF16, BF16, BF16, BF16, BF16, F32)
    spec = lambda w: row(w) if w > 0 else pl.BlockSpec((1, -w, tm), lambda i: (i // tpq, 0, i % tpq))
    shape = lambda w: (r, w) if w > 0 else (r // seq, -w, seq)
    w5 = S5_CHUNK * S5_GROUP_CH
    u5_spec = pl.BlockSpec((S5_GROUPS, tm // S5_CHUNK, w5), lambda i: (0, i, 0))
    return pl.pallas_call(
        functools.partial(_inproj_kernel, tiles_per_seq=tpq),
        grid=(r // tm,),
        in_specs=[row(d),
                  pl.BlockSpec((HALO, d), lambda i: (jnp.maximum(i * hb - 1, 0), 0)),
                  pl.BlockSpec((HALO, d), lambda i: (jnp.minimum(i * hb + hb, last), 0)),
                  seg, seg, full(g), full(w_pad), tab, tab, tab, full(qn), full(kn), full(bd),
                  full(cw), full(cb)],
        out_specs=[spec(w) for w in widths] + [u5_spec],
        out_shape=[jax.ShapeDtypeStruct(shape(w), t) for w, t in zip(widths, dtypes)]
                  + [jax.ShapeDtypeStruct((S5_GROUPS, r // S5_CHUNK, w5), F32)],
        scratch_shapes=[pltpu.VMEM((D_S5 // LANES, tm, LANES), F32)],
        compiler_params=_params(("arbitrary",)),
        name="inproj",
    )(x2, x2, x2, shift, scale, g, w_pad, *tabs, qn, kn, bd, cw, cb)


def _ssd_kernel(xsf_ref, btf_ref, cmf_ref, xsb_ref, btb_ref, cmb_ref, dtrf_ref, dtrb_ref, h0_ref, alc_ref,
                bic_ref, dsk_ref, yf_ref, yb_ref, hout_ref, h_sc, *, nc):
    c = pl.program_id(1)

    @pl.when(c == 0)
    def _():
        h_sc[...] = h0_ref[0]

    q = SSD_CHUNK
    nsub = xsf_ref.shape[1] // q
    hpg = SSD_HEADS // SSD_GROUPS
    xs_refs, dtr_refs, y_refs = (xsf_ref, xsb_ref), (dtrf_ref, dtrb_ref), (yf_ref, yb_ref)
    bt_refs, cm_refs = (btf_ref, btb_ref), (cmf_ref, cmb_ref)
    order = (tuple(range(nsub)), tuple(reversed(range(nsub))))
    tok = lambda j: slice(j * q, (j + 1) * q)
    bmat_t = lambda d, j, g: bt_refs[d][0, g * SSD_STATE:(g + 1) * SSD_STATE, tok(j)]
    cmat = lambda d, j, g: cm_refs[d][0, tok(j), g * SSD_STATE:(g + 1) * SSD_STATE]
    ri = lax.broadcasted_iota(jnp.int32, (q, q), 0)
    ci = lax.broadcasted_iota(jnp.int32, (q, q), 1)
    mask = (ri >= ci, ri <= ci)
    lane = lax.broadcasted_iota(jnp.int32, (1, LANES), 1)
    first = lane < HEAD_DIM
    dgs = [(d, g) for d in range(N_DIRS) for g in range(SSD_GROUPS)]
    items = [(d, j, g) for j in range(nsub) for d, g in dgs]

    h = {dg: h_sc[dg[0], dg[1]] for dg in dgs}
    yoff0 = {(d, g): _dot(cmat(d, order[d][0], g), h[d, g].astype(BF16)) for d, g in dgs}
    cb = {(d, j, g): _dot(cmat(d, j, g), bmat_t(d, j, g)) for d, j, g in items}

    dt_r, acs_c, acs_r, dt_c, tot = {}, {}, {}, {}, {}
    for j in range(nsub):
        for d in range(N_DIRS):
            dt_r[d, j] = _softplus(dtr_refs[d][0, :, tok(j)] + bic_ref[...])
    for j in range(nsub):
        for d in range(N_DIRS):
            dta_r = dt_r[d, j] * (-jnp.exp(alc_ref[...]))
            maskf = mask[d].astype(F32)
            pad = jnp.zeros((LANES - dta_r.shape[0], q), F32)
            acs_c[d, j] = _dot_nt(maskf, jnp.concatenate([dta_r, pad], axis=0), precision=HIGHEST)
            acs_r[d, j] = _dot_nt(dta_r, maskf, precision=HIGHEST)
            dt_c[d, j] = jnp.transpose(jnp.concatenate([dt_r[d, j], pad], axis=0))
            tot[d, j] = acs_c[d, j][q - 1:q] if d == 0 else acs_c[d, j][0:1]

    ydiag, acs_g, tot_g, st = {}, {}, {}, {}
    for d, j, g in items:
        e0 = d * SSD_HEADS + g * hpg
        pick = lambda v: jnp.where(first, v[:, e0:e0 + 1], v[:, e0 + 1:e0 + 2])
        acs_g[d, j, g], tot_g[d, j, g] = pick(acs_c[d, j]), pick(tot[d, j])
        xs_g = xs_refs[d][0, tok(j), g * LANES:(g + 1) * LANES]
        xdt = xs_g * pick(dt_c[d, j])
        xdt_b = xdt.astype(BF16)
        yd = []
        for a in range(hpg):
            e = e0 + a
            dec = jnp.where(mask[d], jnp.exp(acs_c[d, j][:, e:e + 1] - acs_r[d, j][e:e + 1, :]), 0.0)
            yd.append(_dot((cb[d, j, g] * dec).astype(BF16), xdt_b))
        y = jnp.where(first, yd[0], yd[1])
        if d == 0:
            y = y + dsk_ref[:, g * LANES:(g + 1) * LANES] * xs_g
        ydiag[d, j, g] = y
        w = (xdt * jnp.exp(tot_g[d, j, g] - acs_g[d, j, g])).astype(BF16)
        st[d, j, g] = _dot(bmat_t(d, j, g), w)

    for idx in range(nsub):
        for d, g in dgs:
            j = order[d][idx]
            yoff = yoff0[d, g] if idx == 0 else _dot(cmat(d, j, g), h[d, g].astype(BF16))
            y = ydiag[d, j, g] + yoff * jnp.exp(acs_g[d, j, g])
            y_refs[d][0, tok(j), g * LANES:(g + 1) * LANES] = y.astype(y_refs[d].dtype)
            h[d, g] = h[d, g] * jnp.exp(tot_g[d, j, g]) + st[d, j, g]
    for d, g in dgs:
        h_sc[d, g] = h[d, g]

    @pl.when(c == nc - 1)
    def _():
        hout_ref[0] = h_sc[...]


def _ssd_call(xs, bt, cm, dt_t, h0, alog_c, bias_c, dsk):
    b, s, _ = xs.shape
    q = SSD_CHUNK * min(SSD_CHUNKS_PER_STEP, s // SSD_CHUNK)
    nc = s // q
    fwd = lambda c: c
    bwd = lambda c: nc - 1 - c
    rows = lambda a, f: pl.BlockSpec((1, q, a.shape[2]), lambda i, c: (i, f(c), 0))
    cols = lambda a, f: pl.BlockSpec((1, a.shape[1], q), lambda i, c: (i, 0, f(c)))
    drspec = lambda f: cols(dt_t, f)
    full = lambda a: pl.BlockSpec(a.shape, lambda i, c: (0,) * a.ndim)
    hspec = pl.BlockSpec((1,) + h0.shape[1:], lambda i, c: (i, 0, 0, 0, 0))
    return pl.pallas_call(
        functools.partial(_ssd_kernel, nc=nc),
        grid=(b, nc),
        in_specs=[rows(xs, fwd), cols(bt, fwd), rows(cm, fwd), rows(xs, bwd), cols(bt, bwd), rows(cm, bwd),
                  drspec(fwd), drspec(bwd), hspec, full(alog_c), full(bias_c), full(dsk)],
        out_specs=[rows(xs, fwd), rows(xs, bwd), hspec],
        out_shape=[jax.ShapeDtypeStruct(xs.shape, BF16), jax.ShapeDtypeStruct(xs.shape, BF16),
                   jax.ShapeDtypeStruct(h0.shape, F32)],
        scratch_shapes=[pltpu.VMEM(h0.shape[1:], F32)],
        compiler_params=_params(("arbitrary", "arbitrary")),
        name="ssd",
    )(xs, bt, cm, xs, bt, cm, dt_t, dt_t, h0, alog_c, bias_c, dsk)


def _q_transposed(q):
    qt = jnp.transpose(q.astype(F32))
    zero = jnp.zeros((HEAD_DIM, q.shape[0]), BF16)
    out = []
    for h in range(Q_HEADS):
        blk = qt[h * HEAD_DIM:(h + 1) * HEAD_DIM].astype(BF16)
        out.append(jnp.concatenate([blk, zero] if h // HPG == 0 else [zero, blk], axis=0))
    return out


def _flash_tiles(srcs, qt_sc, m_sc, acc_sc):
    tq = qt_sc.shape[2]
    ones = jnp.ones((ONES_ROWS, FLASH_SUB_K), BF16)
    chains = [(h, slice(jq * FLASH_SUB_Q, (jq + 1) * FLASH_SUB_Q))
              for h in range(Q_HEADS) for jq in range(tq // FLASH_SUB_Q)]
    m = [m_sc[h, :, cs] for h, cs in chains]
    acc = [acc_sc[h, :, cs] for h, cs in chains]
    units = [(src, jk, c) for src in srcs for jk in range(src[0].shape[1] // FLASH_SUB_K)
             for c in range(len(chains))]

    def scores(u):
        (k_ref, _), jk, c = u
        h, cs = chains[c]
        return _dot(k_ref[0, jk * FLASH_SUB_K:(jk + 1) * FLASH_SUB_K, :], qt_sc[h, :, cs])

    def finish(u, s):
        (_, vt_ref), jk, c = u
        g = chains[c][0] // HPG
        vte = jnp.concatenate(
            [vt_ref[0, g * HEAD_DIM:(g + 1) * HEAD_DIM, jk * FLASH_SUB_K:(jk + 1) * FLASH_SUB_K], ones], axis=0)
        m_new = jnp.maximum(m[c], jnp.max(s, axis=0, keepdims=True))
        p = jnp.exp2(s - m_new).astype(BF16)
        acc[c] = jnp.exp2(m[c] - m_new) * acc[c] + _dot(vte, p)
        m[c] = m_new

    pending = [scores(u) for u in units[:FLASH_DEPTH]]
    for i, u in enumerate(units):
        s_cur = pending.pop(0)
        if i + FLASH_DEPTH < len(units):
            pending.append(scores(units[i + FLASH_DEPTH]))
        finish(u, s_cur)
    for c, (h, cs) in enumerate(chains):
        m_sc[h, :, cs] = m[c]
        acc_sc[h, :, cs] = acc[c]


def _flash_kernel(sink_ref, q_ref, k_ref, vt_ref, *rest, nk, has_sink, has_extra):
    if has_extra:
        kx_ref, vtx_ref, o_ref, qt_sc, m_sc, acc_sc = rest
    else:
        o_ref, qt_sc, m_sc, acc_sc = rest
    ki = pl.program_id(2)
    tq = q_ref.shape[0]

    def init():
        qts = _q_transposed(q_ref[...])
        for h in range(Q_HEADS):
            qt_sc[h] = qts[h]
            if has_sink:
                m_sc[h] = jnp.full(m_sc.shape[1:], sink_ref[h] * LOG2E, F32)
                acc_sc[h] = jnp.concatenate([jnp.zeros((HEAD_DIM, tq), F32), jnp.ones((ONES_ROWS, tq), F32)], axis=0)
            else:
                m_sc[h] = jnp.full(m_sc.shape[1:], NEG_BIG, F32)
                acc_sc[h] = jnp.zeros(acc_sc.shape[1:], F32)

    if has_extra:
        @pl.when(ki == 0)
        def _():
            init()
            _flash_tiles([(kx_ref, vtx_ref), (k_ref, vt_ref)], qt_sc, m_sc, acc_sc)

        @pl.when(ki > 0)
        def _():
            _flash_tiles([(k_ref, vt_ref)], qt_sc, m_sc, acc_sc)
    else:
        pl.when(ki == 0)(init)
        _flash_tiles([(k_ref, vt_ref)], qt_sc, m_sc, acc_sc)

    @pl.when(ki == nk - 1)
    def _():
        outs = []
        for h in range(Q_HEADS):
            acc = acc_sc[h]
            outs.append(acc[:HEAD_DIM] / acc[HEAD_DIM:HEAD_DIM + 1])
        o_ref[...] = jnp.transpose(jnp.concatenate(outs, axis=0)).astype(o_ref.dtype)


def _pick(n, cands):
    for c in cands:
        if n % c == 0:
            return c
    return n


def _flash_call(q2, k, vt, sink, extra=None):
    b, sk, kw = k.shape
    sq = q2.shape[0] // b
    tq = _pick(sq, (512, 256))
    tk = _pick(sk, (4096, 1024, 512, 256))
    nq = sq // tq
    nk = sk // tk
    has_sink = sink is not None
    if sink is None:
        sink = jnp.zeros((Q_HEADS,), F32)
    in_specs = [pl.BlockSpec(memory_space=pltpu.SMEM),
                pl.BlockSpec((tq, q2.shape[1]), lambda i, qi, ki: (i * nq + qi, 0)),
                pl.BlockSpec((1, tk, kw), lambda i, qi, ki: (i, ki, 0)),
                pl.BlockSpec((1, kw, tk), lambda i, qi, ki: (i, 0, ki))]
    args = [sink, q2, k, vt]
    if extra is not None:
        lx = extra[0].shape[1]
        in_specs += [pl.BlockSpec((1, lx, kw), lambda i, qi, ki: (i, 0, 0)),
                     pl.BlockSpec((1, kw, lx), lambda i, qi, ki: (i, 0, 0))]
        args += list(extra)
    return pl.pallas_call(
        functools.partial(_flash_kernel, nk=nk, has_sink=has_sink, has_extra=extra is not None),
        grid=(b, nq, nk),
        in_specs=in_specs,
        out_specs=pl.BlockSpec((tq, q2.shape[1]), lambda i, qi, ki: (i * nq + qi, 0)),
        out_shape=jax.ShapeDtypeStruct(q2.shape, BF16),
        scratch_shapes=[pltpu.VMEM((Q_HEADS, kw, tq), BF16), pltpu.VMEM((Q_HEADS, 1, tq), F32),
                        pltpu.VMEM((Q_HEADS, HEAD_DIM + ONES_ROWS, tq), F32)],
        compiler_params=_params(("arbitrary", "arbitrary", "arbitrary")),
        name="flash_sink" if has_sink else "flash",
    )(*args)


def _win_kernel(sink_ref, q_ref, kp_ref, kc_ref, kn_ref, kx_ref, vtp_ref, vtc_ref, vtn_ref, vtx_ref,
                o_ref, *, nq):
    j = pl.program_id(1)
    tq = q_ref.shape[0]
    blk = kp_ref.shape[1]
    rc = lax.broadcasted_iota(jnp.int32, (tq, tq), 0) - lax.broadcasted_iota(jnp.int32, (tq, tq), 1)
    m_cur = (rc <= WINDOW) & (rc >= -WINDOW)
    rp = lax.broadcasted_iota(jnp.int32, (blk, tq), 0) - lax.broadcasted_iota(jnp.int32, (blk, tq), 1)
    m_prev = (rp - blk >= -WINDOW) & (j > 0)
    m_next = (rp + tq <= WINDOW) & (j < nq - 1)
    qts = _q_transposed(q_ref[...])
    raw = [[_dot(kr[0], qt) for kr in (kc_ref, kp_ref, kn_ref, kx_ref)] for qt in qts]
    outs = []
    for h in range(Q_HEADS):
        rows = slice((h // HPG) * HEAD_DIM, (h // HPG + 1) * HEAD_DIM)
        sink = sink_ref[h] * LOG2E
        s_c = jnp.where(m_cur, raw[h][0], NEG_BIG)
        s_p = jnp.where(m_prev, raw[h][1], NEG_BIG)
        s_n = jnp.where(m_next, raw[h][2], NEG_BIG)
        s_x = raw[h][3]
        cmax = lambda v: jnp.max(v, axis=0, keepdims=True)
        m = jnp.maximum(jnp.maximum(cmax(s_c), cmax(s_p)), jnp.maximum(cmax(s_n), cmax(s_x)))
        m = jnp.maximum(m, sink)
        p_c, p_p, p_n, p_x = (jnp.exp2(v - m) for v in (s_c, s_p, s_n, s_x))
        csum = lambda v: jnp.sum(v, axis=0, keepdims=True)
        l = csum(p_c) + csum(p_p) + csum(p_n) + csum(p_x) + jnp.exp2(sink - m)
        o = (_dot(vtc_ref[0, rows, :], p_c.astype(BF16)) + _dot(vtp_ref[0, rows, :], p_p.astype(BF16))
             + _dot(vtn_ref[0, rows, :], p_n.astype(BF16)) + _dot(vtx_ref[0, rows, :], p_x.astype(BF16)))
        outs.append(o / l)
    o_ref[...] = jnp.transpose(jnp.concatenate(outs, axis=0)).astype(o_ref.dtype)


def _win_call(q2, k, vt, kx, vtx, sink):
    b, s, kw = k.shape
    lc = kx.shape[1]
    blk = WINDOW
    tq = _pick(s, (512, 256, 128))
    nq = s // tq
    r = tq // blk
    nblk = s // blk
    prev = lambda t: jnp.maximum(t * r - 1, 0)
    nxt = lambda t: jnp.minimum(t * r + r, nblk - 1)
    return pl.pallas_call(
        functools.partial(_win_kernel, nq=nq),
        grid=(b, nq),
        in_specs=[pl.BlockSpec(memory_space=pltpu.SMEM),
                  pl.BlockSpec((tq, q2.shape[1]), lambda i, t: (i * nq + t, 0)),
                  pl.BlockSpec((1, blk, kw), lambda i, t: (i, prev(t), 0)),
                  pl.BlockSpec((1, tq, kw), lambda i, t: (i, t, 0)),
                  pl.BlockSpec((1, blk, kw), lambda i, t: (i, nxt(t), 0)),
                  pl.BlockSpec((1, lc, kw), lambda i, t: (i, 0, 0)),
                  pl.BlockSpec((1, kw, blk), lambda i, t: (i, 0, prev(t))),
                  pl.BlockSpec((1, kw, tq), lambda i, t: (i, 0, t)),
                  pl.BlockSpec((1, kw, blk), lambda i, t: (i, 0, nxt(t))),
                  pl.BlockSpec((1, kw, lc), lambda i, t: (i, 0, 0))],
        out_specs=pl.BlockSpec((tq, q2.shape[1]), lambda i, t: (i * nq + t, 0)),
        out_shape=jax.ShapeDtypeStruct(q2.shape, BF16),
        compiler_params=_params(("arbitrary", "arbitrary")),
        name="window_attn",
    )(sink, q2, k, k, k, kx, vt, vt, vt, vtx)


def _rep_rows(p, n):
    return jnp.concatenate([jnp.broadcast_to(p[s:s + 1, :], (n, p.shape[1])) for s in range(p.shape[0])], axis=0)


def _ctab(pr, pi, mr, mi):
    big_l, k = pr.shape[0], mr.shape[0]
    er, ei = _rep_rows(pr, k), _rep_rows(pi, k)
    tr, ti = jnp.tile(mr, (big_l, 1)), jnp.tile(mi, (big_l, 1))
    return er * tr - ei * ti, er * ti + ei * tr


def _cmul_rows(cr, ci, s):
    n = cr.shape[1]
    return jnp.concatenate([cr, cr], axis=1) * s + jnp.concatenate([-ci, ci], axis=1) * pltpu.roll(s, n, 1)


def _seg_scan(x, cr, ci, rowm, nper, reverse):
    rows = x.shape[0]
    s, sh = x, 1
    while sh < nper:
        if reverse:
            shifted, valid = pltpu.roll(s, rows - sh, 0), rowm < nper - sh
        else:
            shifted, valid = pltpu.roll(s, sh, 0), rowm >= sh
        s = s + jnp.where(valid, _cmul_rows(cr, ci, shifted), 0.0)
        cr, ci = cr * cr - ci * ci, 2.0 * cr * ci
        sh *= 2
    return s


def _s5_kernel(ul_ref, ux_ref, lre_ref, lim_ref, ls_ref, bre_ref, bim_ref, cre_ref, cim_ref,
               yl_ref, yx_ref, t_sc, *, nb, ncl, ncx):
    big_l = S5_CHUNK
    k = S5_GROUP_CH
    n = S5_STATE
    w = big_l * k
    tau = lax.broadcasted_iota(jnp.int32, (big_l, 1), 0).astype(F32)
    tabs = []
    for d in range(N_DIRS):
        lr = jnp.minimum(lre_ref[0, d], S5_MAX_RE)
        li = lim_ref[0, d]
        dl = jnp.exp(ls_ref[0, d])
        ar, th = lr * dl, li * dl

        def power(t, ar=ar, th=th):
            mag = jnp.exp(t * ar)
            return mag * jnp.cos(t * th), mag * jnp.sin(t * th)

        lbr, lbi = power(1.0)
        den = lr * lr + li * li
        zr = ((lbr - 1.0) * lr + lbi * li) / den
        zi = (lbi * lr - (lbr - 1.0) * li) / den
        br, bi = bre_ref[0, d], bim_ref[0, d]
        bbr, bbi = zr * br - zi * bi, zr * bi + zi * br
        cr, ci = cre_ref[0, d], cim_ref[0, d]
        tabs.append(dict(power=power, bbr=bbr, bbi=bbi, cr=cr, ci=ci, lam_l=power(float(big_l))))
    f, b = tabs
    e_f = _ctab(*f["power"](big_l - 1.0 - tau), f["bbr"], f["bbi"])
    e_b = _ctab(*b["power"](tau), b["bbr"], b["bbi"])
    g_f = _ctab(*f["power"](tau + 1.0), f["cr"], f["ci"])
    g_b = _ctab(*b["power"](big_l - tau), b["cr"], b["ci"])
    a_b = _ctab(*b["power"](big_l - 1.0 - tau), b["cr"], b["ci"])
    c_f = jnp.tile(f["cr"], (big_l, 1)), jnp.tile(f["ci"], (big_l, 1))
    bb2 = lambda t: jnp.concatenate([t["bbr"], t["bbi"]], axis=1)
    neg = lambda t: jnp.concatenate([t[0], -t[1]], axis=1)
    ka = _dot_nt(bb2(b), neg(a_b), precision=HIGHEST)
    kb = _dot_nt(bb2(f), neg(g_f), precision=HIGHEST)
    kc = _dot_nt(bb2(f), neg(c_f), precision=HIGHEST)
    lane = lax.broadcasted_iota(jnp.int32, (1, w), 1)
    ka = ka + jnp.where(lane >= w - k, kc, 0.0)
    kall = jnp.concatenate([ka, kb], axis=1)
    for s in range(big_l):
        off = (big_l - 1 - s) * k
        t_sc[s * k:(s + 1) * k, :] = kall[:, off:off + w].astype(BF16)
    ecat = jnp.concatenate([e_f[0], e_f[1], e_b[0], e_b[1]], axis=1).astype(BF16)
    gcat = jnp.concatenate([g_f[0], -g_f[1], g_b[0], -g_b[1]], axis=1).astype(BF16)
    tmat = t_sc[...]

    def rowmod(nper):
        return jnp.concatenate([lax.broadcasted_iota(jnp.int32, (nper, 1), 0)] * nb, axis=0)

    def by_batch(rows, nper):
        return jnp.concatenate([jnp.broadcast_to(r, (nper, r.shape[1])) for r in rows], axis=0)

    ux = ux_ref[0].astype(BF16)
    hx = _dot(ux, ecat)
    rmx = rowmod(ncx)
    rx = nb * ncx
    sxf = _seg_scan(hx[:, :2 * n], *f["lam_l"], rmx, ncx, False)
    sxb = _seg_scan(hx[:, 2 * n:], *b["lam_l"], rmx, ncx, True)
    hin_xf = jnp.where(rmx >= 1, pltpu.roll(sxf, 1, 0), 0.0)
    hin_xb = jnp.where(rmx < ncx - 1, pltpu.roll(sxb, rx - 1, 0), 0.0)
    hin_x = jnp.concatenate([hin_xf, hin_xb], axis=1).astype(BF16)
    yx_ref[0] = _dot(ux, tmat) + _dot_nt(hin_x, gcat)
    hc_f = [sxf[i * ncx + ncx - 1:i * ncx + ncx, :] for i in range(nb)]
    hc_b = [sxb[i * ncx:i * ncx + 1, :] for i in range(nb)]
    ul = ul_ref[0].astype(BF16)
    hl = _dot(ul, ecat)
    rml = rowmod(ncl)
    rl = nb * ncl
    xf = jnp.where(rml == 0, by_batch(hc_f, ncl), pltpu.roll(hl[:, :2 * n], 1, 0))
    xb = jnp.where(rml == ncl - 1, by_batch(hc_b, ncl), pltpu.roll(hl[:, 2 * n:], rl - 1, 0))
    hin_f = _seg_scan(xf, *f["lam_l"], rml, ncl, False)
    hin_b = _seg_scan(xb, *b["lam_l"], rml, ncl, True)
    hin = jnp.concatenate([hin_f, hin_b], axis=1).astype(BF16)
    yl_ref[0] = _dot(ul, tmat) + _dot_nt(hin, gcat)


def _s5_call(ul, ux, lre, lim, ls, bre, bim, cre, cim, *, nb):
    g, rl, w = ul.shape
    rx = ux.shape[1]
    grp = lambda a: pl.BlockSpec((1,) + a.shape[1:], lambda i: (i,) + (0,) * (a.ndim - 1))
    return pl.pallas_call(
        functools.partial(_s5_kernel, nb=nb, ncl=rl // nb, ncx=rx // nb),
        grid=(g,),
        in_specs=[grp(a) for a in (ul, ux, lre, lim, ls, bre, bim, cre, cim)],
        out_specs=[grp(ul), grp(ux)],
        out_shape=[jax.ShapeDtypeStruct(ul.shape, F32), jax.ShapeDtypeStruct(ux.shape, F32)],
        scratch_shapes=[pltpu.VMEM((w, w), BF16)],
        compiler_params=_params(("arbitrary",)),
        name="s5",
    )(ul, ux, lre, lim, ls, bre, bim, cre, cim)


def _outproj_kernel(x_ref, gt_ref, yf_ref, yb_ref, z_ref, ga_ref, wa_ref, s5_ref, u_ref, ng_ref,
                    d5_ref, wglu_ref, bglu_ref, wout_ref, o_ref, y5_sc):
    ys = (yf_ref[...].astype(F32) + yb_ref[...].astype(F32)) * _silu(z_ref[...].astype(F32))
    ys = ys * lax.rsqrt(jnp.mean(ys * ys, axis=-1, keepdims=True) + NORM_EPS) * ng_ref[...]
    n_rest = wout_ref.shape[0] - D_S5
    part = _dot(jnp.concatenate([ys.astype(BF16), ga_ref[...], wa_ref[...]], axis=1), wout_ref[:n_rest, :])
    gpl = LANES // S5_GROUP_CH
    nck = y5_sc.shape[1] // S5_CHUNK
    for s in range(S5_CHUNK):
        for half in range(D_S5 // LANES):
            y5_sc[half, pl.ds(s, nck, stride=S5_CHUNK), :] = jnp.concatenate(
                [s5_ref[half * gpl + j, :, s * S5_GROUP_CH:(s + 1) * S5_GROUP_CH] for j in range(gpl)], axis=1)
    y5 = jnp.concatenate([y5_sc[half] for half in range(D_S5 // LANES)], axis=1)
    t = jax.nn.gelu(y5 + d5_ref[...] * u_ref[...])
    t = _dot(t.astype(BF16), wglu_ref[...]) + bglu_ref[...]
    s5o = t[:, :D_S5] * jax.nn.sigmoid(t[:, D_S5:])
    o_ref[...] = x_ref[...] + gt_ref[0] * (part + _dot(s5o.astype(BF16), wout_ref[n_rest:, :]))


def _outproj_call(x2, gate, yf, yb, z, ga, wa, s5, u, ng, d5, wglu, bglu, wout, *, tm, rows_per_seg):
    r, d = x2.shape
    tps = rows_per_seg // tm
    row = lambda a: pl.BlockSpec((tm, a.shape[1]), lambda i: (i, 0))
    seg = pl.BlockSpec((1, 1, d), lambda i: (i // tps, 0, 0))
    full = lambda a: pl.BlockSpec(a.shape, lambda i: (0,) * a.ndim)
    s5_spec = pl.BlockSpec((s5.shape[0], tm // S5_CHUNK, s5.shape[2]), lambda i: (0, i, 0))
    return pl.pallas_call(
        _outproj_kernel,
        grid=(r // tm,),
        in_specs=[row(x2), seg] + [row(a) for a in (yf, yb, z, ga, wa)] + [s5_spec, row(u)]
                 + [full(a) for a in (ng, d5, wglu, bglu, wout)],
        out_specs=row(x2),
        out_shape=jax.ShapeDtypeStruct(x2.shape, F32),
        scratch_shapes=[pltpu.VMEM((D_S5 // LANES, tm, LANES), F32)],
        compiler_params=_params(("arbitrary",)),
        name="outproj",
    )(x2, gate, yf, yb, z, ga, wa, s5, u, ng, d5, wglu, bglu, wout)


def _ffn_kernel(x_ref, xp_ref, xn_ref, sh_ref, sc_ref, gt_ref, g_ref, wgu_ref, cw_ref, cb_ref,
                wd_ref, gfin_ref, o_ref, *, tiles_per_seq, d_ff, final_norm):
    i = pl.program_id(0)
    tm = x_ref.shape[0]
    x = x_ref[...]
    g, sh, sc = g_ref[...], sh_ref[0], sc_ref[0]
    h = _rms_mod(x, g, sh, sc)
    he = jnp.concatenate([_rms_mod(xp_ref[...], g, sh, sc), h, _rms_mod(xn_ref[...], g, sh, sc)], axis=0)
    ge = _dot(he.astype(BF16), wgu_ref[:, :d_ff])
    upj = _dot(h.astype(BF16), wgu_ref[:, d_ff:])
    t = i % tiles_per_seq
    row = lax.broadcasted_iota(jnp.int32, (tm, 1), 0)
    up = jnp.where((row == 0) & (t == 0), 0.0, ge[HALO - 1:HALO - 1 + tm])
    dn = jnp.where((row == tm - 1) & (t == tiles_per_seq - 1), 0.0, ge[HALO + 1:HALO + 1 + tm])
    cw = cw_ref[...]
    conv = cw[0:1] * up + cw[1:2] * ge[HALO:HALO + tm] + cw[2:3] * dn + cb_ref[...]
    act = (_silu(conv) * upj).astype(BF16)
    y = x + gt_ref[0] * _dot(act, wd_ref[...])
    if final_norm:
        y = y * lax.rsqrt(jnp.mean(y * y, axis=-1, keepdims=True) + NORM_EPS) * gfin_ref[...]
    o_ref[...] = y


def _ffn_call(x2, shift, scale, gate, g, wgu, cw, cb, wd, gfin, *, tm, rows_per_seg, seq, final_norm):
    r, d = x2.shape
    d_ff = wd.shape[0]
    tps = rows_per_seg // tm
    hb = tm // HALO
    last = r // HALO - 1
    seg = pl.BlockSpec((1, 1, d), lambda i: (i // tps, 0, 0))
    full = lambda a: pl.BlockSpec(a.shape, lambda i: (0,) * a.ndim)
    once = lambda a: pl.BlockSpec(a.shape, lambda i: (0,) * a.ndim, pipeline_mode=pl.Buffered(1))
    return pl.pallas_call(
        functools.partial(_ffn_kernel, tiles_per_seq=seq // tm, d_ff=d_ff, final_norm=final_norm),
        grid=(r // tm,),
        in_specs=[pl.BlockSpec((tm, d), lambda i: (i, 0)),
                  pl.BlockSpec((HALO, d), lambda i: (jnp.maximum(i * hb - 1, 0), 0)),
                  pl.BlockSpec((HALO, d), lambda i: (jnp.minimum(i * hb + hb, last), 0)),
                  seg, seg, seg, full(g), once(wgu), full(cw), full(cb), once(wd), full(gfin)],
        out_specs=pl.BlockSpec((tm, d), lambda i: (i, 0)),
        out_shape=jax.ShapeDtypeStruct(x2.shape, F32),
        compiler_params=_params(("arbitrary",)),
        name="conv_ffn",
    )(x2, x2, x2, shift, scale, gate, g, wgu, cw, cb, wd, gfin)


def _rope_tables(seq):
    rows = seq // GRID_W
    r = jnp.broadcast_to(jnp.arange(rows, dtype=F32)[:, None], (rows, GRID_W)).reshape(-1)
    c = jnp.broadcast_to(jnp.arange(GRID_W, dtype=F32)[None, :], (rows, GRID_W)).reshape(-1)
    n_freq = HEAD_DIM // 4
    inv = ROPE_THETA ** (-jnp.arange(n_freq, dtype=F32) / n_freq)
    ang = jnp.concatenate([r[:, None] * inv, c[:, None] * inv], axis=-1)
    cos, sin, zero = jnp.cos(ang), jnp.sin(ang), jnp.zeros_like(ang)
    reps = LANES // HEAD_DIM
    return (jnp.tile(jnp.concatenate([cos, cos], axis=-1), (1, reps)),
            jnp.tile(jnp.concatenate([-sin, zero], axis=-1), (1, reps)),
            jnp.tile(jnp.concatenate([zero, sin], axis=-1), (1, reps)))


def _pad_w_in(w):
    offs = [0]
    for s in IN_SIZES:
        offs.append(offs[-1] + s)
    parts = [w[:, offs[j]:offs[j + 1]] for j in range(len(IN_SIZES))]
    parts[2] = jnp.pad(parts[2], ((0, 0), (0, LANES - IN_SIZES[2])))
    return jnp.concatenate(parts, axis=1).astype(BF16)


def kernel(x, c, ctx, c_ctx, w_mod, b_mod, g_mix, w_in, ssd_conv_w, ssd_conv_b, ssd_a_log, ssd_dt_bias, ssd_d, ssd_norm_g, ga_q_norm, ga_k_norm, wa_sink, s5_lambda_re, s5_lambda_im, s5_log_step, s5_b_re, s5_b_im, s5_c_re, s5_c_im, s5_d, s5_w_glu, s5_b_glu, w_out, g_ffn, w_gate, w_up, ffn_conv_w, ffn_conv_b, w_down, g_final):
    b, s, d = x.shape
    lc = ctx.shape[1]
    depth = w_mod.shape[0]
    hpg = 2
    tm = 512

    cc = jnp.zeros((8, d), F32).at[:b].set(c).at[b].set(c_ctx)
    mod = _mod_call(cc, w_mod, b_mod).reshape(depth, 8, 6, d)

    tabs_l = _rope_tables(s)
    one, zero = jnp.ones((lc, LANES), F32), jnp.zeros((lc, LANES), F32)
    tabs_c = (one, zero, zero)
    bd = jnp.kron(jnp.eye(LANES // HEAD_DIM, dtype=F32),
                  jnp.full((HEAD_DIM, HEAD_DIM), 1.0 / HEAD_DIM, F32)).astype(BF16)
    gfin = g_final.reshape(1, d)

    x2 = x.reshape(b * s, d)
    xc2 = ctx.reshape(b * lc, d)
    for i in range(depth):
        need_ctx = i < depth - 1
        ml = lambda j: mod[i, :b, j].reshape(b, 1, d)
        mc = lambda j: mod[i, b, j].reshape(1, 1, d)
        g_i = g_mix[i].reshape(1, d)
        w_pad = _pad_w_in(w_in[i])
        qn = jnp.tile(ga_q_norm[i], LANES // HEAD_DIM).reshape(1, LANES)
        kn = jnp.tile(ga_k_norm[i], LANES // HEAD_DIM).reshape(1, LANES)
        conv_w = ssd_conv_w[i]
        conv_b = ssd_conv_b[i].reshape(1, -1)
        pl_ = _inproj_call(x2, ml(0), ml(1), g_i, w_pad, tabs_l, qn, kn, bd, conv_w, conv_b,
                           tm=tm, rows_per_seg=s, seq=s)
        pc_ = _inproj_call(xc2, mc(0), mc(1), g_i, w_pad, tabs_c, qn, kn, bd, conv_w, conv_b,
                           tm=lc, rows_per_seg=b * lc, seq=lc)
        zl, xsl, btl, cml, dtl, gql, gkl, gvl, wql, wkl, wvl, ul, u5l = pl_
        zc, xsc, btc, cmc, dtc, gqc, gkc, gvc, wqc, wkc, wvc, uc, u5c = pc_

        alog_c = ssd_a_log[i].reshape(-1, 1)
        bias_c = ssd_dt_bias[i].reshape(-1, 1)
        dsk = jnp.repeat(ssd_d[i], HEAD_DIM).reshape(1, D_SSD)

        def ssd(xs2, bt, cm2, dt_t, h0, n):
            return _ssd_call(xs2.reshape(b, n, -1), bt, cm2.reshape(b, n, -1), dt_t, h0, alog_c, bias_c, dsk)

        h0 = jnp.zeros((b, N_DIRS, SSD_GROUPS, SSD_STATE, hpg * HEAD_DIM), F32)
        yfc, ybc, hc = ssd(xsc, btc, cmc, dtc, h0, lc)
        yfl, ybl, _ = ssd(xsl, btl, cml, dtl, hc, s)

        kx_ga = gkc.reshape(b, lc, -1)
        y_ga = _flash_call(gql, gkl.reshape(b, s, -1), gvl, None, extra=(kx_ga, gvc))

        kx_wa = wkc.reshape(b, lc, -1)
        sink = wa_sink[i].astype(F32)
        y_wa = _win_call(wql, wkl.reshape(b, s, -1), wvl, kx_wa, wvc, sink)

        gm = lambda a: jnp.moveaxis(a, 1, 0)
        lre = gm(s5_lambda_re[i])[:, :, None, :]
        lim = gm(s5_lambda_im[i])[:, :, None, :]
        ls = gm(s5_log_step[i])[:, :, None, None]
        bre = gm(s5_b_re[i]).transpose(0, 1, 3, 2)
        bim = gm(s5_b_im[i]).transpose(0, 1, 3, 2)
        cre, cim = gm(s5_c_re[i]), gm(s5_c_im[i])
        y5l, y5c = _s5_call(u5l, u5c, lre, lim, ls, bre, bim, cre, cim, nb=b)

        ng = ssd_norm_g[i].reshape(1, D_SSD)
        d5 = s5_d[i].reshape(1, D_S5)
        wglu = s5_w_glu[i].astype(BF16)
        bglu = s5_b_glu[i].reshape(1, -1)
        wout = w_out[i].astype(BF16)
        x2 = _outproj_call(x2, ml(2), yfl.reshape(b * s, -1), ybl.reshape(b * s, -1), zl, y_ga, y_wa,
                           y5l, ul, ng, d5, wglu, bglu, wout, tm=tm, rows_per_seg=s)

        gf_i = g_ffn[i].reshape(1, d)
        wgu = jnp.concatenate([w_gate[i], w_up[i]], axis=1).astype(BF16)
        cw = ffn_conv_w[i]
        cb = ffn_conv_b[i].reshape(1, -1)
        wd = w_down[i].astype(BF16)
        x2 = _ffn_call(x2, ml(3), ml(4), ml(5), gf_i, wgu, cw, cb, wd, gfin, tm=512, rows_per_seg=s,
                       seq=s, final_norm=not need_ctx)

        if need_ctx:
            yc_ga = _flash_call(gqc, kx_ga, gvc, None)
            yc_wa = _flash_call(wqc, kx_wa, wvc, sink)
            xc2 = _outproj_call(xc2, mc(2), yfc.reshape(b * lc, -1), ybc.reshape(b * lc, -1), zc, yc_ga,
                                yc_wa, y5c, uc, ng, d5, wglu, bglu, wout, tm=lc, rows_per_seg=b * lc)
            xc2 = _ffn_call(xc2, mc(3), mc(4), mc(5), gf_i, wgu, cw, cb, wd, gfin, tm=lc,
                            rows_per_seg=b * lc, seq=lc, final_norm=False)
    return x2.reshape(b, s, d)
```

```python
import functools

import jax
import jax.numpy as jnp
from jax import lax
from jax.experimental import pallas as pl
from jax.experimental.pallas import tpu as pltpu

F32 = jnp.float32
BF16 = jnp.bfloat16
HIGHEST = lax.Precision.HIGHEST

HEAD_DIM = 64
GRID_W = 64
ROPE_THETA = 10000.0
NORM_EPS = 1e-6
WINDOW = 128
N_DIRS = 2
SSD_HEADS = 4
SSD_GROUPS = 2
SSD_STATE = 128
SSD_CHUNK = 128
SSD_CHUNKS_PER_STEP = 8
D_SSD = SSD_HEADS * HEAD_DIM
SSD_XBC = D_SSD + 2 * SSD_GROUPS * SSD_STATE
S5_GROUPS = 16
S5_GROUP_CH = 16
S5_STATE = 64
S5_MAX_RE = -1e-4
S5_CHUNK = 32
D_S5 = S5_GROUPS * S5_GROUP_CH
LANES = 128
HALO = 8
NEG_BIG = -1e30
LOG2E = 1.4426950408889634
Q_HEADS = 4
KV_HEADS = 2
HPG = Q_HEADS // KV_HEADS
ONES_ROWS = 16
FLASH_SUB_K = 256
FLASH_SUB_Q = 256
FLASH_DEPTH = 6
VMEM_LIMIT = 52 * 1024 * 1024

IN_SIZES = (D_SSD, SSD_XBC, N_DIRS * SSD_HEADS, 256, 128, 128, 256, 128, 128, D_S5)
P_Z, P_XBC, P_DT, P_GQ, P_GK, P_GV, P_WQ, P_WK, P_WV, P_U, P_END = (
    0, 256, 1024, 1152, 1408, 1536, 1664, 1920, 2048, 2176, 2432)


def _params(sem=None, flags=None):
    return pltpu.CompilerParams(dimension_semantics=sem, vmem_limit_bytes=VMEM_LIMIT, flags=flags)


def _silu(v):
    return v * jax.nn.sigmoid(v)


def _softplus(v):
    return jnp.maximum(v, 0.0) + jnp.log1p(jnp.exp(-jnp.abs(v)))


def _dot(a, b):
    return jnp.dot(a, b, preferred_element_type=F32)


def _dot_nt(a, b, precision=None):
    return lax.dot_general(a, b, (((1,), (1,)), ((), ())), preferred_element_type=F32,
                           precision=precision)


def _dot_tn(a, b):
    return lax.dot_general(a, b, (((0,), (0,)), ((), ())), preferred_element_type=F32)


def _mod_kernel(cc_ref, w_ref, b_ref, o_ref):
    s = _silu(cc_ref[...])
    o_ref[0] = jnp.dot(s, w_ref[0], preferred_element_type=F32, precision=HIGHEST) + b_ref[0]


def _mod_call(cc, w_mod, b_mod):
    n_layers, d, n = w_mod.shape
    tn = 1536
    return pl.pallas_call(
        _mod_kernel,
        grid=(n_layers, n // tn),
        in_specs=[pl.BlockSpec((8, d), lambda l, j: (0, 0)),
                  pl.BlockSpec((1, d, tn), lambda l, j: (l, 0, j)),
                  pl.BlockSpec((1, 1, tn), lambda l, j: (l, 0, j))],
        out_specs=pl.BlockSpec((1, 8, tn), lambda l, j: (l, 0, j)),
        out_shape=jax.ShapeDtypeStruct((n_layers, 8, n), F32),
        compiler_params=_params(("arbitrary", "arbitrary")),
        name="adaln_mod",
    )(cc, w_mod, b_mod.reshape(n_layers, 1, n))


def _rms_mod(x, g, shift, scale):
    y = x * lax.rsqrt(jnp.mean(x * x, axis=-1, keepdims=True) + NORM_EPS) * g
    return y * (1.0 + scale) + shift


def _rope(t, cos, sna, snb):
    return t * cos + pltpu.roll(t, 96, 1) * sna + pltpu.roll(t, 32, 1) * snb


def _head_rms(t, gain, bd):
    t2 = t * t
    hi = t2.astype(BF16)
    lo = (t2 - hi.astype(F32)).astype(BF16)
    ms = _dot(hi, bd) + _dot(lo, bd)
    return t * lax.rsqrt(ms + NORM_EPS) * gain


def _conv3_silu(v, prev, nxt, cw, cb):
    n = v.shape[0]
    row = lax.broadcasted_iota(jnp.int32, (n, 1), 0)
    up = jnp.where(row == 0, prev, pltpu.roll(v, 1, 0))
    dn = jnp.where(row == n - 1, nxt, pltpu.roll(v, n - 1, 0))
    return _silu(cw[0:1] * up + cw[1:2] * v + cw[2:3] * dn + cb)


def _inproj_kernel(x_ref, xp_ref, xn_ref, sh_ref, sc_ref, g_ref, w_ref, cos_ref, sna_ref, snb_ref,
                   qn_ref, kn_ref, bd_ref, cw_ref, cb_ref, z_ref, xs_ref, bt_ref, cm_ref, dt_ref, gq_ref, gk_ref,
                   gv_ref, wq_ref, wk_ref, wv_ref, u_ref, u5_ref, u_sc, *, tiles_per_seq):
    g, sh, sc = g_ref[...], sh_ref[0], sc_ref[0]
    hb = _rms_mod(x_ref[...], g, sh, sc).astype(BF16)
    halo = jnp.concatenate([xp_ref[...], xn_ref[...]], axis=0)
    u = _dot(hb, w_ref[:, P_U:P_END])
    p = _dot(hb, w_ref[:, P_Z:P_GQ])
    ph = _dot(_rms_mod(halo, g, sh, sc).astype(BF16), w_ref[:, P_XBC:P_DT])
    pg = _dot(hb, w_ref[:, P_GQ:P_WQ])
    pw = _dot(hb, w_ref[:, P_WQ:P_U])
    u_ref[...] = u
    gpl = LANES // S5_GROUP_CH
    for half in range(D_S5 // LANES):
        u_sc[half] = u[:, half * LANES:(half + 1) * LANES]
    for s in range(S5_CHUNK):
        for half in range(D_S5 // LANES):
            rows = u_sc[half, pl.ds(s, u_sc.shape[1] // S5_CHUNK, stride=S5_CHUNK), :]
            for j in range(gpl):
                u5_ref[half * gpl + j, :, s * S5_GROUP_CH:(s + 1) * S5_GROUP_CH] = (
                    rows[:, j * S5_GROUP_CH:(j + 1) * S5_GROUP_CH])
    cos, sna, snb = cos_ref[...], sna_ref[...], snb_ref[...]
    bd = bd_ref[...]
    q_scale = HEAD_DIM ** -0.5 * LOG2E
    z_ref[...] = p[:, P_Z:P_XBC].astype(z_ref.dtype)
    t = pl.program_id(0) % tiles_per_seq
    prev = jnp.where(t > 0, ph[HALO - 1:HALO, :], 0.0)
    nxt = jnp.where(t < tiles_per_seq - 1, ph[HALO:HALO + 1, :], 0.0)
    act = _conv3_silu(p[:, P_XBC:P_DT], prev, nxt, cw_ref[...], cb_ref[...])
    xs_ref[...] = act[:, :D_SSD]
    n_bc = SSD_GROUPS * SSD_STATE
    bt_ref[0] = jnp.transpose(act[:, D_SSD:D_SSD + n_bc]).astype(BF16)
    cm_ref[...] = act[:, D_SSD + n_bc:].astype(BF16)
    dt_ref[0] = jnp.transpose(p[:, P_DT:P_GQ])[:dt_ref.shape[1]]
    for j in range(2):
        q = _rope(_head_rms(pg[:, j * LANES:(j + 1) * LANES], qn_ref[...], bd), cos, sna, snb) * q_scale
        gq_ref[:, j * LANES:(j + 1) * LANES] = q.astype(BF16)
    gk_ref[...] = _rope(_head_rms(pg[:, P_GK - P_GQ:P_GV - P_GQ], kn_ref[...], bd), cos, sna, snb).astype(BF16)
    gv_ref[0] = jnp.transpose(pg[:, P_GV - P_GQ:]).astype(BF16)
    for j in range(2):
        q = _rope(pw[:, j * LANES:(j + 1) * LANES], cos, sna, snb) * q_scale
        wq_ref[:, j * LANES:(j + 1) * LANES] = q.astype(BF16)
    wk_ref[...] = _rope(pw[:, P_WK - P_WQ:P_WV - P_WQ], cos, sna, snb).astype(BF16)
    wv_ref[0] = jnp.transpose(pw[:, P_WV - P_WQ:]).astype(BF16)


def _inproj_call(x2, shift, scale, g, w_pad, tabs, qn, kn, bd, cw, cb, *, tm, rows_per_seg, seq):
    r, d = x2.shape
    tps = rows_per_seg // tm
    tpq = seq // tm
    hb = tm // HALO
    last = r // HALO - 1
    row = lambda w: pl.BlockSpec((tm, w), lambda i: (i, 0))
    seg = pl.BlockSpec((1, 1, d), lambda i: (i // tps, 0, 0))
    full = lambda a: pl.BlockSpec(a.shape, lambda i: (0,) * a.ndim)
    tab = pl.BlockSpec((tm, LANES), lambda i: (i % tpq, 0))
    n_bc = SSD_GROUPS * SSD_STATE
    widths = (256, D_SSD, -n_bc, n_bc, -IN_SIZES[2], 256, 128, -LANES, 256, 128, -LANES, D_S5)
    dtypes = (BF16, F32, BF16, BF16, F32, BF16, BF16, BF16, BF16, BF16, BF16, F32)
    spec = lambda w: row(w) if w > 0 else pl.BlockSpec((1, -w, tm), lambda i: (i // tpq, 0, i % tpq))
    shape = lambda w: (r, w) if w > 0 else (r // seq, -w, seq)
    w5 = S5_CHUNK * S5_GROUP_CH
    u5_spec = pl.BlockSpec((S5_GROUPS, tm // S5_CHUNK, w5), lambda i: (0, i, 0))
    return pl.pallas_call(
        functools.partial(_inproj_kernel, tiles_per_seq=tpq),
        grid=(r // tm,),
        in_specs=[row(d),
                  pl.BlockSpec((HALO, d), lambda i: (jnp.maximum(i * hb - 1, 0), 0)),
                  pl.BlockSpec((HALO, d), lambda i: (jnp.minimum(i * hb + hb, last), 0)),
                  seg, seg, full(g), full(w_pad), tab, tab, tab, full(qn), full(kn), full(bd),
                  full(cw), full(cb)],
        out_specs=[spec(w) for w in widths] + [u5_spec],
        out_shape=[jax.ShapeDtypeStruct(shape(w), t) for w, t in zip(widths, dtypes)]
                  + [jax.ShapeDtypeStruct((S5_GROUPS, r // S5_CHUNK, w5), F32)],
        scratch_shapes=[pltpu.VMEM((D_S5 // LANES, tm, LANES), F32)],
        compiler_params=_params(("arbitrary",)),
        name="inproj",
    )(x2, x2, x2, shift, scale, g, w_pad, *tabs, qn, kn, bd, cw, cb)


def _ssd_kernel(xsf_ref, btf_ref, cmf_ref, xsb_ref, btb_ref, cmb_ref, dtrf_ref, dtrb_ref, h0_ref, alc_ref,
                bic_ref, dsk_ref, yf_ref, yb_ref, hout_ref, h_sc, *, nc):
    c = pl.program_id(1)

    @pl.when(c == 0)
    def _():
        h_sc[...] = h0_ref[0]

    q = SSD_CHUNK
    nsub = xsf_ref.shape[1] // q
    hpg = SSD_HEADS // SSD_GROUPS
    xs_refs, dtr_refs, y_refs = (xsf_ref, xsb_ref), (dtrf_ref, dtrb_ref), (yf_ref, yb_ref)
    bt_refs, cm_refs = (btf_ref, btb_ref), (cmf_ref, cmb_ref)
    order = (tuple(range(nsub)), tuple(reversed(range(nsub))))
    tok = lambda j: slice(j * q, (j + 1) * q)
    bmat_t = lambda d, j, g: bt_refs[d][0, g * SSD_STATE:(g + 1) * SSD_STATE, tok(j)]
    cmat = lambda d, j, g: cm_refs[d][0, tok(j), g * SSD_STATE:(g + 1) * SSD_STATE]
    ri = lax.broadcasted_iota(jnp.int32, (q, q), 0)
    ci = lax.broadcasted_iota(jnp.int32, (q, q), 1)
    mask = (ri >= ci, ri <= ci)
    lane = lax.broadcasted_iota(jnp.int32, (1, LANES), 1)
    first = lane < HEAD_DIM
    dgs = [(d, g) for d in range(N_DIRS) for g in range(SSD_GROUPS)]
    items = [(d, j, g) for j in range(nsub) for d, g in dgs]

    h = {dg: h_sc[dg[0], dg[1]] for dg in dgs}
    yoff0 = {(d, g): _dot(cmat(d, order[d][0], g), h[d, g].astype(BF16)) for d, g in dgs}
    cb = {(d, j, g): _dot(cmat(d, j, g), bmat_t(d, j, g)) for d, j, g in items}

    dt_r, acs_c, acs_r, dt_c, tot = {}, {}, {}, {}, {}
    for j in range(nsub):
        for d in range(N_DIRS):
            dt_r[d, j] = _softplus(dtr_refs[d][0, :, tok(j)] + bic_ref[...])
    for j in range(nsub):
        for d in range(N_DIRS):
            dta_r = dt_r[d, j] * (-jnp.exp(alc_ref[...]))
            maskf = mask[d].astype(F32)
            pad = jnp.zeros((LANES - dta_r.shape[0], q), F32)
            acs_c[d, j] = _dot_nt(maskf, jnp.concatenate([dta_r, pad], axis=0), precision=HIGHEST)
            acs_r[d, j] = _dot_nt(dta_r, maskf, precision=HIGHEST)
            dt_c[d, j] = jnp.transpose(jnp.concatenate([dt_r[d, j], pad], axis=0))
            tot[d, j] = acs_c[d, j][q - 1:q] if d == 0 else acs_c[d, j][0:1]

    ydiag, acs_g, tot_g, st = {}, {}, {}, {}
    for d, j, g in items:
        e0 = d * SSD_HEADS + g * hpg
        pick = lambda v: jnp.where(first, v[:, e0:e0 + 1], v[:, e0 + 1:e0 + 2])
        acs_g[d, j, g], tot_g[d, j, g] = pick(acs_c[d, j]), pick(tot[d, j])
        xs_g = xs_refs[d][0, tok(j), g * LANES:(g + 1) * LANES]
        xdt = xs_g * pick(dt_c[d, j])
        xdt_b = xdt.astype(BF16)
        yd = []
        for a in range(hpg):
            e = e0 + a
            dec = jnp.where(mask[d], jnp.exp(acs_c[d, j][:, e:e + 1] - acs_r[d, j][e:e + 1, :]), 0.0)
            yd.append(_dot((cb[d, j, g] * dec).astype(BF16), xdt_b))
        y = jnp.where(first, yd[0], yd[1])
        if d == 0:
            y = y + dsk_ref[:, g * LANES:(g + 1) * LANES] * xs_g
        ydiag[d, j, g] = y
        w = (xdt * jnp.exp(tot_g[d, j, g] - acs_g[d, j, g])).astype(BF16)
        st[d, j, g] = _dot(bmat_t(d, j, g), w)

    for idx in range(nsub):
        for d, g in dgs:
            j = order[d][idx]
            yoff = yoff0[d, g] if idx == 0 else _dot(cmat(d, j, g), h[d, g].astype(BF16))
            y = ydiag[d, j, g] + yoff * jnp.exp(acs_g[d, j, g])
            y_refs[d][0, tok(j), g * LANES:(g + 1) * LANES] = y.astype(y_refs[d].dtype)
            h[d, g] = h[d, g] * jnp.exp(tot_g[d, j, g]) + st[d, j, g]
    for d, g in dgs:
        h_sc[d, g] = h[d, g]

    @pl.when(c == nc - 1)
    def _():
        hout_ref[0] = h_sc[...]


def _ssd_call(xs, bt, cm, dt_t, h0, alog_c, bias_c, dsk):
    b, s, _ = xs.shape
    q = SSD_CHUNK * min(SSD_CHUNKS_PER_STEP, s // SSD_CHUNK)
    nc = s // q
    fwd = lambda c: c
    bwd = lambda c: nc - 1 - c
    rows = lambda a, f: pl.BlockSpec((1, q, a.shape[2]), lambda i, c: (i, f(c), 0))
    cols = lambda a, f: pl.BlockSpec((1, a.shape[1], q), lambda i, c: (i, 0, f(c)))
    drspec = lambda f: cols(dt_t, f)
    full = lambda a: pl.BlockSpec(a.shape, lambda i, c: (0,) * a.ndim)
    hspec = pl.BlockSpec((1,) + h0.shape[1:], lambda i, c: (i, 0, 0, 0, 0))
    return pl.pallas_call(
        functools.partial(_ssd_kernel, nc=nc),
        grid=(b, nc),
        in_specs=[rows(xs, fwd), cols(bt, fwd), rows(cm, fwd), rows(xs, bwd), cols(bt, bwd), rows(cm, bwd),
                  drspec(fwd), drspec(bwd), hspec, full(alog_c), full(bias_c), full(dsk)],
        out_specs=[rows(xs, fwd), rows(xs, bwd), hspec],
        out_shape=[jax.ShapeDtypeStruct(xs.shape, BF16), jax.ShapeDtypeStruct(xs.shape, BF16),
                   jax.ShapeDtypeStruct(h0.shape, F32)],
        scratch_shapes=[pltpu.VMEM(h0.shape[1:], F32)],
        compiler_params=_params(("arbitrary", "arbitrary")),
        name="ssd",
    )(xs, bt, cm, xs, bt, cm, dt_t, dt_t, h0, alog_c, bias_c, dsk)


def _q_transposed(q):
    qt = jnp.transpose(q.astype(F32))
    zero = jnp.zeros((HEAD_DIM, q.shape[0]), BF16)
    out = []
    for h in range(Q_HEADS):
        blk = qt[h * HEAD_DIM:(h + 1) * HEAD_DIM].astype(BF16)
        out.append(jnp.concatenate([blk, zero] if h // HPG == 0 else [zero, blk], axis=0))
    return out


def _flash_tiles(srcs, qt_sc, m_sc, acc_sc):
    tq = qt_sc.shape[2]
    ones = jnp.ones((ONES_ROWS, FLASH_SUB_K), BF16)
    chains = [(h, slice(jq * FLASH_SUB_Q, (jq + 1) * FLASH_SUB_Q))
              for h in range(Q_HEADS) for jq in range(tq // FLASH_SUB_Q)]
    m = [m_sc[h, :, cs] for h, cs in chains]
    acc = [acc_sc[h, :, cs] for h, cs in chains]
    units = [(src, jk, c) for src in srcs for jk in range(src[0].shape[1] // FLASH_SUB_K)
             for c in range(len(chains))]

    def scores(u):
        (k_ref, _), jk, c = u
        h, cs = chains[c]
        return _dot(k_ref[0, jk * FLASH_SUB_K:(jk + 1) * FLASH_SUB_K, :], qt_sc[h, :, cs])

    def finish(u, s):
        (_, vt_ref), jk, c = u
        g = chains[c][0] // HPG
        vte = jnp.concatenate(
            [vt_ref[0, g * HEAD_DIM:(g + 1) * HEAD_DIM, jk * FLASH_SUB_K:(jk + 1) * FLASH_SUB_K], ones], axis=0)
        m_new = jnp.maximum(m[c], jnp.max(s, axis=0, keepdims=True))
        p = jnp.exp2(s - m_new).astype(BF16)
        acc[c] = jnp.exp2(m[c] - m_new) * acc[c] + _dot(vte, p)
        m[c] = m_new

    pending = [scores(u) for u in units[:FLASH_DEPTH]]
    for i, u in enumerate(units):
        s_cur = pending.pop(0)
        if i + FLASH_DEPTH < len(units):
            pending.append(scores(units[i + FLASH_DEPTH]))
        finish(u, s_cur)
    for c, (h, cs) in enumerate(chains):
        m_sc[h, :, cs] = m[c]
        acc_sc[h, :, cs] = acc[c]


def _flash_kernel(sink_ref, q_ref, k_ref, vt_ref, *rest, nk, has_sink, has_extra):
    if has_extra:
        kx_ref, vtx_ref, o_ref, qt_sc, m_sc, acc_sc = rest
    else:
        o_ref, qt_sc, m_sc, acc_sc = rest
    ki = pl.program_id(2)
    tq = q_ref.shape[0]

    def init():
        qts = _q_transposed(q_ref[...])
        for h in range(Q_HEADS):
            qt_sc[h] = qts[h]
            if has_sink:
                m_sc[h] = jnp.full(m_sc.shape[1:], sink_ref[h] * LOG2E, F32)
                acc_sc[h] = jnp.concatenate([jnp.zeros((HEAD_DIM, tq), F32), jnp.ones((ONES_ROWS, tq), F32)], axis=0)
            else:
                m_sc[h] = jnp.full(m_sc.shape[1:], NEG_BIG, F32)
                acc_sc[h] = jnp.zeros(acc_sc.shape[1:], F32)

    if has_extra:
        @pl.when(ki == 0)
        def _():
            init()
            _flash_tiles([(kx_ref, vtx_ref), (k_ref, vt_ref)], qt_sc, m_sc, acc_sc)

        @pl.when(ki > 0)
        def _():
            _flash_tiles([(k_ref, vt_ref)], qt_sc, m_sc, acc_sc)
    else:
        pl.when(ki == 0)(init)
        _flash_tiles([(k_ref, vt_ref)], qt_sc, m_sc, acc_sc)

    @pl.when(ki == nk - 1)
    def _():
        outs = []
        for h in range(Q_HEADS):
            acc = acc_sc[h]
            outs.append(acc[:HEAD_DIM] / acc[HEAD_DIM:HEAD_DIM + 1])
        o_ref[...] = jnp.transpose(jnp.concatenate(outs, axis=0)).astype(o_ref.dtype)


def _pick(n, cands):
    for c in cands:
        if n % c == 0:
            return c
    return n


def _flash_call(q2, k, vt, sink, extra=None):
    b, sk, kw = k.shape
    sq = q2.shape[0] // b
    tq = _pick(sq, (512, 256))
    tk = _pick(sk, (4096, 1024, 512, 256))
    nq = sq // tq
    nk = sk // tk
    has_sink = sink is not None
    if sink is None:
        sink = jnp.zeros((Q_HEADS,), F32)
    in_specs = [pl.BlockSpec(memory_space=pltpu.SMEM),
                pl.BlockSpec((tq, q2.shape[1]), lambda i, qi, ki: (i * nq + qi, 0)),
                pl.BlockSpec((1, tk, kw), lambda i, qi, ki: (i, ki, 0)),
                pl.BlockSpec((1, kw, tk), lambda i, qi, ki: (i, 0, ki))]
    args = [sink, q2, k, vt]
    if extra is not None:
        lx = extra[0].shape[1]
        in_specs += [pl.BlockSpec((1, lx, kw), lambda i, qi, ki: (i, 0, 0)),
                     pl.BlockSpec((1, kw, lx), lambda i, qi, ki: (i, 0, 0))]
        args += list(extra)
    return pl.pallas_call(
        functools.partial(_flash_kernel, nk=nk, has_sink=has_sink, has_extra=extra is not None),
        grid=(b, nq, nk),
        in_specs=in_specs,
        out_specs=pl.BlockSpec((tq, q2.shape[1]), lambda i, qi, ki: (i * nq + qi, 0)),
        out_shape=jax.ShapeDtypeStruct(q2.shape, BF16),
        scratch_shapes=[pltpu.VMEM((Q_HEADS, kw, tq), BF16), pltpu.VMEM((Q_HEADS, 1, tq), F32),
                        pltpu.VMEM((Q_HEADS, HEAD_DIM + ONES_ROWS, tq), F32)],
        compiler_params=_params(("arbitrary", "arbitrary", "arbitrary")),
        name="flash_sink" if has_sink else "flash",
    )(*args)


def _win_kernel(sink_ref, q_ref, kp_ref, kc_ref, kn_ref, kx_ref, vtp_ref, vtc_ref, vtn_ref, vtx_ref,
                o_ref, *, nq):
    j = pl.program_id(1)
    tq = q_ref.shape[0]
    blk = kp_ref.shape[1]
    sq = FLASH_SUB_Q
    nqs = tq // sq
    qts = _q_transposed(q_ref[...])
    chains = [(h, jq) for h in range(Q_HEADS) for jq in range(nqs)]

    def band(n, rel0):
        rel = rel0 + lax.broadcasted_iota(jnp.int32, (n, sq), 0) - lax.broadcasted_iota(jnp.int32, (n, sq), 1)
        return (rel <= WINDOW) & (rel >= -WINDOW)

    def chain_units(jq):
        units = [(kx_ref, vtx_ref, slice(0, kx_ref.shape[1]), None)]
        for a in range(nqs):
            units.append((kc_ref, vtc_ref, slice(a * sq, (a + 1) * sq), band(sq, (a - jq) * sq)))
        if jq == 0:
            units.append((kp_ref, vtp_ref, slice(0, blk), band(blk, -blk) & (j > 0)))
        if jq == nqs - 1:
            units.append((kn_ref, vtn_ref, slice(0, blk), band(blk, tq - jq * sq) & (j < nq - 1)))
        return units

    per_chain = [chain_units(jq) for _, jq in chains]
    units = [(c, u) for i in range(max(len(p) for p in per_chain)) for c, p in enumerate(per_chain)
             if i < len(p) for u in (p[i],)]
    m = [jnp.full((1, sq), sink_ref[h] * LOG2E, F32) for h, _ in chains]
    acc = [jnp.concatenate([jnp.zeros((HEAD_DIM, sq), F32), jnp.ones((ONES_ROWS, sq), F32)], axis=0)
           for _ in chains]

    def scores(cu):
        c, (k_ref, _, ks, _) = cu
        h, jq = chains[c]
        return _dot(k_ref[0, ks, :], qts[h][:, jq * sq:(jq + 1) * sq])

    def finish(cu, s):
        c, (_, vt_ref, ks, mask) = cu
        g = chains[c][0] // HPG
        if mask is not None:
            s = jnp.where(mask, s, NEG_BIG)
        vte = jnp.concatenate([vt_ref[0, g * HEAD_DIM:(g + 1) * HEAD_DIM, ks],
                               jnp.ones((ONES_ROWS, ks.stop - ks.start), BF16)], axis=0)
        m_new = jnp.maximum(m[c], jnp.max(s, axis=0, keepdims=True))
        p = jnp.exp2(s - m_new).astype(BF16)
        acc[c] = jnp.exp2(m[c] - m_new) * acc[c] + _dot(vte, p)
        m[c] = m_new

    pending = [scores(u) for u in units[:FLASH_DEPTH]]
    for i, u in enumerate(units):
        s_cur = pending.pop(0)
        if i + FLASH_DEPTH < len(units):
            pending.append(scores(units[i + FLASH_DEPTH]))
        finish(u, s_cur)
    outs = []
    for h in range(Q_HEADS):
        parts = [acc[c] for c, (hh, _) in enumerate(chains) if hh == h]
        outs.append(jnp.concatenate([a[:HEAD_DIM] / a[HEAD_DIM:HEAD_DIM + 1] for a in parts], axis=1))
    o_ref[...] = jnp.transpose(jnp.concatenate(outs, axis=0)).astype(o_ref.dtype)


def _win_call(q2, k, vt, kx, vtx, sink):
    b, s, kw = k.shape
    lc = kx.shape[1]
    blk = WINDOW
    tq = _pick(s, (512, 256, 128))
    nq = s // tq
    r = tq // blk
    nblk = s // blk
    prev = lambda t: jnp.maximum(t * r - 1, 0)
    nxt = lambda t: jnp.minimum(t * r + r, nblk - 1)
    return pl.pallas_call(
        functools.partial(_win_kernel, nq=nq),
        grid=(b, nq),
        in_specs=[pl.BlockSpec(memory_space=pltpu.SMEM),
                  pl.BlockSpec((tq, q2.shape[1]), lambda i, t: (i * nq + t, 0)),
                  pl.BlockSpec((1, blk, kw), lambda i, t: (i, prev(t), 0)),
                  pl.BlockSpec((1, tq, kw), lambda i, t: (i, t, 0)),
                  pl.BlockSpec((1, blk, kw), lambda i, t: (i, nxt(t), 0)),
                  pl.BlockSpec((1, lc, kw), lambda i, t: (i, 0, 0)),
                  pl.BlockSpec((1, kw, blk), lambda i, t: (i, 0, prev(t))),
                  pl.BlockSpec((1, kw, tq), lambda i, t: (i, 0, t)),
                  pl.BlockSpec((1, kw, blk), lambda i, t: (i, 0, nxt(t))),
                  pl.BlockSpec((1, kw, lc), lambda i, t: (i, 0, 0))],
        out_specs=pl.BlockSpec((tq, q2.shape[1]), lambda i, t: (i * nq + t, 0)),
        out_shape=jax.ShapeDtypeStruct(q2.shape, BF16),
        compiler_params=_params(("arbitrary", "arbitrary")),
        name="window_attn",
    )(sink, q2, k, k, k, kx, vt, vt, vt, vtx)


def _rep_rows(p, n):
    return jnp.concatenate([jnp.broadcast_to(p[s:s + 1, :], (n, p.shape[1])) for s in range(p.shape[0])], axis=0)


def _ctab(pr, pi, mr, mi):
    big_l, k = pr.shape[0], mr.shape[0]
    er, ei = _rep_rows(pr, k), _rep_rows(pi, k)
    tr, ti = jnp.tile(mr, (big_l, 1)), jnp.tile(mi, (big_l, 1))
    return er * tr - ei * ti, er * ti + ei * tr


def _cmul_rows(cr, ci, s):
    n = cr.shape[1]
    return jnp.concatenate([cr, cr], axis=1) * s + jnp.concatenate([-ci, ci], axis=1) * pltpu.roll(s, n, 1)


def _seg_scan(x, cr, ci, rowm, nper, reverse):
    rows = x.shape[0]
    s, sh = x, 1
    while sh < nper:
        if reverse:
            shifted, valid = pltpu.roll(s, rows - sh, 0), rowm < nper - sh
        else:
            shifted, valid = pltpu.roll(s, sh, 0), rowm >= sh
        s = s + jnp.where(valid, _cmul_rows(cr, ci, shifted), 0.0)
        cr, ci = cr * cr - ci * ci, 2.0 * cr * ci
        sh *= 2
    return s


def _s5_kernel(ul_ref, ux_ref, lre_ref, lim_ref, ls_ref, bre_ref, bim_ref, cre_ref, cim_ref,
               yl_ref, yx_ref, t_sc, *, nb, ncl, ncx):
    big_l = S5_CHUNK
    k = S5_GROUP_CH
    n = S5_STATE
    w = big_l * k
    tau = lax.broadcasted_iota(jnp.int32, (big_l, 1), 0).astype(F32)
    tabs = []
    for d in range(N_DIRS):
        lr = jnp.minimum(lre_ref[0, d], S5_MAX_RE)
        li = lim_ref[0, d]
        dl = jnp.exp(ls_ref[0, d])
        ar, th = lr * dl, li * dl

        def power(t, ar=ar, th=th):
            mag = jnp.exp(t * ar)
            return mag * jnp.cos(t * th), mag * jnp.sin(t * th)

        lbr, lbi = power(1.0)
        den = lr * lr + li * li
        zr = ((lbr - 1.0) * lr + lbi * li) / den
        zi = (lbi * lr - (lbr - 1.0) * li) / den
        br, bi = bre_ref[0, d], bim_ref[0, d]
        bbr, bbi = zr * br - zi * bi, zr * bi + zi * br
        cr, ci = cre_ref[0, d], cim_ref[0, d]
        tabs.append(dict(power=power, bbr=bbr, bbi=bbi, cr=cr, ci=ci, lam_l=power(float(big_l))))
    f, b = tabs
    e_f = _ctab(*f["power"](big_l - 1.0 - tau), f["bbr"], f["bbi"])
    e_b = _ctab(*b["power"](tau), b["bbr"], b["bbi"])
    g_f = _ctab(*f["power"](tau + 1.0), f["cr"], f["ci"])
    g_b = _ctab(*b["power"](big_l - tau), b["cr"], b["ci"])
    a_b = _ctab(*b["power"](big_l - 1.0 - tau), b["cr"], b["ci"])
    c_f = jnp.tile(f["cr"], (big_l, 1)), jnp.tile(f["ci"], (big_l, 1))
    bb2 = lambda t: jnp.concatenate([t["bbr"], t["bbi"]], axis=1)
    neg = lambda t: jnp.concatenate([t[0], -t[1]], axis=1)
    ka = _dot_nt(bb2(b), neg(a_b), precision=HIGHEST)
    kb = _dot_nt(bb2(f), neg(g_f), precision=HIGHEST)
    kc = _dot_nt(bb2(f), neg(c_f), precision=HIGHEST)
    lane = lax.broadcasted_iota(jnp.int32, (1, w), 1)
    ka = ka + jnp.where(lane >= w - k, kc, 0.0)
    kall = jnp.concatenate([ka, kb], axis=1)
    for s in range(big_l):
        off = (big_l - 1 - s) * k
        t_sc[s * k:(s + 1) * k, :] = kall[:, off:off + w].astype(BF16)
    ecat = jnp.concatenate([e_f[0], e_f[1], e_b[0], e_b[1]], axis=1).astype(BF16)
    gcat = jnp.concatenate([g_f[0], -g_f[1], g_b[0], -g_b[1]], axis=1).astype(BF16)
    tmat = t_sc[...]

    def rowmod(nper):
        return jnp.concatenate([lax.broadcasted_iota(jnp.int32, (nper, 1), 0)] * nb, axis=0)

    def by_batch(rows, nper):
        return jnp.concatenate([jnp.broadcast_to(r, (nper, r.shape[1])) for r in rows], axis=0)

    ux = ux_ref[0].astype(BF16)
    hx = _dot(ux, ecat)
    rmx = rowmod(ncx)
    rx = nb * ncx
    sxf = _seg_scan(hx[:, :2 * n], *f["lam_l"], rmx, ncx, False)
    sxb = _seg_scan(hx[:, 2 * n:], *b["lam_l"], rmx, ncx, True)
    hin_xf = jnp.where(rmx >= 1, pltpu.roll(sxf, 1, 0), 0.0)
    hin_xb = jnp.where(rmx < ncx - 1, pltpu.roll(sxb, rx - 1, 0), 0.0)
    hin_x = jnp.concatenate([hin_xf, hin_xb], axis=1).astype(BF16)
    yx_ref[0] = _dot(ux, tmat) + _dot_nt(hin_x, gcat)
    hc_f = [sxf[i * ncx + ncx - 1:i * ncx + ncx, :] for i in range(nb)]
    hc_b = [sxb[i * ncx:i * ncx + 1, :] for i in range(nb)]
    ul = ul_ref[0].astype(BF16)
    hl = _dot(ul, ecat)
    rml = rowmod(ncl)
    rl = nb * ncl
    xf = jnp.where(rml == 0, by_batch(hc_f, ncl), pltpu.roll(hl[:, :2 * n], 1, 0))
    xb = jnp.where(rml == ncl - 1, by_batch(hc_b, ncl), pltpu.roll(hl[:, 2 * n:], rl - 1, 0))
    hin_f = _seg_scan(xf, *f["lam_l"], rml, ncl, False)
    hin_b = _seg_scan(xb, *b["lam_l"], rml, ncl, True)
    hin = jnp.concatenate([hin_f, hin_b], axis=1).astype(BF16)
    yl_ref[0] = _dot(ul, tmat) + _dot_nt(hin, gcat)


def _s5_call(ul, ux, lre, lim, ls, bre, bim, cre, cim, *, nb):
    g, rl, w = ul.shape
    rx = ux.shape[1]
    grp = lambda a: pl.BlockSpec((1,) + a.shape[1:], lambda i: (i,) + (0,) * (a.ndim - 1))
    return pl.pallas_call(
        functools.partial(_s5_kernel, nb=nb, ncl=rl // nb, ncx=rx // nb),
        grid=(g,),
        in_specs=[grp(a) for a in (ul, ux, lre, lim, ls, bre, bim, cre, cim)],
        out_specs=[grp(ul), grp(ux)],
        out_shape=[jax.ShapeDtypeStruct(ul.shape, F32), jax.ShapeDtypeStruct(ux.shape, F32)],
        scratch_shapes=[pltpu.VMEM((w, w), BF16)],
        compiler_params=_params(("arbitrary",)),
        name="s5",
    )(ul, ux, lre, lim, ls, bre, bim, cre, cim)


def _outproj_kernel(x_ref, gt_ref, yf_ref, yb_ref, z_ref, ga_ref, wa_ref, s5_ref, u_ref, ng_ref,
                    d5_ref, wglu_ref, bglu_ref, wout_ref, o_ref, y5_sc):
    ys = (yf_ref[...].astype(F32) + yb_ref[...].astype(F32)) * _silu(z_ref[...].astype(F32))
    ys = ys * lax.rsqrt(jnp.mean(ys * ys, axis=-1, keepdims=True) + NORM_EPS) * ng_ref[...]
    n_rest = wout_ref.shape[0] - D_S5
    part = _dot(jnp.concatenate([ys.astype(BF16), ga_ref[...], wa_ref[...]], axis=1), wout_ref[:n_rest, :])
    gpl = LANES // S5_GROUP_CH
    nck = y5_sc.shape[1] // S5_CHUNK
    for s in range(S5_CHUNK):
        for half in range(D_S5 // LANES):
            y5_sc[half, pl.ds(s, nck, stride=S5_CHUNK), :] = jnp.concatenate(
                [s5_ref[half * gpl + j, :, s * S5_GROUP_CH:(s + 1) * S5_GROUP_CH] for j in range(gpl)], axis=1)
    y5 = jnp.concatenate([y5_sc[half] for half in range(D_S5 // LANES)], axis=1)
    t = jax.nn.gelu(y5 + d5_ref[...] * u_ref[...])
    t = _dot(t.astype(BF16), wglu_ref[...]) + bglu_ref[...]
    s5o = t[:, :D_S5] * jax.nn.sigmoid(t[:, D_S5:])
    o_ref[...] = x_ref[...] + gt_ref[0] * (part + _dot(s5o.astype(BF16), wout_ref[n_rest:, :]))


def _outproj_call(x2, gate, yf, yb, z, ga, wa, s5, u, ng, d5, wglu, bglu, wout, *, tm, rows_per_seg):
    r, d = x2.shape
    tps = rows_per_seg // tm
    row = lambda a: pl.BlockSpec((tm, a.shape[1]), lambda i: (i, 0))
    seg = pl.BlockSpec((1, 1, d), lambda i: (i // tps, 0, 0))
    full = lambda a: pl.BlockSpec(a.shape, lambda i: (0,) * a.ndim)
    s5_spec = pl.BlockSpec((s5.shape[0], tm // S5_CHUNK, s5.shape[2]), lambda i: (0, i, 0))
    return pl.pallas_call(
        _outproj_kernel,
        grid=(r // tm,),
        in_specs=[row(x2), seg] + [row(a) for a in (yf, yb, z, ga, wa)] + [s5_spec, row(u)]
                 + [full(a) for a in (ng, d5, wglu, bglu, wout)],
        out_specs=row(x2),
        out_shape=jax.ShapeDtypeStruct(x2.shape, F32),
        scratch_shapes=[pltpu.VMEM((D_S5 // LANES, tm, LANES), F32)],
        compiler_params=_params(("arbitrary",)),
        name="outproj",
    )(x2, gate, yf, yb, z, ga, wa, s5, u, ng, d5, wglu, bglu, wout)


def _ffn_kernel(x_ref, xp_ref, xn_ref, sh_ref, sc_ref, gt_ref, g_ref, wgu_ref, cw_ref, cb_ref,
                wd_ref, gfin_ref, o_ref, *, tiles_per_seq, d_ff, final_norm):
    i = pl.program_id(0)
    tm = x_ref.shape[0]
    x = x_ref[...]
    g, sh, sc = g_ref[...], sh_ref[0], sc_ref[0]
    h = _rms_mod(x, g, sh, sc)
    he = jnp.concatenate([_rms_mod(xp_ref[...], g, sh, sc), h, _rms_mod(xn_ref[...], g, sh, sc)], axis=0)
    ge = _dot(he.astype(BF16), wgu_ref[:, :d_ff])
    upj = _dot(h.astype(BF16), wgu_ref[:, d_ff:])
    t = i % tiles_per_seq
    row = lax.broadcasted_iota(jnp.int32, (tm, 1), 0)
    up = jnp.where((row == 0) & (t == 0), 0.0, ge[HALO - 1:HALO - 1 + tm])
    dn = jnp.where((row == tm - 1) & (t == tiles_per_seq - 1), 0.0, ge[HALO + 1:HALO + 1 + tm])
    cw = cw_ref[...]
    conv = cw[0:1] * up + cw[1:2] * ge[HALO:HALO + tm] + cw[2:3] * dn + cb_ref[...]
    act = (_silu(conv) * upj).astype(BF16)
    y = x + gt_ref[0] * _dot(act, wd_ref[...])
    if final_norm:
        y = y * lax.rsqrt(jnp.mean(y * y, axis=-1, keepdims=True) + NORM_EPS) * gfin_ref[...]
    o_ref[...] = y


def _ffn_call(x2, shift, scale, gate, g, wgu, cw, cb, wd, gfin, *, tm, rows_per_seg, seq, final_norm):
    r, d = x2.shape
    d_ff = wd.shape[0]
    tps = rows_per_seg // tm
    hb = tm // HALO
    last = r // HALO - 1
    seg = pl.BlockSpec((1, 1, d), lambda i: (i // tps, 0, 0))
    full = lambda a: pl.BlockSpec(a.shape, lambda i: (0,) * a.ndim)
    once = lambda a: pl.BlockSpec(a.shape, lambda i: (0,) * a.ndim, pipeline_mode=pl.Buffered(1))
    return pl.pallas_call(
        functools.partial(_ffn_kernel, tiles_per_seq=seq // tm, d_ff=d_ff, final_norm=final_norm),
        grid=(r // tm,),
        in_specs=[pl.BlockSpec((tm, d), lambda i: (i, 0)),
                  pl.BlockSpec((HALO, d), lambda i: (jnp.maximum(i * hb - 1, 0), 0)),
                  pl.BlockSpec((HALO, d), lambda i: (jnp.minimum(i * hb + hb, last), 0)),
                  seg, seg, seg, full(g), once(wgu), full(cw), full(cb), once(wd), full(gfin)],
        out_specs=pl.BlockSpec((tm, d), lambda i: (i, 0)),
        out_shape=jax.ShapeDtypeStruct(x2.shape, F32),
        compiler_params=_params(("arbitrary",)),
        name="conv_ffn",
    )(x2, x2, x2, shift, scale, gate, g, wgu, cw, cb, wd, gfin)


def _rope_tables(seq):
    rows = seq // GRID_W
    r = jnp.broadcast_to(jnp.arange(rows, dtype=F32)[:, None], (rows, GRID_W)).reshape(-1)
    c = jnp.broadcast_to(jnp.arange(GRID_W, dtype=F32)[None, :], (rows, GRID_W)).reshape(-1)
    n_freq = HEAD_DIM // 4
    inv = ROPE_THETA ** (-jnp.arange(n_freq, dtype=F32) / n_freq)
    ang = jnp.concatenate([r[:, None] * inv, c[:, None] * inv], axis=-1)
    cos, sin, zero = jnp.cos(ang), jnp.sin(ang), jnp.zeros_like(ang)
    reps = LANES // HEAD_DIM
    return (jnp.tile(jnp.concatenate([cos, cos], axis=-1), (1, reps)),
            jnp.tile(jnp.concatenate([-sin, zero], axis=-1), (1, reps)),
            jnp.tile(jnp.concatenate([zero, sin], axis=-1), (1, reps)))


def _pad_w_in(w):
    offs = [0]
    for s in IN_SIZES:
        offs.append(offs[-1] + s)
    parts = [w[:, offs[j]:offs[j + 1]] for j in range(len(IN_SIZES))]
    parts[2] = jnp.pad(parts[2], ((0, 0), (0, LANES - IN_SIZES[2])))
    return jnp.concatenate(parts, axis=1).astype(BF16)


def kernel(x, c, ctx, c_ctx, w_mod, b_mod, g_mix, w_in, ssd_conv_w, ssd_conv_b, ssd_a_log, ssd_dt_bias, ssd_d, ssd_norm_g, ga_q_norm, ga_k_norm, wa_sink, s5_lambda_re, s5_lambda_im, s5_log_step, s5_b_re, s5_b_im, s5_c_re, s5_c_im, s5_d, s5_w_glu, s5_b_glu, w_out, g_ffn, w_gate, w_up, ffn_conv_w, ffn_conv_b, w_down, g_final):
    b, s, d = x.shape
    lc = ctx.shape[1]
    depth = w_mod.shape[0]
    hpg = 2
    tm = 512

    cc = jnp.zeros((8, d), F32).at[:b].set(c).at[b].set(c_ctx)
    mod = _mod_call(cc, w_mod, b_mod).reshape(depth, 8, 6, d)

    tabs_l = _rope_tables(s)
    one, zero = jnp.ones((lc, LANES), F32), jnp.zeros((lc, LANES), F32)
    tabs_c = (one, zero, zero)
    bd = jnp.kron(jnp.eye(LANES // HEAD_DIM, dtype=F32),
                  jnp.full((HEAD_DIM, HEAD_DIM), 1.0 / HEAD_DIM, F32)).astype(BF16)
    gfin = g_final.reshape(1, d)

    x2 = x.reshape(b * s, d)
    xc2 = ctx.reshape(b * lc, d)
    for i in range(depth):
        need_ctx = i < depth - 1
        ml = lambda j: mod[i, :b, j].reshape(b, 1, d)
        mc = lambda j: mod[i, b, j].reshape(1, 1, d)
        g_i = g_mix[i].reshape(1, d)
        w_pad = _pad_w_in(w_in[i])
        qn = jnp.tile(ga_q_norm[i], LANES // HEAD_DIM).reshape(1, LANES)
        kn = jnp.tile(ga_k_norm[i], LANES // HEAD_DIM).reshape(1, LANES)
        conv_w = ssd_conv_w[i]
        conv_b = ssd_conv_b[i].reshape(1, -1)
        pl_ = _inproj_call(x2, ml(0), ml(1), g_i, w_pad, tabs_l, qn, kn, bd, conv_w, conv_b,
                           tm=tm, rows_per_seg=s, seq=s)
        pc_ = _inproj_call(xc2, mc(0), mc(1), g_i, w_pad, tabs_c, qn, kn, bd, conv_w, conv_b,
                           tm=lc, rows_per_seg=b * lc, seq=lc)
        zl, xsl, btl, cml, dtl, gql, gkl, gvl, wql, wkl, wvl, ul, u5l = pl_
        zc, xsc, btc, cmc, dtc, gqc, gkc, gvc, wqc, wkc, wvc, uc, u5c = pc_

        alog_c = ssd_a_log[i].reshape(-1, 1)
        bias_c = ssd_dt_bias[i].reshape(-1, 1)
        dsk = jnp.repeat(ssd_d[i], HEAD_DIM).reshape(1, D_SSD)

        def ssd(xs2, bt, cm2, dt_t, h0, n):
            return _ssd_call(xs2.reshape(b, n, -1), bt, cm2.reshape(b, n, -1), dt_t, h0, alog_c, bias_c, dsk)

        h0 = jnp.zeros((b, N_DIRS, SSD_GROUPS, SSD_STATE, hpg * HEAD_DIM), F32)
        yfc, ybc, hc = ssd(xsc, btc, cmc, dtc, h0, lc)
        yfl, ybl, _ = ssd(xsl, btl, cml, dtl, hc, s)

        kx_ga = gkc.reshape(b, lc, -1)
        y_ga = _flash_call(gql, gkl.reshape(b, s, -1), gvl, None, extra=(kx_ga, gvc))

        kx_wa = wkc.reshape(b, lc, -1)
        sink = wa_sink[i].astype(F32)
        y_wa = _win_call(wql, wkl.reshape(b, s, -1), wvl, kx_wa, wvc, sink)

        gm = lambda a: jnp.moveaxis(a, 1, 0)
        lre = gm(s5_lambda_re[i])[:, :, None, :]
        lim = gm(s5_lambda_im[i])[:, :, None, :]
        ls = gm(s5_log_step[i])[:, :, None, None]
        bre = gm(s5_b_re[i]).transpose(0, 1, 3, 2)
        bim = gm(s5_b_im[i]).transpose(0, 1, 3, 2)
        cre, cim = gm(s5_c_re[i]), gm(s5_c_im[i])
        y5l, y5c = _s5_call(u5l, u5c, lre, lim, ls, bre, bim, cre, cim, nb=b)

        ng = ssd_norm_g[i].reshape(1, D_SSD)
        d5 = s5_d[i].reshape(1, D_S5)
        wglu = s5_w_glu[i].astype(BF16)
        bglu = s5_b_glu[i].reshape(1, -1)
        wout = w_out[i].astype(BF16)
        x2 = _outproj_call(x2, ml(2), yfl.reshape(b * s, -1), ybl.reshape(b * s, -1), zl, y_ga, y_wa,
                           y5l, ul, ng, d5, wglu, bglu, wout, tm=tm, rows_per_seg=s)

        gf_i = g_ffn[i].reshape(1, d)
        wgu = jnp.concatenate([w_gate[i], w_up[i]], axis=1).astype(BF16)
        cw = ffn_conv_w[i]
        cb = ffn_conv_b[i].reshape(1, -1)
        wd = w_down[i].astype(BF16)
        x2 = _ffn_call(x2, ml(3), ml(4), ml(5), gf_i, wgu, cw, cb, wd, gfin, tm=512, rows_per_seg=s,
                       seq=s, final_norm=not need_ctx)

        if need_ctx:
            yc_ga = _flash_call(gqc, kx_ga, gvc, None)
            yc_wa = _flash_call(wqc, kx_wa, wvc, sink)
            xc2 = _outproj_call(xc2, mc(2), yfc.reshape(b * lc, -1), ybc.reshape(b * lc, -1), zc, yc_ga,
                                yc_wa, y5c, uc, ng, d5, wglu, bglu, wout, tm=lc, rows_per_seg=b * lc)
            xc2 = _ffn_call(xc2, mc(3), mc(4), mc(5), gf_i, wgu, cw, cb, wd, gfin, tm=lc,
                            rows_per_seg=b * lc, seq=lc, final_norm=False)
    return x2.reshape(b, s, d)
```
